```python
import math
import jax, jax.numpy as jnp
from jax import lax
import numpy as np

D_MODEL = 1024
BATCH = 32
SEQ = 256
DEPTH = 2
DEC_BATCH = 4
DEC_SEQ = 2048
PAST_LEN = 256

GRID_W = 64
CHUNK = 128
N_EVEN = (DEPTH + 1) // 2
N_ODD = DEPTH // 2
D_SSD = D_MODEL
SSD_HEAD = 64
H_SSD = D_SSD // SSD_HEAD
SSD_GROUPS = 2
D_STATE = 128
CONV_W = 4
XBC = D_SSD + 2 * SSD_GROUPS * D_STATE
D_SG = D_MODEL
SG_GROUPS = 8
SG_CH = D_SG // SG_GROUPS
IN_A = D_SSD + XBC + 2 * H_SSD + 2 * D_SG
MIX_A = D_SSD + D_SG
H_RET = 4
DK_RET = D_MODEL // H_RET
DV_RET = 2 * DK_RET
IN_C = 2 * H_RET * DK_RET + 2 * H_RET * DV_RET
MIX_C = H_RET * DV_RET
ROPE_BASE = 10000.0
N_EXPERTS = 32
TOP_K = 4
D_FF = D_MODEL
SWIGLU_LIMIT = 7.0
SWIGLU_ALPHA = 1.702
MOE_BLOCK = 128
EPS = 1e-6

kernel_name = 'hybrid_ssd_sgu_retention_moe_diffusion_step'


def rms_norm(x, g):
    xf = x.astype(jnp.float32)
    y = xf * lax.rsqrt(jnp.mean(xf * xf, axis=-1, keepdims=True) + EPS)
    return (y * g.astype(jnp.float32)).astype(x.dtype)


def rms_plain(xf):
    return xf * lax.rsqrt(jnp.mean(xf * xf, axis=-1, keepdims=True) + EPS)


def layer_norm(x):
    xf = x.astype(jnp.float32)
    mu = jnp.mean(xf, axis=-1, keepdims=True)
    xc = xf - mu
    return (xc * lax.rsqrt(jnp.mean(xc * xc, axis=-1, keepdims=True) + 1e-5)).astype(x.dtype)


def chunked_scan(q, k, v, log_a, s0):
    out_dtype = v.dtype
    q, k, v = (t.astype(jnp.float32) for t in (q, k, v))
    b, L, h, dk = q.shape
    dv = v.shape[-1]
    n = L // CHUNK
    qc = q.reshape(b, n, CHUNK, h, dk)
    kc = k.reshape(b, n, CHUNK, h, dk)
    vc = v.reshape(b, n, CHUNK, h, dv)
    cum = jnp.cumsum(log_a.astype(jnp.float32).reshape(b, n, CHUNK, h), axis=2)
    cum_t = jnp.swapaxes(cum, 2, 3)
    lower = jnp.tril(jnp.ones((CHUNK, CHUNK), dtype=bool))
    decay = jnp.exp(jnp.where(lower, cum_t[..., :, None] - cum_t[..., None, :], -jnp.inf))
    scores = jnp.einsum('bcihk,bcjhk->bchij', qc, kc) * decay
    y = jnp.einsum('bchij,bcjhv->bcihv', scores, vc)
    k_w = kc * jnp.exp(cum[:, :, -1:, :] - cum)[..., None]
    contrib = jnp.einsum('bcjhk,bcjhv->bchkv', k_w, vc)
    chunk_decay = jnp.exp(cum[:, :, -1, :])

    def step(s, inp):
        d, ctb = inp
        return d[..., None, None] * s + ctb, s

    s_final, s_prev = lax.scan(step, s0.astype(jnp.float32),
                               (jnp.moveaxis(chunk_decay, 1, 0), jnp.moveaxis(contrib, 1, 0)))
    s_prev = jnp.moveaxis(s_prev, 0, 1)
    y = y + jnp.einsum('bcihk,bchkv->bcihv', qc, s_prev) * jnp.exp(cum)[..., None]
    return y.reshape(b, L, h, dv).astype(out_dtype), s_final.astype(out_dtype)


def rev(t):
    return jnp.flip(t, axis=1)


def centred_dwconv(x, w, bias):
    y = lax.conv_general_dilated(x, w[:, None, :], window_strides=(1,), padding=[(2, 1)],
                                 dimension_numbers=('NWC', 'WIO', 'NWC'),
                                 feature_group_count=x.shape[-1])
    return y + bias


def ssd_sgu_mixer(h, s0, w_in, conv_w, conv_b, dt_bias, a_log, d_skip, ssd_norm, w_sp, b_sp, w_out):
    bsz, L, _ = h.shape
    proj = h @ w_in
    o1 = D_SSD
    o2 = o1 + XBC
    o3 = o2 + 2 * H_SSD
    o4 = o3 + D_SG
    z, xbc, dt_raw, u, v = jnp.split(proj, [o1, o2, o3, o4], axis=-1)
    xbc = jax.nn.silu(centred_dwconv(xbc, conv_w, conv_b))
    xs, b_in, c_in = jnp.split(xbc, [D_SSD, D_SSD + SSD_GROUPS * D_STATE], axis=-1)
    xs = xs.reshape(bsz, L, H_SSD, SSD_HEAD)
    rep = H_SSD // SSD_GROUPS
    k = jnp.repeat(b_in.reshape(bsz, L, SSD_GROUPS, D_STATE), rep, axis=2)
    q = jnp.repeat(c_in.reshape(bsz, L, SSD_GROUPS, D_STATE), rep, axis=2)
    dt = jax.nn.softplus(dt_raw.astype(jnp.float32).reshape(bsz, L, 2, H_SSD) + dt_bias.astype(jnp.float32))
    log_a = dt * (-jnp.exp(a_log.astype(jnp.float32)))
    y_f, s_f = chunked_scan(q, k, xs * dt[:, :, 0, :, None], log_a[:, :, 0], s0[:, 0])
    y_b, s_b = chunked_scan(rev(q), rev(k), rev(xs * dt[:, :, 1, :, None]), rev(log_a[:, :, 1]), s0[:, 1])
    y = (y_f.astype(jnp.float32) + rev(y_b).astype(jnp.float32)
         + d_skip.astype(jnp.float32)[:, None] * xs.astype(jnp.float32))
    y = y.reshape(bsz, L, D_SSD) * jax.nn.silu(z.astype(jnp.float32))
    y = rms_plain(y.reshape(bsz, L, SSD_GROUPS, D_SSD // SSD_GROUPS)).reshape(bsz, L, D_SSD)
    y = (y * ssd_norm.astype(jnp.float32)).astype(h.dtype)
    u = jax.nn.gelu(u)
    v = layer_norm(jax.nn.gelu(v))
    vb = v.reshape(bsz, L // CHUNK, CHUNK, SG_GROUPS, SG_CH)
    vmix = jnp.einsum('gpq,bcqgd->bcpgd', w_sp, vb) + b_sp.T[:, :, None]
    sgu = u * vmix.reshape(bsz, L, D_SG)
    out = jnp.concatenate([y, sgu], axis=-1) @ w_out
    return out, jnp.stack([s_f, s_b], axis=1)


def grid_rope(n_tok):
    rows = n_tok // GRID_W
    row = jnp.repeat(jnp.arange(rows), GRID_W).astype(jnp.float32)
    col = jnp.tile(jnp.arange(GRID_W), rows).astype(jnp.float32)
    n_freq = DK_RET // 4
    inv = ROPE_BASE ** (-jnp.arange(n_freq, dtype=jnp.float32) / n_freq)
    ang = jnp.concatenate([row[:, None] * inv, col[:, None] * inv], axis=-1)
    return (jnp.cos(ang), jnp.sin(ang))


def apply_rope(x, cos, sin):
    x1, x2 = jnp.split(x.astype(jnp.float32), 2, axis=-1)
    cs = cos[:, None, :]
    sn = sin[:, None, :]
    return jnp.concatenate([x1 * cs - x2 * sn, x2 * cs + x1 * sn], axis=-1).astype(x.dtype)


def retention_mixer(h, s0, rope, w_in, decay_logit, w_out):
    bsz, L, _ = h.shape
    hk = H_RET * DK_RET
    hv = H_RET * DV_RET
    q, k, v, g = jnp.split(h @ w_in, [hk, 2 * hk, 2 * hk + hv], axis=-1)
    q = q.reshape(bsz, L, H_RET, DK_RET)
    k = k.reshape(bsz, L, H_RET, DK_RET) * (DK_RET ** -0.5)
    v = v.reshape(bsz, L, H_RET, DV_RET)
    if rope is not None:
        q = apply_rope(q, rope[0], rope[1])
        k = apply_rope(k, rope[0], rope[1])
    log_g = jax.nn.log_sigmoid(decay_logit.astype(jnp.float32))
    la_f = jnp.broadcast_to(log_g[0], (bsz, L, H_RET))
    la_b = jnp.broadcast_to(log_g[1], (bsz, L, H_RET))
    y_f, s_f = chunked_scan(q, k, v, la_f, s0[:, 0])
    y_b, s_b = chunked_scan(rev(q), rev(k), rev(v), la_b, s0[:, 1])
    y = rms_plain(y_f.astype(jnp.float32) + rev(y_b).astype(jnp.float32))
    y = y.reshape(bsz, L, MIX_C).astype(h.dtype)
    out = (jax.nn.silu(g) * y) @ w_out
    return out, jnp.stack([s_f, s_b], axis=1)


def moe(h, w_router, b_router, w_gu, b_gu, w_down, b_down):
    bsz, L, d = h.shape
    xf = h.reshape(-1, d)
    t = xf.shape[0]
    tk = t * TOP_K
    logits = (xf @ w_router + b_router).astype(jnp.float32)
    top_val, top_idx = lax.top_k(logits, TOP_K)
    gates = jax.nn.softmax(top_val, axis=-1)
    e_flat = top_idx.reshape(-1)
    tok_flat = jnp.repeat(jnp.arange(t, dtype=jnp.int32), TOP_K)
    g_flat = gates.reshape(-1)
    order = jnp.argsort(e_flat)
    e_sorted = e_flat[order]
    counts = jnp.bincount(e_flat, length=N_EXPERTS)
    padded = (counts + MOE_BLOCK - 1) // MOE_BLOCK * MOE_BLOCK
    pad_end = jnp.cumsum(padded)
    pad_start = pad_end - padded
    start = jnp.cumsum(counts) - counts
    dest = pad_start[e_sorted] + jnp.arange(tk) - start[e_sorted]
    n_rows = (tk + MOE_BLOCK - 1) // MOE_BLOCK * MOE_BLOCK + N_EXPERTS * MOE_BLOCK
    n_blk = n_rows // MOE_BLOCK
    row_tok = jnp.zeros((n_rows,), jnp.int32).at[dest].set(tok_flat[order])
    row_gate = jnp.zeros((n_rows,), jnp.float32).at[dest].set(g_flat[order])
    blk_expert = jnp.minimum(jnp.searchsorted(pad_end, jnp.arange(n_blk) * MOE_BLOCK, side='right'),
                             N_EXPERTS - 1)

    def expert_block(args):
        toks, e = args
        hb = xf[toks] @ w_gu[e] + b_gu[e]
        glu = jnp.minimum(hb[:, 0::2], SWIGLU_LIMIT)
        lin = jnp.clip(hb[:, 1::2], -SWIGLU_LIMIT, SWIGLU_LIMIT)
        act = glu * jax.nn.sigmoid(SWIGLU_ALPHA * glu) * (lin + 1.0)
        return act @ w_down[e] + b_down[e]

    y_rows = lax.map(expert_block, (row_tok.reshape(n_blk, MOE_BLOCK), blk_expert))
    y_rows = (row_gate[:, None] * y_rows.reshape(n_rows, d).astype(jnp.float32)).astype(xf.dtype)
    out = jnp.zeros_like(xf).at[row_tok].add(y_rows)
    return out.reshape(bsz, L, d)


def trunk(x, cvec, s_ssd, s_ret, rope, w_mod, b_mod, norm_mix, norm_ffn, w_in_a, conv_w, conv_b,
          dt_bias, a_log, d_skip, ssd_norm, w_sp, b_sp, w_out_a, w_in_c, decay_logit, w_out_c,
          w_router, b_router, w_gu, b_gu, w_down, b_down, norm_final):
    new_ssd = []
    new_ret = []
    for l in range(DEPTH):
        m = (jax.nn.silu(cvec) @ w_mod[l] + b_mod[l])[..., None, :]
        sh1, sc1, g1, sh2, sc2, g2 = jnp.split(m, 6, axis=-1)
        h = rms_norm(x, norm_mix[l]) * (1.0 + sc1) + sh1
        i = l // 2
        if l % 2 == 0:
            out, st = ssd_sgu_mixer(h, s_ssd[:, i], w_in_a[i], conv_w[i], conv_b[i], dt_bias[i], a_log[i],
                                    d_skip[i], ssd_norm[i], w_sp[i], b_sp[i], w_out_a[i])
            new_ssd.append(st)
        else:
            out, st = retention_mixer(h, s_ret[:, i], rope, w_in_c[i], decay_logit[i], w_out_c[i])
            new_ret.append(st)
        x = x + g1 * out
        h = rms_norm(x, norm_ffn[l]) * (1.0 + sc2) + sh2
        x = x + g2 * moe(h, w_router[l], b_router[l], w_gu[l], b_gu[l], w_down[l], b_down[l])
    return rms_norm(x, norm_final), jnp.stack(new_ssd, axis=1), jnp.stack(new_ret, axis=1)


def setup_inputs(seed: int = 0) -> dict:
    key = jax.random.key(seed)
    ks = iter(jax.random.split(key, 48))
    f32 = jnp.float32

    def nrm(shape, scale):
        return jax.random.normal(next(ks), shape, f32) * scale

    dt0 = jnp.exp(jax.random.uniform(next(ks), (N_EVEN, 2, H_SSD), f32,
                                     minval=math.log(1e-3), maxval=math.log(1e-1)))
    dt_bias = dt0 + jnp.log(-jnp.expm1(-dt0))
    a_log = jnp.log(jax.random.uniform(next(ks), (N_EVEN, 2, H_SSD), f32, minval=1.0, maxval=16.0))
    gam = 1.0 - 2.0 ** (-5.0 - jnp.arange(H_RET, dtype=f32))
    decay_logit = (jnp.log(gam) - jnp.log1p(-gam))[None, None, :] + nrm((N_ODD, 2, H_RET), 0.1)
    return {
        'x_prompt': nrm((BATCH, SEQ, D_MODEL), 1.0),
        'x_sample': nrm((DEC_BATCH, DEC_SEQ, D_MODEL), 1.0),
        'state_ssd': nrm((DEC_BATCH, N_EVEN, 2, H_SSD, D_STATE, SSD_HEAD), 0.5),
        'state_ret': nrm((DEC_BATCH, N_ODD, 2, H_RET, DK_RET, DV_RET), 0.3),
        'c': nrm((DEC_BATCH, D_MODEL), 1.0),
        'c_ctx': nrm((D_MODEL,), 1.0),
        'w_mod': nrm((DEPTH, D_MODEL, 6 * D_MODEL), 0.5 * D_MODEL ** -0.5),
        'b_mod': nrm((DEPTH, 6 * D_MODEL), 0.02),
        'norm_mix': 1.0 + nrm((DEPTH, D_MODEL), 0.02),
        'norm_ffn': 1.0 + nrm((DEPTH, D_MODEL), 0.02),
        'w_in_a': nrm((N_EVEN, D_MODEL, IN_A), D_MODEL ** -0.5),
        'conv_w': nrm((N_EVEN, CONV_W, XBC), CONV_W ** -0.5),
        'conv_b': nrm((N_EVEN, XBC), 0.02),
        'dt_bias': dt_bias,
        'a_log': a_log,
        'd_skip': 1.0 + nrm((N_EVEN, H_SSD), 0.02),
        'ssd_norm': 1.0 + nrm((N_EVEN, D_SSD), 0.02),
        'w_sp': nrm((N_EVEN, SG_GROUPS, CHUNK, CHUNK), CHUNK ** -0.5),
        'b_sp': 1.0 + nrm((N_EVEN, SG_GROUPS, CHUNK), 0.02),
        'w_out_a': nrm((N_EVEN, MIX_A, D_MODEL), MIX_A ** -0.5),
        'w_in_c': nrm((N_ODD, D_MODEL, IN_C), D_MODEL ** -0.5),
        'decay_logit': decay_logit,
        'w_out_c': nrm((N_ODD, MIX_C, D_MODEL), MIX_C ** -0.5),
        'w_router': nrm((DEPTH, D_MODEL, N_EXPERTS), D_MODEL ** -0.5),
        'b_router': nrm((DEPTH, N_EXPERTS), 0.01),
        'w_gu': nrm((DEPTH, N_EXPERTS, D_MODEL, 2 * D_FF), D_MODEL ** -0.5),
        'b_gu': nrm((DEPTH, N_EXPERTS, 2 * D_FF), 0.02),
        'w_down': nrm((DEPTH, N_EXPERTS, D_FF, D_MODEL), D_FF ** -0.5),
        'b_down': nrm((DEPTH, N_EXPERTS, D_MODEL), 0.02),
        'norm_final': 1.0 + nrm((D_MODEL,), 0.02),
    }


def reference(x_prompt, x_sample, state_ssd, state_ret, c, c_ctx, w_mod, b_mod, norm_mix, norm_ffn,
              w_in_a, conv_w, conv_b, dt_bias, a_log, d_skip, ssd_norm, w_sp, b_sp, w_out_a,
              w_in_c, decay_logit, w_out_c, w_router, b_router, w_gu, b_gu, w_down, b_down, norm_final):
    weights = (w_mod, b_mod, norm_mix, norm_ffn, w_in_a, conv_w, conv_b, dt_bias, a_log, d_skip,
               ssd_norm, w_sp, b_sp, w_out_a, w_in_c, decay_logit, w_out_c, w_router, b_router,
               w_gu, b_gu, w_down, b_down, norm_final)
    n_req = x_prompt.shape[0]
    zero_ssd = jnp.zeros((n_req,) + state_ssd.shape[1:], x_prompt.dtype)
    zero_ret = jnp.zeros((n_req,) + state_ret.shape[1:], x_prompt.dtype)
    y_prompt, new_state_ssd, new_state_ret = trunk(x_prompt, c_ctx, zero_ssd, zero_ret, None, *weights)
    rope = grid_rope(x_sample.shape[1])
    y_sample, _, _ = trunk(x_sample, c, state_ssd, state_ret, rope, *weights)
    return (y_prompt, y_sample, new_state_ssd, new_state_ret)
```

```python
import functools
import math

import jax
import jax.numpy as jnp
from jax import lax
from jax.experimental import pallas as pl
from jax.experimental.pallas import tpu as pltpu

D_MODEL = 1024
GRID_W = 64
CHUNK = 128
SSD_HEAD = 64
SSD_GROUPS = 2
D_STATE = 128
GROUP_W = D_MODEL // SSD_GROUPS
HEADS_PER_GROUP = GROUP_W // SSD_HEAD
SG_GROUPS = 8
H_RET = 4
DK_RET = D_MODEL // H_RET
DV_RET = 2 * DK_RET
ROPE_BASE = 10000.0
N_EXPERTS = 32
TOP_K = 4
SWIGLU_LIMIT = 7.0
SWIGLU_ALPHA = 1.702
EPS = 1e-6

LANES = 128
MOD_ROWS = 8
MOE_ROWS = 256
VMEM_LIMIT = 56 * 1024 * 1024

F32 = jnp.float32
BF16 = jnp.bfloat16
HI = lax.Precision.HIGHEST


def _cparams(sem):
    return pltpu.CompilerParams(dimension_semantics=sem, vmem_limit_bytes=VMEM_LIMIT)


def _silu(x):
    return x * (1.0 / (1.0 + jnp.exp(-x)))


def _gelu_tanh(x):
    return 0.5 * x * (1.0 + jnp.tanh(math.sqrt(2.0 / math.pi) * (x + 0.044715 * (x * x * x))))


def _softplus(x):
    return jnp.maximum(x, 0.0) + jnp.log(1.0 + jnp.exp(-jnp.abs(x)))


def _mod_kernel(c_ref, w_ref, b_ref, o_ref):
    a = _silu(c_ref[...])
    o_ref[...] = jnp.dot(a, w_ref[...], precision=HI, preferred_element_type=F32) + b_ref[...]


def _modulation(cvecs, w_mod, b_mod):
    depth, d, n = w_mod.shape
    tn = 1536
    return pl.pallas_call(
        _mod_kernel,
        grid=(depth, n // tn),
        in_specs=[
            pl.BlockSpec((MOD_ROWS, d), lambda l, j: (0, 0)),
            pl.BlockSpec((None, d, tn), lambda l, j: (l, 0, j)),
            pl.BlockSpec((None, 1, tn), lambda l, j: (l, 0, j)),
        ],
        out_specs=pl.BlockSpec((None, MOD_ROWS, tn), lambda l, j: (l, 0, j)),
        out_shape=jax.ShapeDtypeStruct((depth, MOD_ROWS, n), F32),
        compiler_params=_cparams(("arbitrary", "arbitrary")),
        name="modulation",
    )(cvecs, w_mod, b_mod.reshape(depth, 1, n))


class _Tokens:
    def __init__(self, n_ctx_seq, ctx_len, n_dec_seq, dec_len):
        self.n_ctx_seq, self.ctx_len = n_ctx_seq, ctx_len
        self.n_dec_seq, self.dec_len = n_dec_seq, dec_len
        self.t_ctx = n_ctx_seq * ctx_len
        self.t_dec = n_dec_seq * dec_len
        self.total = self.t_ctx + self.t_dec

    def group_of_tile(self, bm):
        assert self.t_ctx % bm == 0 and self.dec_len % bm == 0
        n_ctx_tiles = self.t_ctx // bm
        per_seq = self.dec_len // bm

        def group(i):
            return jnp.where(i < n_ctx_tiles, 0, 1 + (i - n_ctx_tiles) // per_seq)

        return group


def _modulated_norm(x, gam, sc, sh):
    ms = jnp.mean(x * x, axis=-1, keepdims=True)
    return (x * lax.rsqrt(ms + EPS) * gam) * (1.0 + sc) + sh


def _proj_kernel(*refs, epilogue, n_extra, tok, bm):
    x_ref, gam_ref, sc_ref, sh_ref, w_ref = refs[:5]
    extra = refs[5:5 + n_extra]
    o_ref = refs[5 + n_extra]
    h_scr = refs[6 + n_extra]
    i = pl.program_id(0)

    @pl.when(pl.program_id(1) == 0)
    def _():
        h_scr[...] = _modulated_norm(x_ref[...], gam_ref[...], sc_ref[...], sh_ref[...]).astype(BF16)

    acc = jnp.dot(h_scr[...], w_ref[...], preferred_element_type=F32)
    o_ref[...] = epilogue(acc, i, tok, bm, *extra).astype(o_ref.dtype)


def _ep_plain(acc, i, tok, bm):
    return acc


def _ep_silu(acc, i, tok, bm):
    return _silu(acc)


def _ep_gelu(acc, i, tok, bm):
    return _gelu_tanh(acc)


def _ep_gelu_ln(acc, i, tok, bm):
    g = _gelu_tanh(acc)
    mu = jnp.mean(g, axis=-1, keepdims=True)
    gc = g - mu
    return gc * lax.rsqrt(jnp.mean(gc * gc, axis=-1, keepdims=True) + 1e-5)


def _ep_conv_silu(acc, i, tok, bm, cw_ref, cb_ref):
    n = acc.shape[0]
    seq = jnp.where(i * bm < tok.t_ctx, tok.ctx_len, tok.dec_len)
    t = lax.broadcasted_iota(jnp.int32, (n, 1), 0) & (seq - 1)
    cw = cw_ref[...]
    y = acc * cw[2:3, :] + cb_ref[...]
    y = y + jnp.where(t >= 2, pltpu.roll(acc, 2, axis=0), 0.0) * cw[0:1, :]
    y = y + jnp.where(t >= 1, pltpu.roll(acc, 1, axis=0), 0.0) * cw[1:2, :]
    y = y + jnp.where(t < seq - 1, pltpu.roll(acc, n - 1, axis=0), 0.0) * cw[3:4, :]
    return _silu(y)


def _ep_rope(acc, i, tok, bm, cos_ref, sin_ref, *, scale):
    cs, sn = cos_ref[...], sin_ref[...]
    half = DK_RET // 2
    outs = []
    for h in range(acc.shape[1] // DK_RET):
        x1 = acc[:, h * DK_RET:h * DK_RET + half] * scale
        x2 = acc[:, h * DK_RET + half:(h + 1) * DK_RET] * scale
        outs.append(x1 * cs - x2 * sn)
        outs.append(x2 * cs + x1 * sn)
    return jnp.concatenate(outs, axis=1)


def _norm_proj(x, gam, sc, sh, w, tok, *, bm, tn, out_dtype, epilogue, extra=(), extra_specs=(), name):
    t, d = x.shape
    n = w.shape[1]
    group = tok.group_of_tile(bm)
    kern = functools.partial(_proj_kernel, epilogue=epilogue, n_extra=len(extra), tok=tok, bm=bm)
    return pl.pallas_call(
        kern,
        grid=(t // bm, n // tn),
        in_specs=[
            pl.BlockSpec((bm, d), lambda i, j: (i, 0)),
            pl.BlockSpec((1, d), lambda i, j: (0, 0)),
            pl.BlockSpec((None, 1, d), lambda i, j: (group(i), 0, 0)),
            pl.BlockSpec((None, 1, d), lambda i, j: (group(i), 0, 0)),
            pl.BlockSpec((d, tn), lambda i, j: (0, j)),
            *extra_specs,
        ],
        out_specs=pl.BlockSpec((bm, tn), lambda i, j: (i, j)),
        out_shape=jax.ShapeDtypeStruct((t, n), out_dtype),
        scratch_shapes=[pltpu.VMEM((bm, d), BF16)],
        compiler_params=_cparams(("arbitrary", "arbitrary")),
        name=name,
    )(x, gam, sc, sh, w, *extra)


def _ssd_kernel(*refs, n_chunks, has_init, has_final):
    xs_ref, b_ref, c_ref, dt_ref, sz_ref, dtb_ref, a_ref, dsk_ref, nrm_ref = refs[:9]
    k = 9
    s0_ref = None
    if has_init:
        s0_ref = refs[k]
        k += 1
    y_ref = refs[k]
    k += 1
    sfin_ref = None
    if has_final:
        sfin_ref = refs[k]
        k += 1
    yacc, s_scr = refs[k], refs[k + 1]

    q = CHUNK
    row = lax.broadcasted_iota(jnp.int32, (q, q), 0)
    col = lax.broadcasted_iota(jnp.int32, (q, q), 1)
    lane = lax.broadcasted_iota(jnp.int32, (1, q), 1)
    left = lane < SSD_HEAD
    dtb = dtb_ref[...]
    a_neg = a_ref[...]
    n_pairs = HEADS_PER_GROUP // 2

    def chunk(ci, direction):
        r0 = pl.multiple_of(ci * q, q)
        xs = xs_ref[pl.ds(r0, q), :]
        bm_ = b_ref[pl.ds(r0, q), :]
        cm = c_ref[pl.ds(r0, q), :]
        dt = _softplus(dt_ref[pl.ds(r0, q), :] + dtb)
        la = dt * a_neg
        if direction == 0:
            tri = (col <= row).astype(F32)
            keep = col <= row
            last = q - 1
        else:
            tri = (col >= row).astype(F32)
            keep = col >= row
            last = 0
        cum = jnp.dot(tri, la, precision=HI, preferred_element_type=F32)
        cum_t = cum.T
        dt_t = dt.T
        tot_t = jnp.broadcast_to(cum_t[:, last:last + 1], (q, q))
        w_t = dt_t * jnp.exp(tot_t - cum_t)
        g = lax.dot_general(cm, bm_, (((1,), (1,)), ((), ())), preferred_element_type=F32)
        b_t = bm_.astype(F32).T
        cm_f = cm.astype(F32)
        outs = []
        for p in range(n_pairs):
            xs_p = xs[:, p * LANES:(p + 1) * LANES]
            s_p = s_scr[:, p * LANES:(p + 1) * LANES]
            s_b = s_p.astype(BF16)
            zero = jnp.zeros_like(xs_p)
            zero_s = jnp.zeros_like(s_b)
            lhs, rhs, lhs_s, rhs_s, decs = [], [], [], [], []
            for hh in range(2):
                cidx = direction * HEADS_PER_GROUP + 2 * p + hh
                cum_b = jnp.broadcast_to(cum[:, cidx:cidx + 1], (q, q))
                dec = jnp.exp(jnp.where(keep, cum_b - cum_t[cidx:cidx + 1, :], -jnp.inf))
                scores = g * dec * dt_t[cidx:cidx + 1, :]
                lhs += [scores.astype(BF16), (cm_f * jnp.exp(cum_b)).astype(BF16)]
                sel = left if hh == 0 else jnp.logical_not(left)
                rhs += [jnp.where(sel, xs_p, zero), jnp.where(sel, s_b, zero_s)]
                lhs_s.append((b_t * w_t[cidx:cidx + 1, :]).astype(BF16))
                rhs_s.append(jnp.where(sel, xs_p, zero))
                decs.append(jnp.exp(cum_t[cidx:cidx + 1, last:last + 1]))
            y_p = jnp.dot(jnp.concatenate(lhs, axis=1), jnp.concatenate(rhs, axis=0),
                          preferred_element_type=F32)
            upd = jnp.dot(jnp.concatenate(lhs_s, axis=1), jnp.concatenate(rhs_s, axis=0),
                          preferred_element_type=F32)
            s_scr[:, p * LANES:(p + 1) * LANES] = s_p * jnp.where(left, decs[0], decs[1]) + upd
            outs.append(y_p)
        return jnp.concatenate(outs, axis=1)

    def init_state(direction):
        if has_init:
            s_scr[...] = s0_ref[direction]
        else:
            s_scr[...] = jnp.zeros_like(s_scr)

    init_state(0)

    def fwd_body(ci, carry):
        r0 = pl.multiple_of(ci * q, q)
        yacc[pl.ds(r0, q), :] = chunk(ci, 0)
        return carry

    lax.fori_loop(0, n_chunks, fwd_body, 0)
    if has_final:
        sfin_ref[0] = s_scr[...]
    init_state(1)

    def bwd_body(k2, carry):
        ci = n_chunks - 1 - k2
        r0 = pl.multiple_of(ci * q, q)
        y = yacc[pl.ds(r0, q), :] + chunk(ci, 1)
        y = y + dsk_ref[...] * xs_ref[pl.ds(r0, q), :].astype(F32)
        y = y * sz_ref[pl.ds(r0, q), :].astype(F32)
        y = y * lax.rsqrt(jnp.mean(y * y, axis=-1, keepdims=True) + EPS)
        y_ref[pl.ds(r0, q), :] = (y * nrm_ref[...]).astype(y_ref.dtype)
        return carry

    lax.fori_loop(0, n_chunks, bwd_body, 0)
    if has_final:
        sfin_ref[1] = s_scr[...]


def _ssd_scan(xbc, dt_raw, sz, dtb, a_neg, dsk, nrm, s0, *, row0, n_seq, seq_len, want_final):
    assert row0 % seq_len == 0
    b0 = row0 // seq_len
    has_init = s0 is not None
    gw = GROUP_W
    n_xs = D_MODEL // gw
    b_blk0 = D_MODEL // D_STATE
    c_blk0 = b_blk0 + SSD_GROUPS
    in_specs = [
        pl.BlockSpec((seq_len, gw), lambda b, g: (b0 + b, g)),
        pl.BlockSpec((seq_len, D_STATE), lambda b, g: (b0 + b, b_blk0 + g)),
        pl.BlockSpec((seq_len, D_STATE), lambda b, g: (b0 + b, c_blk0 + g)),
        pl.BlockSpec((seq_len, LANES), lambda b, g: (b0 + b, g)),
        pl.BlockSpec((seq_len, gw), lambda b, g: (b0 + b, g)),
        pl.BlockSpec((None, 1, LANES), lambda b, g: (g, 0, 0)),
        pl.BlockSpec((None, 1, LANES), lambda b, g: (g, 0, 0)),
        pl.BlockSpec((None, 1, gw), lambda b, g: (g, 0, 0)),
        pl.BlockSpec((None, 1, gw), lambda b, g: (g, 0, 0)),
    ]
    args = [xbc, xbc, xbc, dt_raw, sz, dtb, a_neg, dsk, nrm]
    del n_xs
    if has_init:
        in_specs.append(pl.BlockSpec((None, None, 2, D_STATE, gw), lambda b, g: (b, g, 0, 0, 0)))
        args.append(s0)
    t = xbc.shape[0]
    out_shape = [jax.ShapeDtypeStruct((n_seq * seq_len, D_MODEL), BF16)]
    out_specs = [pl.BlockSpec((seq_len, gw), lambda b, g: (b, g))]
    if want_final:
        out_shape.append(jax.ShapeDtypeStruct((n_seq, SSD_GROUPS, 2, D_STATE, gw), F32))
        out_specs.append(pl.BlockSpec((None, None, 2, D_STATE, gw), lambda b, g: (b, g, 0, 0, 0)))
    del t
    kern = functools.partial(_ssd_kernel, n_chunks=seq_len // CHUNK, has_init=has_init, has_final=want_final)
    return pl.pallas_call(
        kern,
        grid=(n_seq, SSD_GROUPS),
        in_specs=in_specs,
        out_specs=out_specs,
        out_shape=out_shape,
        scratch_shapes=[pltpu.VMEM((seq_len, gw), F32), pltpu.VMEM((D_STATE, gw), F32)],
        compiler_params=_cparams(("arbitrary", "arbitrary")),
        name="ssd_scan_ctx" if want_final else "ssd_scan_dec",
    )(*args)


def _ret_kernel(*refs, n_chunks, has_init, has_final):
    lg_ref, q_ref, k_ref, v_ref, sg_ref = refs[:5]
    k = 5
    s0_ref = None
    if has_init:
        s0_ref = refs[k]
        k += 1
    y_ref = refs[k]
    k += 1
    sfin_ref = None
    if has_final:
        sfin_ref = refs[k]
        k += 1
    yacc, s_scr = refs[k], refs[k + 1]

    qn = CHUNK
    h = pl.program_id(1)
    row = lax.broadcasted_iota(jnp.int32, (qn, qn), 0)
    col = lax.broadcasted_iota(jnp.int32, (qn, qn), 1)
    rowk = lax.broadcasted_iota(jnp.int32, (qn, DK_RET), 0).astype(F32)

    def tables(direction):
        lg = lg_ref[direction, h]
        if direction == 0:
            keep = col <= row
            dist = (row - col).astype(F32)
            e_q = jnp.exp(lg * (rowk + 1.0))
            w_k = jnp.exp(lg * (qn - 1.0 - rowk))
        else:
            keep = col >= row
            dist = (col - row).astype(F32)
            e_q = jnp.exp(lg * (qn - rowk))
            w_k = jnp.exp(lg * rowk)
        dmat = jnp.where(keep, jnp.exp(lg * dist), 0.0)
        return dmat, e_q, w_k, jnp.exp(jnp.full((1, 1), float(qn), F32) * lg)

    def chunk(ci, tabs):
        dmat, e_q, w_k, dec = tabs
        r0 = pl.multiple_of(ci * qn, qn)
        qc = q_ref[pl.ds(r0, qn), :]
        kc = k_ref[pl.ds(r0, qn), :]
        vc = v_ref[pl.ds(r0, qn), :]
        scores = lax.dot_general(qc, kc, (((1,), (1,)), ((), ())), preferred_element_type=F32) * dmat
        s_old = s_scr[...]
        lhs = jnp.concatenate([scores.astype(BF16), (qc.astype(F32) * e_q).astype(BF16)], axis=1)
        rhs = jnp.concatenate([vc, s_old.astype(BF16)], axis=0)
        y = jnp.dot(lhs, rhs, preferred_element_type=F32)
        kw_t = (kc.astype(F32) * w_k).T.astype(BF16)
        s_scr[...] = s_old * dec + jnp.dot(kw_t, vc, preferred_element_type=F32)
        return y

    def init_state(direction):
        if has_init:
            s_scr[...] = s0_ref[direction]
        else:
            s_scr[...] = jnp.zeros_like(s_scr)

    init_state(0)
    tabs_f = tables(0)

    def fwd_body(ci, carry):
        r0 = pl.multiple_of(ci * qn, qn)
        yacc[pl.ds(r0, qn), :] = chunk(ci, tabs_f)
        return carry

    lax.fori_loop(0, n_chunks, fwd_body, 0)
    if has_final:
        sfin_ref[0] = s_scr[...]
    init_state(1)
    tabs_b = tables(1)

    def bwd_body(k2, carry):
        ci = n_chunks - 1 - k2
        r0 = pl.multiple_of(ci * qn, qn)
        y = yacc[pl.ds(r0, qn), :] + chunk(ci, tabs_b)
        y = y * lax.rsqrt(jnp.mean(y * y, axis=-1, keepdims=True) + EPS)
        y_ref[pl.ds(r0, qn), :] = (y * sg_ref[pl.ds(r0, qn), :].astype(F32)).astype(y_ref.dtype)
        return carry

    lax.fori_loop(0, n_chunks, bwd_body, 0)
    if has_final:
        sfin_ref[1] = s_scr[...]


def _ret_scan(log_g, q, k, v, sg, s0, *, row0, n_seq, seq_len, want_final):
    assert row0 % seq_len == 0
    b0 = row0 // seq_len
    has_init = s0 is not None
    in_specs = [
        pl.BlockSpec(memory_space=pltpu.SMEM),
        pl.BlockSpec((seq_len, DK_RET), lambda b, h: (b0 + b, h)),
        pl.BlockSpec((seq_len, DK_RET), lambda b, h: (b0 + b, h)),
        pl.BlockSpec((seq_len, DV_RET), lambda b, h: (b0 + b, h)),
        pl.BlockSpec((seq_len, DV_RET), lambda b, h: (b0 + b, h)),
    ]
    args = [log_g, q, k, v, sg]
    if has_init:
        in_specs.append(pl.BlockSpec((None, 2, None, DK_RET, DV_RET), lambda b, h: (b, 0, h, 0, 0)))
        args.append(s0)
    out_shape = [jax.ShapeDtypeStruct((n_seq * seq_len, H_RET * DV_RET), BF16)]
    out_specs = [pl.BlockSpec((seq_len, DV_RET), lambda b, h: (b, h))]
    if want_final:
        out_shape.append(jax.ShapeDtypeStruct((n_seq, 2, H_RET, DK_RET, DV_RET), F32))
        out_specs.append(pl.BlockSpec((None, 2, None, DK_RET, DV_RET), lambda b, h: (b, 0, h, 0, 0)))
    kern = functools.partial(_ret_kernel, n_chunks=seq_len // CHUNK, has_init=has_init, has_final=want_final)
    return pl.pallas_call(
        kern,
        grid=(n_seq, H_RET),
        in_specs=in_specs,
        out_specs=out_specs,
        out_shape=out_shape,
        scratch_shapes=[pltpu.VMEM((seq_len, DV_RET), F32), pltpu.VMEM((DK_RET, DV_RET), F32)],
        compiler_params=_cparams(("arbitrary", "arbitrary")),
        name="ret_scan_ctx" if want_final else "ret_scan_dec",
    )(*args)


def _out_a_kernel(x_ref, gate_ref, y_ref, u_ref, v_ref, wsp_ref, bsp_ref, w1_ref, w2_ref, o_ref, sgu_scr):
    bm = x_ref.shape[0]
    for ci in range(bm // CHUNK):
        rows = slice(ci * CHUNK, (ci + 1) * CHUNK)
        for g in range(SG_GROUPS):
            cols = slice(g * LANES, (g + 1) * LANES)
            mix = jnp.dot(wsp_ref[g], v_ref[rows, cols], preferred_element_type=F32) + bsp_ref[:, cols]
            sgu_scr[rows, cols] = (u_ref[rows, cols].astype(F32) * mix).astype(BF16)
    out = jnp.dot(y_ref[...], w1_ref[...], preferred_element_type=F32)
    out = out + jnp.dot(sgu_scr[...], w2_ref[...], preferred_element_type=F32)
    o_ref[...] = x_ref[...] + gate_ref[...] * out


def _out_proj_a(x, gate, y, u, v, w_sp, b_full, w1, w2, tok, *, bm):
    t, d = x.shape
    group = tok.group_of_tile(bm)
    return pl.pallas_call(
        _out_a_kernel,
        grid=(t // bm,),
        in_specs=[
            pl.BlockSpec((bm, d), lambda i: (i, 0)),
            pl.BlockSpec((None, 1, d), lambda i: (group(i), 0, 0)),
            pl.BlockSpec((bm, d), lambda i: (i, 0)),
            pl.BlockSpec((bm, d), lambda i: (i, 0)),
            pl.BlockSpec((bm, d), lambda i: (i, 0)),
            pl.BlockSpec((SG_GROUPS, CHUNK, CHUNK), lambda i: (0, 0, 0)),
            pl.BlockSpec((CHUNK, d), lambda i: (0, 0)),
            pl.BlockSpec((d, d), lambda i: (0, 0)),
            pl.BlockSpec((d, d), lambda i: (0, 0)),
        ],
        out_specs=pl.BlockSpec((bm, d), lambda i: (i, 0)),
        out_shape=jax.ShapeDtypeStruct((t, d), F32),
        scratch_shapes=[pltpu.VMEM((bm, d), BF16)],
        compiler_params=_cparams(("arbitrary",)),
        name="out_proj_a",
    )(x, gate, y, u, v, w_sp, b_full, w1, w2)


def _out_c_kernel(x_ref, gate_ref, y_ref, w_ref, o_ref):
    out = jnp.dot(y_ref[...], w_ref[...], preferred_element_type=F32)
    o_ref[...] = x_ref[...] + gate_ref[...] * out


def _out_proj_c(x, gate, y, w, tok, *, bm):
    t, d = x.shape
    kdim = y.shape[1]
    group = tok.group_of_tile(bm)
    return pl.pallas_call(
        _out_c_kernel,
        grid=(t // bm,),
        in_specs=[
            pl.BlockSpec((bm, d), lambda i: (i, 0)),
            pl.BlockSpec((None, 1, d), lambda i: (group(i), 0, 0)),
            pl.BlockSpec((bm, kdim), lambda i: (i, 0)),
            pl.BlockSpec((kdim, d), lambda i: (0, 0)),
        ],
        out_specs=pl.BlockSpec((bm, d), lambda i: (i, 0)),
        out_shape=jax.ShapeDtypeStruct((t, d), F32),
        compiler_params=_cparams(("arbitrary",)),
        name="out_proj_c",
    )(x, gate, y, w)


def _router_kernel(x_ref, gam_ref, sc_ref, sh_ref, wr_ref, br_ref, h_ref, e_ref, r_ref, g_ref, cnt_ref, cnt_scr):
    i = pl.program_id(0)
    bm = x_ref.shape[0]

    @pl.when(i == 0)
    def _():
        cnt_scr[...] = jnp.zeros_like(cnt_scr)

    h = _modulated_norm(x_ref[...], gam_ref[...], sc_ref[...], sh_ref[...])
    h_ref[...] = h
    logits = jnp.dot(h, wr_ref[...], precision=HI, preferred_element_type=F32) + br_ref[...]
    lane = lax.broadcasted_iota(jnp.int32, logits.shape, 1).astype(F32)
    vals, idxs = [], []
    work = logits
    for _ in range(TOP_K):
        m = jnp.max(work, axis=-1, keepdims=True)
        idx = jnp.min(jnp.where(work == m, lane, float(LANES)), axis=-1, keepdims=True)
        vals.append(m)
        idxs.append(idx)
        work = jnp.where(lane == idx, -jnp.inf, work)
    exps = [jnp.exp(v - vals[0]) for v in vals]
    inv = 1.0 / functools.reduce(lambda a, b: a + b, exps)
    hot = functools.reduce(jnp.logical_or, [lane == idx for idx in idxs])
    hot_f = hot.astype(F32)
    ri = lax.broadcasted_iota(jnp.int32, (bm, bm), 0)
    ci = lax.broadcasted_iota(jnp.int32, (bm, bm), 1)
    before = (ci < ri).astype(BF16)
    rank_all = cnt_scr[...] + jnp.dot(before, hot_f.astype(BF16), preferred_element_type=F32)
    e_out = jnp.zeros(logits.shape, F32)
    r_out = jnp.zeros(logits.shape, F32)
    g_out = jnp.zeros(logits.shape, F32)
    for k in range(TOP_K):
        rk = jnp.sum(jnp.where(lane == idxs[k], rank_all, 0.0), axis=-1, keepdims=True)
        e_out = jnp.where(lane == float(k), idxs[k], e_out)
        r_out = jnp.where(lane == float(k), rk, r_out)
        g_out = jnp.where(lane == float(k), exps[k] * inv, g_out)
    e_ref[...] = e_out.astype(jnp.int32)
    r_ref[...] = r_out.astype(jnp.int32)
    g_ref[...] = g_out
    cnt_scr[...] = cnt_scr[...] + jnp.sum(hot_f, axis=0, keepdims=True)
    cnt_ref[...] = cnt_scr[...]


def _router(x, gam, sc, sh, w_r, b_r, tok, *, bm):
    t, d = x.shape
    group = tok.group_of_tile(bm)
    row_spec = pl.BlockSpec((bm, LANES), lambda i: (i, 0))
    return pl.pallas_call(
        _router_kernel,
        grid=(t // bm,),
        in_specs=[
            pl.BlockSpec((bm, d), lambda i: (i, 0)),
            pl.BlockSpec((1, d), lambda i: (0, 0)),
            pl.BlockSpec((None, 1, d), lambda i: (group(i), 0, 0)),
            pl.BlockSpec((None, 1, d), lambda i: (group(i), 0, 0)),
            pl.BlockSpec((d, LANES), lambda i: (0, 0)),
            pl.BlockSpec((1, LANES), lambda i: (0, 0)),
        ],
        out_specs=[pl.BlockSpec((bm, d), lambda i: (i, 0)), row_spec, row_spec, row_spec,
                   pl.BlockSpec((1, LANES), lambda i: (0, 0))],
        out_shape=[jax.ShapeDtypeStruct((t, d), F32), jax.ShapeDtypeStruct((t, LANES), jnp.int32),
                   jax.ShapeDtypeStruct((t, LANES), jnp.int32), jax.ShapeDtypeStruct((t, LANES), F32),
                   jax.ShapeDtypeStruct((1, LANES), F32)],
        scratch_shapes=[pltpu.VMEM((1, LANES), F32)],
        compiler_params=_cparams(("arbitrary",)),
        name="moe_router",
    )(x, gam, sc, sh, w_r, b_r)


def _gather_kernel(tok_ref, h_hbm, o_ref, sem):
    rows = o_ref.shape[0]

    def issue(r, carry):
        pltpu.make_async_copy(h_hbm.at[pl.ds(tok_ref[r], 1)], o_ref.at[pl.ds(r, 1)], sem).start()
        return carry

    lax.fori_loop(0, rows, issue, 0, unroll=8)
    pltpu.make_async_copy(h_hbm.at[pl.ds(0, rows)], o_ref, sem).wait()


def _gather_rows(h, row_tok, n_rows):
    t, d = h.shape
    n_blk = n_rows // MOE_ROWS
    return pl.pallas_call(
        _gather_kernel,
        grid=(n_blk,),
        in_specs=[
            pl.BlockSpec((None, None, MOE_ROWS), lambda i: (i, 0, 0), memory_space=pltpu.SMEM),
            pl.BlockSpec(memory_space=pl.ANY),
        ],
        out_specs=pl.BlockSpec((MOE_ROWS, d), lambda i: (i, 0)),
        out_shape=jax.ShapeDtypeStruct((n_rows, d), h.dtype),
        scratch_shapes=[pltpu.SemaphoreType.DMA],
        compiler_params=_cparams(("arbitrary",)),
        name="moe_gather",
    )(row_tok.reshape(n_blk, 1, MOE_ROWS), h)


def _expert_kernel(be_ref, na_ref, x_ref, wg_ref, wl_ref, bg_ref, bl_ref, wd_ref, bd_ref, o_ref):
    i = pl.program_id(0)

    @pl.when(i < na_ref[0])
    def _():
        x = x_ref[...].astype(BF16)
        hg = jnp.dot(x, wg_ref[...], preferred_element_type=F32) + bg_ref[...]
        hl = jnp.dot(x, wl_ref[...], preferred_element_type=F32) + bl_ref[...]
        glu = jnp.minimum(hg, SWIGLU_LIMIT)
        lin = jnp.clip(hl, -SWIGLU_LIMIT, SWIGLU_LIMIT)
        act = glu * (1.0 / (1.0 + jnp.exp(-SWIGLU_ALPHA * glu))) * (lin + 1.0)
        y = jnp.dot(act.astype(BF16), wd_ref[...], preferred_element_type=F32) + bd_ref[...]
        o_ref[...] = y

    @pl.when(i >= na_ref[0])
    def _():
        o_ref[...] = jnp.zeros_like(o_ref)


def _experts(blk_expert, n_active, xs, wg, wl, bg, bl, wd, bd):
    n_rows, d = xs.shape
    dff = wg.shape[2]
    n_blk = n_rows // MOE_ROWS
    wmap = lambda i, be, na: (be[i], 0, 0)
    grid_spec = pltpu.PrefetchScalarGridSpec(
        num_scalar_prefetch=2,
        grid=(n_blk,),
        in_specs=[
            pl.BlockSpec((MOE_ROWS, d), lambda i, be, na: (i, 0)),
            pl.BlockSpec((None, d, dff), wmap),
            pl.BlockSpec((None, d, dff), wmap),
            pl.BlockSpec((None, 1, dff), wmap),
            pl.BlockSpec((None, 1, dff), wmap),
            pl.BlockSpec((None, dff, d), wmap),
            pl.BlockSpec((None, 1, d), wmap),
        ],
        out_specs=pl.BlockSpec((MOE_ROWS, d), lambda i, be, na: (i, 0)),
    )
    return pl.pallas_call(
        _expert_kernel,
        grid_spec=grid_spec,
        out_shape=jax.ShapeDtypeStruct((n_rows, d), F32),
        compiler_params=_cparams(("arbitrary",)),
        name="moe_experts",
    )(blk_expert, n_active, xs, wg, wl, bg, bl, wd, bd)


def _combine_kernel(dest_ref, x_ref, gate_ref, g_ref, gam_ref, ys_hbm, o_ref, *rest, final_norm):
    if final_norm:
        of_ref, buf, sem = rest
    else:
        buf, sem = rest
    bm = x_ref.shape[0]

    def issue(r, carry):
        for k in range(TOP_K):
            pltpu.make_async_copy(ys_hbm.at[pl.ds(dest_ref[r * TOP_K + k], 1)],
                                  buf.at[k, pl.ds(r, 1)], sem).start()
        return carry

    lax.fori_loop(0, bm, issue, 0, unroll=4)
    for k in range(TOP_K):
        pltpu.make_async_copy(ys_hbm.at[pl.ds(0, bm)], buf.at[k], sem).wait()
    gates = g_ref[...]
    acc = jnp.zeros(x_ref.shape, F32)
    for k in range(TOP_K):
        acc = acc + gates[:, k:k + 1] * buf[k]
    xn = x_ref[...] + gate_ref[...] * acc
    o_ref[...] = xn
    if final_norm:
        ms = jnp.mean(xn * xn, axis=-1, keepdims=True)
        of_ref[...] = xn * lax.rsqrt(ms + EPS) * gam_ref[...]


def _combine(dest, x, gate, gates, gam_final, ys, tok, *, bm, final_norm):
    t, d = x.shape
    group = tok.group_of_tile(bm)
    n_tiles = t // bm
    out_shape = [jax.ShapeDtypeStruct((t, d), F32)]
    out_specs = [pl.BlockSpec((bm, d), lambda i: (i, 0))]
    if final_norm:
        out_shape.append(jax.ShapeDtypeStruct((t, d), F32))
        out_specs.append(pl.BlockSpec((bm, d), lambda i: (i, 0)))
    return pl.pallas_call(
        functools.partial(_combine_kernel, final_norm=final_norm),
        grid=(n_tiles,),
        in_specs=[
            pl.BlockSpec((None, None, bm * TOP_K), lambda i: (i, 0, 0), memory_space=pltpu.SMEM),
            pl.BlockSpec((bm, d), lambda i: (i, 0)),
            pl.BlockSpec((None, 1, d), lambda i: (group(i), 0, 0)),
            pl.BlockSpec((bm, LANES), lambda i: (i, 0)),
            pl.BlockSpec((1, d), lambda i: (0, 0)),
            pl.BlockSpec(memory_space=pl.ANY),
        ],
        out_specs=out_specs,
        out_shape=out_shape,
        scratch_shapes=[pltpu.VMEM((TOP_K, bm, d), F32), pltpu.SemaphoreType.DMA],
        compiler_params=_cparams(("arbitrary",)),
        name="moe_combine_final" if final_norm else "moe_combine",
    )(dest.reshape(n_tiles, 1, bm * TOP_K), x, gate, gates, gam_final, ys)


def _moe(x, gam, sc, sh, gate, w_r, b_r, wg, wl, bg, bl, wd, bd, gam_final, tok, *, final_norm):
    t, d = x.shape
    h, e_out, r_out, gates, counts = _router(x, gam, sc, sh, w_r, b_r, tok, bm=512)
    counts = counts[0, :N_EXPERTS].astype(jnp.int32)
    padded = (counts + MOE_ROWS - 1) // MOE_ROWS * MOE_ROWS
    pad_end = jnp.cumsum(padded)
    pad_start = pad_end - padded
    e_sel = e_out[:, :TOP_K]
    dest = pad_start[e_sel] + r_out[:, :TOP_K]
    n_rows = t * TOP_K + N_EXPERTS * MOE_ROWS
    n_blk = n_rows // MOE_ROWS
    tok_ids = jnp.broadcast_to(jnp.arange(t, dtype=jnp.int32)[:, None], (t, TOP_K))
    row_tok = jnp.zeros((n_rows,), jnp.int32).at[dest.reshape(-1)].set(tok_ids.reshape(-1))
    blk_expert = jnp.minimum(jnp.searchsorted(pad_end, jnp.arange(n_blk, dtype=jnp.int32) * MOE_ROWS,
                                              side="right"), N_EXPERTS - 1).astype(jnp.int32)
    n_active = (pad_end[-1:] // MOE_ROWS).astype(jnp.int32)
    xs = _gather_rows(h, row_tok, n_rows)
    ys = _experts(blk_expert, n_active, xs, wg, wl, bg, bl, wd, bd)
    return _combine(dest, x, gate, gates, gam_final, ys, tok, bm=256, final_norm=final_norm)


def _rope_tables(tok):
    rows = tok.dec_len // GRID_W
    r = jnp.repeat(jnp.arange(rows), GRID_W).astype(F32)
    cidx = jnp.tile(jnp.arange(GRID_W), rows).astype(F32)
    n_freq = DK_RET // 4
    inv = ROPE_BASE ** (-jnp.arange(n_freq, dtype=F32) / n_freq)
    ang = jnp.concatenate([r[:, None] * inv, cidx[:, None] * inv], axis=-1)
    cos = jnp.concatenate([jnp.ones((tok.t_ctx, DK_RET // 2), F32), jnp.tile(jnp.cos(ang), (tok.n_dec_seq, 1))])
    sin = jnp.concatenate([jnp.zeros((tok.t_ctx, DK_RET // 2), F32), jnp.tile(jnp.sin(ang), (tok.n_dec_seq, 1))])
    return cos, sin


def _group_cols(p):
    h = p.shape[1]
    a = p.reshape(2, SSD_GROUPS, h // SSD_GROUPS).transpose(1, 0, 2).reshape(SSD_GROUPS, -1)
    return jnp.pad(a, ((0, 0), (0, LANES - a.shape[1])))[:, None, :]


def kernel(x_prompt, x_sample, state_ssd, state_ret, c, c_ctx, w_mod, b_mod, norm_mix, norm_ffn, w_in_a, conv_w, conv_b, dt_bias, a_log, d_skip, ssd_norm, w_sp, b_sp, w_out_a, w_in_c, decay_logit, w_out_c, w_router, b_router, w_gu, b_gu, w_down, b_down, norm_final):
    n_ctx, ctx_len, d = x_prompt.shape
    n_dec, dec_len, _ = x_sample.shape
    tok = _Tokens(n_ctx, ctx_len, n_dec, dec_len)
    depth = w_mod.shape[0]
    x = jnp.concatenate([x_prompt.reshape(tok.t_ctx, d), x_sample.reshape(tok.t_dec, d)])

    cvecs = jnp.concatenate([c_ctx[None], c, jnp.zeros((MOD_ROWS - 1 - n_dec, d), F32)])
    mod = _modulation(cvecs, w_mod, b_mod)
    mod = mod.reshape(depth, MOD_ROWS, 6, 1, d).transpose(0, 2, 1, 3, 4)

    h_ssd = a_log.shape[2]
    xbc_w = d + 2 * SSD_GROUPS * D_STATE
    o1, o2, o3 = d, d + xbc_w, d + xbc_w + 2 * h_ssd
    cos, sin = _rope_tables(tok)
    new_ssd, new_ret = [], []
    y_final = None
    bm_proj = 512

    for l in range(depth):
        sh1, sc1, g1, sh2, sc2, g2 = (mod[l, j] for j in range(6))
        gam_mix = norm_mix[l][None]
        i = l // 2
        if l % 2 == 0:
            w_in = w_in_a[i]
            w_z = w_in[:, :o1].astype(BF16)
            w_xbc = w_in[:, o1:o2].astype(BF16)
            w_dt = w_in[:, o2:o3].reshape(d, 2, SSD_GROUPS, h_ssd // SSD_GROUPS).transpose(0, 2, 1, 3)
            w_dt = jnp.pad(w_dt.reshape(d, SSD_GROUPS, -1), ((0, 0), (0, 0), (0, LANES - 2 * h_ssd // SSD_GROUPS)))
            w_dt = w_dt.reshape(d, SSD_GROUPS * LANES).astype(BF16)
            w_uv = w_in[:, o3:].astype(BF16)
            proj = functools.partial(_norm_proj, x, gam_mix, sc1, sh1, tok=tok, bm=bm_proj)
            sz = proj(w_z, tn=512, out_dtype=BF16, epilogue=_ep_silu, name="proj_z")
            xbc = _norm_proj(x, gam_mix, sc1, sh1, w_xbc, tok=tok, bm=dec_len, tn=256, out_dtype=BF16,
                             epilogue=_ep_conv_silu, extra=(conv_w[i], conv_b[i][None]),
                             extra_specs=(pl.BlockSpec((4, 256), lambda r, j: (0, j)),
                                          pl.BlockSpec((1, 256), lambda r, j: (0, j))), name="proj_xbc")
            dt_raw = proj(w_dt, tn=SSD_GROUPS * LANES, out_dtype=F32, epilogue=_ep_plain, name="proj_dt")
            u = proj(w_uv[:, :d], tn=512, out_dtype=BF16, epilogue=_ep_gelu, name="proj_u")
            v = proj(w_uv[:, d:], tn=d, out_dtype=BF16, epilogue=_ep_gelu_ln, name="proj_v")
            dtb = _group_cols(dt_bias[i])
            a_neg = _group_cols(-jnp.exp(a_log[i]))
            dsk = jnp.repeat(d_skip[i], SSD_HEAD).reshape(SSD_GROUPS, 1, GROUP_W)
            nrm = ssd_norm[i].reshape(SSD_GROUPS, 1, GROUP_W)
            s0 = state_ssd[:, i].reshape(n_dec, 2, SSD_GROUPS, HEADS_PER_GROUP, D_STATE, SSD_HEAD)
            s0 = s0.transpose(0, 2, 1, 4, 3, 5).reshape(n_dec, SSD_GROUPS, 2, D_STATE, GROUP_W)
            y_ctx, s_fin = _ssd_scan(xbc, dt_raw, sz, dtb, a_neg, dsk, nrm, None, row0=0, n_seq=n_ctx,
                                     seq_len=ctx_len, want_final=True)
            (y_dec,) = _ssd_scan(xbc, dt_raw, sz, dtb, a_neg, dsk, nrm, s0, row0=tok.t_ctx, n_seq=n_dec,
                                 seq_len=dec_len, want_final=False)
            s_fin = s_fin.reshape(n_ctx, SSD_GROUPS, 2, D_STATE, HEADS_PER_GROUP, SSD_HEAD)
            new_ssd.append(s_fin.transpose(0, 2, 1, 4, 3, 5).reshape(n_ctx, 2, h_ssd, D_STATE, SSD_HEAD))
            y = jnp.concatenate([y_ctx, y_dec])
            b_full = jnp.repeat(b_sp[i].T, LANES, axis=1)
            w_o = w_out_a[i].astype(BF16)
            x = _out_proj_a(x, g1, y, u, v, w_sp[i].astype(BF16), b_full, w_o[:d], w_o[d:], tok, bm=512)
        else:
            hk = H_RET * DK_RET
            hv = H_RET * DV_RET
            w_in = w_in_c[i].astype(BF16)
            rope_specs = (pl.BlockSpec((bm_proj, DK_RET // 2), lambda r, j: (r, 0)),) * 2
            proj = functools.partial(_norm_proj, x, gam_mix, sc1, sh1, tok=tok, bm=bm_proj)
            q = proj(w_in[:, :hk], tn=512, out_dtype=BF16, epilogue=functools.partial(_ep_rope, scale=1.0),
                     extra=(cos, sin), extra_specs=rope_specs, name="proj_q")
            kk = proj(w_in[:, hk:2 * hk], tn=512, out_dtype=BF16,
                      epilogue=functools.partial(_ep_rope, scale=DK_RET ** -0.5),
                      extra=(cos, sin), extra_specs=rope_specs, name="proj_k")
            vv = proj(w_in[:, 2 * hk:2 * hk + hv], tn=512, out_dtype=BF16, epilogue=_ep_plain, name="proj_rv")
            sg = proj(w_in[:, 2 * hk + hv:], tn=512, out_dtype=BF16, epilogue=_ep_silu, name="proj_rg")
            log_g = jax.nn.log_sigmoid(decay_logit[i].astype(F32))
            y_ctx, s_fin = _ret_scan(log_g, q, kk, vv, sg, None, row0=0, n_seq=n_ctx, seq_len=ctx_len, want_final=True)
            (y_dec,) = _ret_scan(log_g, q, kk, vv, sg, state_ret[:, i], row0=tok.t_ctx, n_seq=n_dec, seq_len=dec_len,
                                 want_final=False)
            new_ret.append(s_fin)
            y = jnp.concatenate([y_ctx, y_dec])
            x = _out_proj_c(x, g1, y, w_out_c[i].astype(BF16), tok, bm=512)

        w_r = jnp.pad(w_router[l], ((0, 0), (0, LANES - N_EXPERTS)))
        b_r = jnp.pad(b_router[l], (0, LANES - N_EXPERTS), constant_values=-1e30)[None]
        wg = w_gu[l][:, :, 0::2].astype(BF16)
        wl = w_gu[l][:, :, 1::2].astype(BF16)
        bg = b_gu[l][:, None, 0::2]
        bl = b_gu[l][:, None, 1::2]
        wd = w_down[l].astype(BF16)
        bd = b_down[l][:, None, :]
        last = l == depth - 1
        res = _moe(x, norm_ffn[l][None], sc2, sh2, g2, w_r, b_r, wg, wl, bg, bl, wd, bd, norm_final[None], tok,
                   final_norm=last)
        x = res[0]
        if last:
            y_final = res[1]

    y_prompt = y_final[:tok.t_ctx].reshape(n_ctx, ctx_len, d)
    y_sample = y_final[tok.t_ctx:].reshape(n_dec, dec_len, d)
    return (y_prompt, y_sample, jnp.stack(new_ssd, axis=1), jnp.stack(new_ret, axis=1))
```

```python
import functools
import math

import jax
import jax.numpy as jnp
from jax import lax
from jax.experimental import pallas as pl
from jax.experimental.pallas import tpu as pltpu

D_MODEL = 1024
GRID_W = 64
CHUNK = 128
SSD_HEAD = 64
SSD_GROUPS = 2
D_STATE = 128
GROUP_W = D_MODEL // SSD_GROUPS
HEADS_PER_GROUP = GROUP_W // SSD_HEAD
SG_GROUPS = 8
H_RET = 4
DK_RET = D_MODEL // H_RET
DV_RET = 2 * DK_RET
ROPE_BASE = 10000.0
N_EXPERTS = 32
TOP_K = 4
SWIGLU_LIMIT = 7.0
SWIGLU_ALPHA = 1.702
EPS = 1e-6

LANES = 128
MOD_ROWS = 8
MOE_ROWS = 256
VMEM_LIMIT = 56 * 1024 * 1024

F32 = jnp.float32
BF16 = jnp.bfloat16
HI = lax.Precision.HIGHEST


def _cparams(sem):
    return pltpu.CompilerParams(dimension_semantics=sem, vmem_limit_bytes=VMEM_LIMIT)


def _silu(x):
    return x * (1.0 / (1.0 + jnp.exp(-x)))


def _gelu_tanh(x):
    return 0.5 * x * (1.0 + jnp.tanh(math.sqrt(2.0 / math.pi) * (x + 0.044715 * (x * x * x))))


def _softplus(x):
    return jnp.maximum(x, 0.0) + jnp.log(1.0 + jnp.exp(-jnp.abs(x)))


def _mod_kernel(c_ref, w_ref, b_ref, o_ref):
    a = _silu(c_ref[...])
    o_ref[...] = jnp.dot(a, w_ref[...], precision=HI, preferred_element_type=F32) + b_ref[...]


def _modulation(cvecs, w_mod, b_mod):
    depth, d, n = w_mod.shape
    tn = 1536
    return pl.pallas_call(
        _mod_kernel,
        grid=(depth, n // tn),
        in_specs=[
            pl.BlockSpec((MOD_ROWS, d), lambda l, j: (0, 0)),
            pl.BlockSpec((None, d, tn), lambda l, j: (l, 0, j)),
            pl.BlockSpec((None, 1, tn), lambda l, j: (l, 0, j)),
        ],
        out_specs=pl.BlockSpec((None, MOD_ROWS, tn), lambda l, j: (l, 0, j)),
        out_shape=jax.ShapeDtypeStruct((depth, MOD_ROWS, n), F32),
        compiler_params=_cparams(("arbitrary", "arbitrary")),
        name="modulation",
    )(cvecs, w_mod, b_mod.reshape(depth, 1, n))


class _Tokens:
    def __init__(self, n_ctx_seq, ctx_len, n_dec_seq, dec_len):
        self.n_ctx_seq, self.ctx_len = n_ctx_seq, ctx_len
        self.n_dec_seq, self.dec_len = n_dec_seq, dec_len
        self.t_ctx = n_ctx_seq * ctx_len
        self.t_dec = n_dec_seq * dec_len
        self.total = self.t_ctx + self.t_dec

    def group_of_tile(self, bm):
        assert self.t_ctx % bm == 0 and self.dec_len % bm == 0
        n_ctx_tiles = self.t_ctx // bm
        per_seq = self.dec_len // bm

        def group(i):
            return jnp.where(i < n_ctx_tiles, 0, 1 + (i - n_ctx_tiles) // per_seq)

        return group


def _modulated_norm(x, gam, sc, sh):
    ms = jnp.mean(x * x, axis=-1, keepdims=True)
    return (x * lax.rsqrt(ms + EPS) * gam) * (1.0 + sc) + sh


def _proj_kernel(*refs, epilogue, n_extra, tok, bm):
    x_ref, gam_ref, sc_ref, sh_ref, w_ref = refs[:5]
    extra = refs[5:5 + n_extra]
    o_ref = refs[5 + n_extra]
    h_scr = refs[6 + n_extra]
    i = pl.program_id(0)

    @pl.when(pl.program_id(1) == 0)
    def _():
        h_scr[...] = _modulated_norm(x_ref[...], gam_ref[...], sc_ref[...], sh_ref[...]).astype(BF16)

    acc = jnp.dot(h_scr[...], w_ref[...], preferred_element_type=F32)
    o_ref[...] = epilogue(acc, i, tok, bm, *extra).astype(o_ref.dtype)


def _ep_plain(acc, i, tok, bm):
    return acc


def _ep_silu(acc, i, tok, bm):
    return _silu(acc)


def _ep_gelu(acc, i, tok, bm):
    return _gelu_tanh(acc)


def _ep_gelu_ln(acc, i, tok, bm):
    g = _gelu_tanh(acc)
    mu = jnp.mean(g, axis=-1, keepdims=True)
    gc = g - mu
    return gc * lax.rsqrt(jnp.mean(gc * gc, axis=-1, keepdims=True) + 1e-5)


def _ep_conv_silu(acc, i, tok, bm, cw_ref, cb_ref):
    n = acc.shape[0]
    seq = jnp.where(i * bm < tok.t_ctx, tok.ctx_len, tok.dec_len)
    t = lax.broadcasted_iota(jnp.int32, (n, 1), 0) & (seq - 1)
    cw = cw_ref[...]
    y = acc * cw[2:3, :] + cb_ref[...]
    y = y + jnp.where(t >= 2, pltpu.roll(acc, 2, axis=0), 0.0) * cw[0:1, :]
    y = y + jnp.where(t >= 1, pltpu.roll(acc, 1, axis=0), 0.0) * cw[1:2, :]
    y = y + jnp.where(t < seq - 1, pltpu.roll(acc, n - 1, axis=0), 0.0) * cw[3:4, :]
    return _silu(y)


def _ep_rope(acc, i, tok, bm, cos_ref, sin_ref, *, scale):
    cs, sn = cos_ref[...], sin_ref[...]
    half = DK_RET // 2
    outs = []
    for h in range(acc.shape[1] // DK_RET):
        x1 = acc[:, h * DK_RET:h * DK_RET + half] * scale
        x2 = acc[:, h * DK_RET + half:(h + 1) * DK_RET] * scale
        outs.append(x1 * cs - x2 * sn)
        outs.append(x2 * cs + x1 * sn)
    return jnp.concatenate(outs, axis=1)


def _norm_proj(x, gam, sc, sh, w, tok, *, bm, tn, out_dtype, epilogue, extra=(), extra_specs=(), name):
    t, d = x.shape
    n = w.shape[1]
    group = tok.group_of_tile(bm)
    kern = functools.partial(_proj_kernel, epilogue=epilogue, n_extra=len(extra), tok=tok, bm=bm)
    return pl.pallas_call(
        kern,
        grid=(t // bm, n // tn),
        in_specs=[
            pl.BlockSpec((bm, d), lambda i, j: (i, 0)),
            pl.BlockSpec((1, d), lambda i, j: (0, 0)),
            pl.BlockSpec((None, 1, d), lambda i, j: (group(i), 0, 0)),
            pl.BlockSpec((None, 1, d), lambda i, j: (group(i), 0, 0)),
            pl.BlockSpec((d, tn), lambda i, j: (0, j)),
            *extra_specs,
        ],
        out_specs=pl.BlockSpec((bm, tn), lambda i, j: (i, j)),
        out_shape=jax.ShapeDtypeStruct((t, n), out_dtype),
        scratch_shapes=[pltpu.VMEM((bm, d), BF16)],
        compiler_params=_cparams(("arbitrary", "arbitrary")),
        name=name,
    )(x, gam, sc, sh, w, *extra)


def _ssd_kernel(*refs, n_chunks, has_init, has_final):
    xs_ref, b_ref, c_ref, dt_ref, sz_ref, dtb_ref, a_ref, dsk_ref, nrm_ref = refs[:9]
    k = 9
    s0_ref = None
    if has_init:
        s0_ref = refs[k]
        k += 1
    y_ref = refs[k]
    k += 1
    sfin_ref = None
    if has_final:
        sfin_ref = refs[k]
        k += 1
    yacc, s_scr = refs[k], refs[k + 1]

    q = CHUNK
    row = lax.broadcasted_iota(jnp.int32, (q, q), 0)
    col = lax.broadcasted_iota(jnp.int32, (q, q), 1)
    lane = lax.broadcasted_iota(jnp.int32, (1, q), 1)
    left = lane < SSD_HEAD
    dtb = dtb_ref[...]
    a_neg = a_ref[...]
    n_pairs = HEADS_PER_GROUP // 2

    def chunk(ci, direction):
        r0 = pl.multiple_of(ci * q, q)
        xs = xs_ref[pl.ds(r0, q), :]
        bm_ = b_ref[pl.ds(r0, q), :]
        cm = c_ref[pl.ds(r0, q), :]
        dt = _softplus(dt_ref[pl.ds(r0, q), :] + dtb)
        la = dt * a_neg
        if direction == 0:
            tri = (col <= row).astype(F32)
            keep = col <= row
            last = q - 1
        else:
            tri = (col >= row).astype(F32)
            keep = col >= row
            last = 0
        cum = jnp.dot(tri, la, precision=HI, preferred_element_type=F32)
        cum_t = cum.T
        dt_t = dt.T
        tot_t = jnp.broadcast_to(cum_t[:, last:last + 1], (q, q))
        w_t = dt_t * jnp.exp(tot_t - cum_t)
        g = lax.dot_general(cm, bm_, (((1,), (1,)), ((), ())), preferred_element_type=F32)
        b_t = bm_.astype(F32).T
        cm_f = cm.astype(F32)
        outs = []
        for p in range(n_pairs):
            xs_p = xs[:, p * LANES:(p + 1) * LANES]
            s_p = s_scr[:, p * LANES:(p + 1) * LANES]
            s_b = s_p.astype(BF16)
            zero = jnp.zeros_like(xs_p)
            zero_s = jnp.zeros_like(s_b)
            lhs, rhs, lhs_s, rhs_s, decs = [], [], [], [], []
            for hh in range(2):
                cidx = direction * HEADS_PER_GROUP + 2 * p + hh
                cum_b = jnp.broadcast_to(cum[:, cidx:cidx + 1], (q, q))
                dec = jnp.exp(jnp.where(keep, cum_b - cum_t[cidx:cidx + 1, :], -jnp.inf))
                scores = g * dec * dt_t[cidx:cidx + 1, :]
                lhs += [scores.astype(BF16), (cm_f * jnp.exp(cum_b)).astype(BF16)]
                sel = left if hh == 0 else jnp.logical_not(left)
                rhs += [jnp.where(sel, xs_p, zero), jnp.where(sel, s_b, zero_s)]
                lhs_s.append((b_t * w_t[cidx:cidx + 1, :]).astype(BF16))
                rhs_s.append(jnp.where(sel, xs_p, zero))
                decs.append(jnp.exp(cum_t[cidx:cidx + 1, last:last + 1]))
            y_p = jnp.dot(jnp.concatenate(lhs, axis=1), jnp.concatenate(rhs, axis=0),
                          preferred_element_type=F32)
            upd = jnp.dot(jnp.concatenate(lhs_s, axis=1), jnp.concatenate(rhs_s, axis=0),
                          preferred_element_type=F32)
            s_scr[:, p * LANES:(p + 1) * LANES] = s_p * jnp.where(left, decs[0], decs[1]) + upd
            outs.append(y_p)
        return jnp.concatenate(outs, axis=1)

    def init_state(direction):
        if has_init:
            s_scr[...] = s0_ref[direction]
        else:
            s_scr[...] = jnp.zeros_like(s_scr)

    init_state(0)

    def fwd_body(ci, carry):
        r0 = pl.multiple_of(ci * q, q)
        yacc[pl.ds(r0, q), :] = chunk(ci, 0)
        return carry

    lax.fori_loop(0, n_chunks, fwd_body, 0)
    if has_final:
        sfin_ref[0] = s_scr[...]
    init_state(1)

    def bwd_body(k2, carry):
        ci = n_chunks - 1 - k2
        r0 = pl.multiple_of(ci * q, q)
        y = yacc[pl.ds(r0, q), :] + chunk(ci, 1)
        y = y + dsk_ref[...] * xs_ref[pl.ds(r0, q), :].astype(F32)
        y = y * sz_ref[pl.ds(r0, q), :].astype(F32)
        y = y * lax.rsqrt(jnp.mean(y * y, axis=-1, keepdims=True) + EPS)
        y_ref[pl.ds(r0, q), :] = (y * nrm_ref[...]).astype(y_ref.dtype)
        return carry

    lax.fori_loop(0, n_chunks, bwd_body, 0)
    if has_final:
        sfin_ref[1] = s_scr[...]


def _ssd_scan(xbc, dt_raw, sz, dtb, a_neg, dsk, nrm, s0, *, row0, n_seq, seq_len, want_final):
    assert row0 % seq_len == 0
    b0 = row0 // seq_len
    has_init = s0 is not None
    gw = GROUP_W
    n_xs = D_MODEL // gw
    b_blk0 = D_MODEL // D_STATE
    c_blk0 = b_blk0 + SSD_GROUPS
    in_specs = [
        pl.BlockSpec((seq_len, gw), lambda b, g: (b0 + b, g)),
        pl.BlockSpec((seq_len, D_STATE), lambda b, g: (b0 + b, b_blk0 + g)),
        pl.BlockSpec((seq_len, D_STATE), lambda b, g: (b0 + b, c_blk0 + g)),
        pl.BlockSpec((seq_len, LANES), lambda b, g: (b0 + b, g)),
        pl.BlockSpec((seq_len, gw), lambda b, g: (b0 + b, g)),
        pl.BlockSpec((None, 1, LANES), lambda b, g: (g, 0, 0)),
        pl.BlockSpec((None, 1, LANES), lambda b, g: (g, 0, 0)),
        pl.BlockSpec((None, 1, gw), lambda b, g: (g, 0, 0)),
        pl.BlockSpec((None, 1, gw), lambda b, g: (g, 0, 0)),
    ]
    args = [xbc, xbc, xbc, dt_raw, sz, dtb, a_neg, dsk, nrm]
    del n_xs
    if has_init:
        in_specs.append(pl.BlockSpec((None, None, 2, D_STATE, gw), lambda b, g: (b, g, 0, 0, 0)))
        args.append(s0)
    t = xbc.shape[0]
    out_shape = [jax.ShapeDtypeStruct((n_seq * seq_len, D_MODEL), BF16)]
    out_specs = [pl.BlockSpec((seq_len, gw), lambda b, g: (b, g))]
    if want_final:
        out_shape.append(jax.ShapeDtypeStruct((n_seq, SSD_GROUPS, 2, D_STATE, gw), F32))
        out_specs.append(pl.BlockSpec((None, None, 2, D_STATE, gw), lambda b, g: (b, g, 0, 0, 0)))
    del t
    kern = functools.partial(_ssd_kernel, n_chunks=seq_len // CHUNK, has_init=has_init, has_final=want_final)
    return pl.pallas_call(
        kern,
        grid=(n_seq, SSD_GROUPS),
        in_specs=in_specs,
        out_specs=out_specs,
        out_shape=out_shape,
        scratch_shapes=[pltpu.VMEM((seq_len, gw), F32), pltpu.VMEM((D_STATE, gw), F32)],
        compiler_params=_cparams(("arbitrary", "arbitrary")),
        name="ssd_scan_ctx" if want_final else "ssd_scan_dec",
    )(*args)


def _ret_kernel(*refs, n_chunks, has_init, has_final):
    lg_ref, q_ref, k_ref, v_ref, sg_ref = refs[:5]
    k = 5
    s0_ref = None
    if has_init:
        s0_ref = refs[k]
        k += 1
    y_ref = refs[k]
    k += 1
    sfin_ref = None
    if has_final:
        sfin_ref = refs[k]
        k += 1
    yacc, s_scr = refs[k], refs[k + 1]

    qn = CHUNK
    h = pl.program_id(1)
    row = lax.broadcasted_iota(jnp.int32, (qn, qn), 0)
    col = lax.broadcasted_iota(jnp.int32, (qn, qn), 1)
    rowk = lax.broadcasted_iota(jnp.int32, (qn, DK_RET), 0).astype(F32)

    def tables(direction):
        lg = lg_ref[direction, h]
        if direction == 0:
            keep = col <= row
            dist = (row - col).astype(F32)
            e_q = jnp.exp(lg * (rowk + 1.0))
            w_k = jnp.exp(lg * (qn - 1.0 - rowk))
        else:
            keep = col >= row
            dist = (col - row).astype(F32)
            e_q = jnp.exp(lg * (qn - rowk))
            w_k = jnp.exp(lg * rowk)
        dmat = jnp.where(keep, jnp.exp(lg * dist), 0.0)
        return dmat, e_q, w_k, jnp.exp(jnp.full((1, 1), float(qn), F32) * lg)

    def chunk(ci, tabs):
        dmat, e_q, w_k, dec = tabs
        r0 = pl.multiple_of(ci * qn, qn)
        qc = q_ref[pl.ds(r0, qn), :]
        kc = k_ref[pl.ds(r0, qn), :]
        vc = v_ref[pl.ds(r0, qn), :]
        scores = lax.dot_general(qc, kc, (((1,), (1,)), ((), ())), preferred_element_type=F32) * dmat
        s_old = s_scr[...]
        lhs = jnp.concatenate([scores.astype(BF16), (qc.astype(F32) * e_q).astype(BF16)], axis=1)
        rhs = jnp.concatenate([vc, s_old.astype(BF16)], axis=0)
        y = jnp.dot(lhs, rhs, preferred_element_type=F32)
        kw_t = (kc.astype(F32) * w_k).T.astype(BF16)
        s_scr[...] = s_old * dec + jnp.dot(kw_t, vc, preferred_element_type=F32)
        return y

    def init_state(direction):
        if has_init:
            s_scr[...] = s0_ref[direction]
        else:
            s_scr[...] = jnp.zeros_like(s_scr)

    init_state(0)
    tabs_f = tables(0)

    def fwd_body(ci, carry):
        r0 = pl.multiple_of(ci * qn, qn)
        yacc[pl.ds(r0, qn), :] = chunk(ci, tabs_f)
        return carry

    lax.fori_loop(0, n_chunks, fwd_body, 0)
    if has_final:
        sfin_ref[0] = s_scr[...]
    init_state(1)
    tabs_b = tables(1)

    def bwd_body(k2, carry):
        ci = n_chunks - 1 - k2
        r0 = pl.multiple_of(ci * qn, qn)
        y = yacc[pl.ds(r0, qn), :] + chunk(ci, tabs_b)
        y = y * lax.rsqrt(jnp.mean(y * y, axis=-1, keepdims=True) + EPS)
        y_ref[pl.ds(r0, qn), :] = (y * sg_ref[pl.ds(r0, qn), :].astype(F32)).astype(y_ref.dtype)
        return carry

    lax.fori_loop(0, n_chunks, bwd_body, 0)
    if has_final:
        sfin_ref[1] = s_scr[...]


def _ret_scan(log_g, q, k, v, sg, s0, *, row0, n_seq, seq_len, want_final):
    assert row0 % seq_len == 0
    b0 = row0 // seq_len
    has_init = s0 is not None
    in_specs = [
        pl.BlockSpec(memory_space=pltpu.SMEM),
        pl.BlockSpec((seq_len, DK_RET), lambda b, h: (b0 + b, h)),
        pl.BlockSpec((seq_len, DK_RET), lambda b, h: (b0 + b, h)),
        pl.BlockSpec((seq_len, DV_RET), lambda b, h: (b0 + b, h)),
        pl.BlockSpec((seq_len, DV_RET), lambda b, h: (b0 + b, h)),
    ]
    args = [log_g, q, k, v, sg]
    if has_init:
        in_specs.append(pl.BlockSpec((None, 2, None, DK_RET, DV_RET), lambda b, h: (b, 0, h, 0, 0)))
        args.append(s0)
    out_shape = [jax.ShapeDtypeStruct((n_seq * seq_len, H_RET * DV_RET), BF16)]
    out_specs = [pl.BlockSpec((seq_len, DV_RET), lambda b, h: (b, h))]
    if want_final:
        out_shape.append(jax.ShapeDtypeStruct((n_seq, 2, H_RET, DK_RET, DV_RET), F32))
        out_specs.append(pl.BlockSpec((None, 2, None, DK_RET, DV_RET), lambda b, h: (b, 0, h, 0, 0)))
    kern = functools.partial(_ret_kernel, n_chunks=seq_len // CHUNK, has_init=has_init, has_final=want_final)
    return pl.pallas_call(
        kern,
        grid=(n_seq, H_RET),
        in_specs=in_specs,
        out_specs=out_specs,
        out_shape=out_shape,
        scratch_shapes=[pltpu.VMEM((seq_len, DV_RET), F32), pltpu.VMEM((DK_RET, DV_RET), F32)],
        compiler_params=_cparams(("arbitrary", "arbitrary")),
        name="ret_scan_ctx" if want_final else "ret_scan_dec",
    )(*args)


def _out_a_kernel(x_ref, gate_ref, y_ref, u_ref, v_ref, wsp_ref, bsp_ref, w1_ref, w2_ref, o_ref, sgu_scr):
    bm = x_ref.shape[0]
    for ci in range(bm // CHUNK):
        rows = slice(ci * CHUNK, (ci + 1) * CHUNK)
        for g in range(SG_GROUPS):
            cols = slice(g * LANES, (g + 1) * LANES)
            mix = jnp.dot(wsp_ref[g], v_ref[rows, cols], preferred_element_type=F32) + bsp_ref[:, cols]
            sgu_scr[rows, cols] = (u_ref[rows, cols].astype(F32) * mix).astype(BF16)
    out = jnp.dot(y_ref[...], w1_ref[...], preferred_element_type=F32)
    out = out + jnp.dot(sgu_scr[...], w2_ref[...], preferred_element_type=F32)
    o_ref[...] = x_ref[...] + gate_ref[...] * out


def _out_proj_a(x, gate, y, u, v, w_sp, b_full, w1, w2, tok, *, bm):
    t, d = x.shape
    group = tok.group_of_tile(bm)
    return pl.pallas_call(
        _out_a_kernel,
        grid=(t // bm,),
        in_specs=[
            pl.BlockSpec((bm, d), lambda i: (i, 0)),
            pl.BlockSpec((None, 1, d), lambda i: (group(i), 0, 0)),
            pl.BlockSpec((bm, d), lambda i: (i, 0)),
            pl.BlockSpec((bm, d), lambda i: (i, 0)),
            pl.BlockSpec((bm, d), lambda i: (i, 0)),
            pl.BlockSpec((SG_GROUPS, CHUNK, CHUNK), lambda i: (0, 0, 0)),
            pl.BlockSpec((CHUNK, d), lambda i: (0, 0)),
            pl.BlockSpec((d, d), lambda i: (0, 0)),
            pl.BlockSpec((d, d), lambda i: (0, 0)),
        ],
        out_specs=pl.BlockSpec((bm, d), lambda i: (i, 0)),
        out_shape=jax.ShapeDtypeStruct((t, d), F32),
        scratch_shapes=[pltpu.VMEM((bm, d), BF16)],
        compiler_params=_cparams(("arbitrary",)),
        name="out_proj_a",
    )(x, gate, y, u, v, w_sp, b_full, w1, w2)


def _out_c_kernel(x_ref, gate_ref, y_ref, w_ref, o_ref):
    out = jnp.dot(y_ref[...], w_ref[...], preferred_element_type=F32)
    o_ref[...] = x_ref[...] + gate_ref[...] * out


def _out_proj_c(x, gate, y, w, tok, *, bm):
    t, d = x.shape
    kdim = y.shape[1]
    group = tok.group_of_tile(bm)
    return pl.pallas_call(
        _out_c_kernel,
        grid=(t // bm,),
        in_specs=[
            pl.BlockSpec((bm, d), lambda i: (i, 0)),
            pl.BlockSpec((None, 1, d), lambda i: (group(i), 0, 0)),
            pl.BlockSpec((bm, kdim), lambda i: (i, 0)),
            pl.BlockSpec((kdim, d), lambda i: (0, 0)),
        ],
        out_specs=pl.BlockSpec((bm, d), lambda i: (i, 0)),
        out_shape=jax.ShapeDtypeStruct((t, d), F32),
        compiler_params=_cparams(("arbitrary",)),
        name="out_proj_c",
    )(x, gate, y, w)


def _split_kernel(w_ref, g_ref, l_ref):
    w = w_ref[...]
    k, n2 = w.shape
    half = LANES // 2
    lane = lax.broadcasted_iota(jnp.int32, (k, LANES), 1)
    first = lane < half
    idx = jnp.where(first, 2 * lane, 2 * (lane - half) + 1)
    gs, ls = [], []
    for j in range(n2 // (2 * LANES)):
        a = jnp.take_along_axis(w[:, (2 * j) * LANES:(2 * j + 1) * LANES], idx, axis=1)
        b = jnp.take_along_axis(w[:, (2 * j + 1) * LANES:(2 * j + 2) * LANES], idx, axis=1)
        gs.append(jnp.where(first, a, pltpu.roll(b, half, axis=1)))
        ls.append(jnp.where(first, pltpu.roll(a, half, axis=1), b))
    g_ref[...] = jnp.concatenate(gs, axis=1).astype(BF16)
    l_ref[...] = jnp.concatenate(ls, axis=1).astype(BF16)


def _split_gate_lin(w_gu):
    dl, e, k, n2 = w_gu.shape
    tn = 256
    spec_out = pl.BlockSpec((None, None, k, tn), lambda a, b, j: (a, b, 0, j))
    return pl.pallas_call(
        _split_kernel,
        grid=(dl, e, n2 // (2 * tn)),
        in_specs=[pl.BlockSpec((None, None, k, 2 * tn), lambda a, b, j: (a, b, 0, j))],
        out_specs=[spec_out, spec_out],
        out_shape=[jax.ShapeDtypeStruct((dl, e, k, n2 // 2), BF16)] * 2,
        compiler_params=_cparams(("arbitrary",) * 3),
        name="split_gate_lin",
    )(w_gu)


def _router_kernel(x_ref, gam_ref, sc_ref, sh_ref, wr_ref, br_ref, h_ref, e_ref, r_ref, g_ref, cnt_ref, cnt_scr):
    i = pl.program_id(0)
    bm = x_ref.shape[0]

    @pl.when(i == 0)
    def _():
        cnt_scr[...] = jnp.zeros_like(cnt_scr)

    h = _modulated_norm(x_ref[...], gam_ref[...], sc_ref[...], sh_ref[...])
    h_ref[...] = h
    logits = jnp.dot(h, wr_ref[...], precision=HI, preferred_element_type=F32) + br_ref[...]
    lane = lax.broadcasted_iota(jnp.int32, logits.shape, 1).astype(F32)
    vals, idxs = [], []
    work = logits
    for _ in range(TOP_K):
        m = jnp.max(work, axis=-1, keepdims=True)
        idx = jnp.min(jnp.where(work == m, lane, float(LANES)), axis=-1, keepdims=True)
        vals.append(m)
        idxs.append(idx)
        work = jnp.where(lane == idx, -jnp.inf, work)
    exps = [jnp.exp(v - vals[0]) for v in vals]
    inv = 1.0 / functools.reduce(lambda a, b: a + b, exps)
    hot = functools.reduce(jnp.logical_or, [lane == idx for idx in idxs])
    hot_f = hot.astype(F32)
    ri = lax.broadcasted_iota(jnp.int32, (bm, bm), 0)
    ci = lax.broadcasted_iota(jnp.int32, (bm, bm), 1)
    before = (ci < ri).astype(BF16)
    rank_all = cnt_scr[...] + jnp.dot(before, hot_f.astype(BF16), preferred_element_type=F32)
    e_out = jnp.zeros(logits.shape, F32)
    r_out = jnp.zeros(logits.shape, F32)
    g_out = jnp.zeros(logits.shape, F32)
    for k in range(TOP_K):
        rk = jnp.sum(jnp.where(lane == idxs[k], rank_all, 0.0), axis=-1, keepdims=True)
        e_out = jnp.where(lane == float(k), idxs[k], e_out)
        r_out = jnp.where(lane == float(k), rk, r_out)
        g_out = jnp.where(lane == float(k), exps[k] * inv, g_out)
    e_ref[...] = e_out.astype(jnp.int32)
    r_ref[...] = r_out.astype(jnp.int32)
    g_ref[...] = g_out
    cnt_scr[...] = cnt_scr[...] + jnp.sum(hot_f, axis=0, keepdims=True)
    cnt_ref[...] = cnt_scr[...]


def _router(x, gam, sc, sh, w_r, b_r, tok, *, bm):
    t, d = x.shape
    group = tok.group_of_tile(bm)
    row_spec = pl.BlockSpec((bm, LANES), lambda i: (i, 0))
    return pl.pallas_call(
        _router_kernel,
        grid=(t // bm,),
        in_specs=[
            pl.BlockSpec((bm, d), lambda i: (i, 0)),
            pl.BlockSpec((1, d), lambda i: (0, 0)),
            pl.BlockSpec((None, 1, d), lambda i: (group(i), 0, 0)),
            pl.BlockSpec((None, 1, d), lambda i: (group(i), 0, 0)),
            pl.BlockSpec((d, LANES), lambda i: (0, 0)),
            pl.BlockSpec((1, LANES), lambda i: (0, 0)),
        ],
        out_specs=[pl.BlockSpec((bm, d), lambda i: (i, 0)), row_spec, row_spec, row_spec,
                   pl.BlockSpec((1, LANES), lambda i: (0, 0))],
        out_shape=[jax.ShapeDtypeStruct((t, d), F32), jax.ShapeDtypeStruct((t, LANES), jnp.int32),
                   jax.ShapeDtypeStruct((t, LANES), jnp.int32), jax.ShapeDtypeStruct((t, LANES), F32),
                   jax.ShapeDtypeStruct((1, LANES), F32)],
        scratch_shapes=[pltpu.VMEM((1, LANES), F32)],
        compiler_params=_cparams(("arbitrary",)),
        name="moe_router",
    )(x, gam, sc, sh, w_r, b_r)


def _gather_kernel(tok_ref, h_hbm, o_ref, sem):
    rows = o_ref.shape[0]

    def issue(r, carry):
        pltpu.make_async_copy(h_hbm.at[pl.ds(tok_ref[r], 1)], o_ref.at[pl.ds(r, 1)], sem).start()
        return carry

    lax.fori_loop(0, rows, issue, 0, unroll=8)
    pltpu.make_async_copy(h_hbm.at[pl.ds(0, rows)], o_ref, sem).wait()


def _gather_rows(h, row_tok, n_rows):
    t, d = h.shape
    n_blk = n_rows // MOE_ROWS
    return pl.pallas_call(
        _gather_kernel,
        grid=(n_blk,),
        in_specs=[
            pl.BlockSpec((None, None, MOE_ROWS), lambda i: (i, 0, 0), memory_space=pltpu.SMEM),
            pl.BlockSpec(memory_space=pl.ANY),
        ],
        out_specs=pl.BlockSpec((MOE_ROWS, d), lambda i: (i, 0)),
        out_shape=jax.ShapeDtypeStruct((n_rows, d), h.dtype),
        scratch_shapes=[pltpu.SemaphoreType.DMA],
        compiler_params=_cparams(("arbitrary",)),
        name="moe_gather",
    )(row_tok.reshape(n_blk, 1, MOE_ROWS), h)


def _expert_kernel(be_ref, na_ref, x_ref, wg_ref, wl_ref, bg_ref, bl_ref, wd_ref, bd_ref, o_ref):
    i = pl.program_id(0)

    @pl.when(i < na_ref[0])
    def _():
        x = x_ref[...].astype(BF16)
        hg = jnp.dot(x, wg_ref[...], preferred_element_type=F32) + bg_ref[...]
        hl = jnp.dot(x, wl_ref[...], preferred_element_type=F32) + bl_ref[...]
        glu = jnp.minimum(hg, SWIGLU_LIMIT)
        lin = jnp.clip(hl, -SWIGLU_LIMIT, SWIGLU_LIMIT)
        act = glu * (1.0 / (1.0 + jnp.exp(-SWIGLU_ALPHA * glu))) * (lin + 1.0)
        y = jnp.dot(act.astype(BF16), wd_ref[...], preferred_element_type=F32) + bd_ref[...]
        o_ref[...] = y

    @pl.when(i >= na_ref[0])
    def _():
        o_ref[...] = jnp.zeros_like(o_ref)


def _experts(blk_expert, n_active, xs, wg, wl, bg, bl, wd, bd, layer):
    n_rows, d = xs.shape
    dff = wg.shape[3]
    n_blk = n_rows // MOE_ROWS
    wmap = lambda i, be, na: (layer, be[i], 0, 0)
    grid_spec = pltpu.PrefetchScalarGridSpec(
        num_scalar_prefetch=2,
        grid=(n_blk,),
        in_specs=[
            pl.BlockSpec((MOE_ROWS, d), lambda i, be, na: (i, 0)),
            pl.BlockSpec((None, None, d, dff), wmap),
            pl.BlockSpec((None, None, d, dff), wmap),
            pl.BlockSpec((None, None, 1, dff), wmap),
            pl.BlockSpec((None, None, 1, dff), wmap),
            pl.BlockSpec((None, None, dff, d), wmap),
            pl.BlockSpec((None, None, 1, d), wmap),
        ],
        out_specs=pl.BlockSpec((MOE_ROWS, d), lambda i, be, na: (i, 0)),
    )
    return pl.pallas_call(
        _expert_kernel,
        grid_spec=grid_spec,
        out_shape=jax.ShapeDtypeStruct((n_rows, d), F32),
        compiler_params=_cparams(("arbitrary",)),
        name="moe_experts",
    )(blk_expert, n_active, xs, wg, wl, bg, bl, wd, bd)


def _combine_kernel(dest_ref, x_ref, gate_ref, g_ref, gam_ref, ys_hbm, o_ref, *rest, final_norm):
    if final_norm:
        of_ref, buf, sem = rest
    else:
        buf, sem = rest
    bm = x_ref.shape[0]

    def issue(r, carry):
        for k in range(TOP_K):
            pltpu.make_async_copy(ys_hbm.at[pl.ds(dest_ref[r * TOP_K + k], 1)],
                                  buf.at[k, pl.ds(r, 1)], sem).start()
        return carry

    lax.fori_loop(0, bm, issue, 0, unroll=4)
    for k in range(TOP_K):
        pltpu.make_async_copy(ys_hbm.at[pl.ds(0, bm)], buf.at[k], sem).wait()
    gates = g_ref[...]
    acc = jnp.zeros(x_ref.shape, F32)
    for k in range(TOP_K):
        acc = acc + gates[:, k:k + 1] * buf[k]
    xn = x_ref[...] + gate_ref[...] * acc
    o_ref[...] = xn
    if final_norm:
        ms = jnp.mean(xn * xn, axis=-1, keepdims=True)
        of_ref[...] = xn * lax.rsqrt(ms + EPS) * gam_ref[...]


def _combine(dest, x, gate, gates, gam_final, ys, tok, *, bm, final_norm):
    t, d = x.shape
    group = tok.group_of_tile(bm)
    n_tiles = t // bm
    out_shape = [jax.ShapeDtypeStruct((t, d), F32)]
    out_specs = [pl.BlockSpec((bm, d), lambda i: (i, 0))]
    if final_norm:
        out_shape.append(jax.ShapeDtypeStruct((t, d), F32))
        out_specs.append(pl.BlockSpec((bm, d), lambda i: (i, 0)))
    return pl.pallas_call(
        functools.partial(_combine_kernel, final_norm=final_norm),
        grid=(n_tiles,),
        in_specs=[
            pl.BlockSpec((None, None, bm * TOP_K), lambda i: (i, 0, 0), memory_space=pltpu.SMEM),
            pl.BlockSpec((bm, d), lambda i: (i, 0)),
            pl.BlockSpec((None, 1, d), lambda i: (group(i), 0, 0)),
            pl.BlockSpec((bm, LANES), lambda i: (i, 0)),
            pl.BlockSpec((1, d), lambda i: (0, 0)),
            pl.BlockSpec(memory_space=pl.ANY),
        ],
        out_specs=out_specs,
        out_shape=out_shape,
        scratch_shapes=[pltpu.VMEM((TOP_K, bm, d), F32), pltpu.SemaphoreType.DMA],
        compiler_params=_cparams(("arbitrary",)),
        name="moe_combine_final" if final_norm else "moe_combine",
    )(dest.reshape(n_tiles, 1, bm * TOP_K), x, gate, gates, gam_final, ys)


def _moe(x, gam, sc, sh, gate, w_r, b_r, wg, wl, bg, bl, wd, bd, layer, gam_final, tok, *, final_norm):
    t, d = x.shape
    h, e_out, r_out, gates, counts = _router(x, gam, sc, sh, w_r, b_r, tok, bm=512)
    counts = counts[0, :N_EXPERTS].astype(jnp.int32)
    padded = (counts + MOE_ROWS - 1) // MOE_ROWS * MOE_ROWS
    pad_end = jnp.cumsum(padded)
    pad_start = pad_end - padded
    e_sel = e_out[:, :TOP_K]
    dest = pad_start[e_sel] + r_out[:, :TOP_K]
    n_rows = t * TOP_K + N_EXPERTS * MOE_ROWS
    n_blk = n_rows // MOE_ROWS
    tok_ids = jnp.broadcast_to(jnp.arange(t, dtype=jnp.int32)[:, None], (t, TOP_K))
    row_tok = jnp.zeros((n_rows,), jnp.int32).at[dest.reshape(-1)].set(tok_ids.reshape(-1))
    blk_start = jnp.arange(n_blk, dtype=jnp.int32) * MOE_ROWS
    blk_expert = jnp.minimum(jnp.sum((pad_end[None, :] <= blk_start[:, None]).astype(jnp.int32), axis=1),
                             N_EXPERTS - 1)
    n_active = (pad_end[-1:] // MOE_ROWS).astype(jnp.int32)
    xs = _gather_rows(h, row_tok, n_rows)
    ys = _experts(blk_expert, n_active, xs, wg, wl, bg, bl, wd, bd, layer)
    return _combine(dest, x, gate, gates, gam_final, ys, tok, bm=256, final_norm=final_norm)


def _rope_tables(tok):
    rows = tok.dec_len // GRID_W
    r = jnp.repeat(jnp.arange(rows), GRID_W).astype(F32)
    cidx = jnp.tile(jnp.arange(GRID_W), rows).astype(F32)
    n_freq = DK_RET // 4
    inv = ROPE_BASE ** (-jnp.arange(n_freq, dtype=F32) / n_freq)
    ang = jnp.concatenate([r[:, None] * inv, cidx[:, None] * inv], axis=-1)
    cos = jnp.concatenate([jnp.ones((tok.t_ctx, DK_RET // 2), F32), jnp.tile(jnp.cos(ang), (tok.n_dec_seq, 1))])
    sin = jnp.concatenate([jnp.zeros((tok.t_ctx, DK_RET // 2), F32), jnp.tile(jnp.sin(ang), (tok.n_dec_seq, 1))])
    return cos, sin


def _group_cols(p):
    h = p.shape[1]
    a = p.reshape(2, SSD_GROUPS, h // SSD_GROUPS).transpose(1, 0, 2).reshape(SSD_GROUPS, -1)
    return jnp.pad(a, ((0, 0), (0, LANES - a.shape[1])))[:, None, :]


def kernel(x_prompt, x_sample, state_ssd, state_ret, c, c_ctx, w_mod, b_mod, norm_mix, norm_ffn, w_in_a, conv_w, conv_b, dt_bias, a_log, d_skip, ssd_norm, w_sp, b_sp, w_out_a, w_in_c, decay_logit, w_out_c, w_router, b_router, w_gu, b_gu, w_down, b_down, norm_final):
    n_ctx, ctx_len, d = x_prompt.shape
    n_dec, dec_len, _ = x_sample.shape
    tok = _Tokens(n_ctx, ctx_len, n_dec, dec_len)
    depth = w_mod.shape[0]
    x = jnp.concatenate([x_prompt.reshape(tok.t_ctx, d), x_sample.reshape(tok.t_dec, d)])

    cvecs = jnp.concatenate([c_ctx[None], c, jnp.zeros((MOD_ROWS - 1 - n_dec, d), F32)])
    mod = _modulation(cvecs, w_mod, b_mod)
    mod = mod.reshape(depth, MOD_ROWS, 6, 1, d).transpose(0, 2, 1, 3, 4)

    h_ssd = a_log.shape[2]
    xbc_w = d + 2 * SSD_GROUPS * D_STATE
    o1, o2, o3 = d, d + xbc_w, d + xbc_w + 2 * h_ssd
    cos, sin = _rope_tables(tok)
    wg_all, wl_all = _split_gate_lin(w_gu)
    bg_all = b_gu[:, :, None, 0::2]
    bl_all = b_gu[:, :, None, 1::2]
    wd_all = w_down.astype(BF16)
    bd_all = b_down[:, :, None, :]
    new_ssd, new_ret = [], []
    y_final = None
    bm_proj = 512

    for l in range(depth):
        sh1, sc1, g1, sh2, sc2, g2 = (mod[l, j] for j in range(6))
        gam_mix = norm_mix[l][None]
        i = l // 2
        if l % 2 == 0:
            w_in = w_in_a[i]
            w_z = w_in[:, :o1].astype(BF16)
            w_xbc = w_in[:, o1:o2].astype(BF16)
            w_dt = w_in[:, o2:o3].reshape(d, 2, SSD_GROUPS, h_ssd // SSD_GROUPS).transpose(0, 2, 1, 3)
            w_dt = jnp.pad(w_dt.reshape(d, SSD_GROUPS, -1), ((0, 0), (0, 0), (0, LANES - 2 * h_ssd // SSD_GROUPS)))
            w_dt = w_dt.reshape(d, SSD_GROUPS * LANES).astype(BF16)
            w_uv = w_in[:, o3:].astype(BF16)
            proj = functools.partial(_norm_proj, x, gam_mix, sc1, sh1, tok=tok, bm=bm_proj)
            sz = proj(w_z, tn=512, out_dtype=BF16, epilogue=_ep_silu, name="proj_z")
            xbc = _norm_proj(x, gam_mix, sc1, sh1, w_xbc, tok=tok, bm=dec_len, tn=256, out_dtype=BF16,
                             epilogue=_ep_conv_silu, extra=(conv_w[i], conv_b[i][None]),
                             extra_specs=(pl.BlockSpec((4, 256), lambda r, j: (0, j)),
                                          pl.BlockSpec((1, 256), lambda r, j: (0, j))), name="proj_xbc")
            dt_raw = proj(w_dt, tn=SSD_GROUPS * LANES, out_dtype=F32, epilogue=_ep_plain, name="proj_dt")
            u = proj(w_uv[:, :d], tn=512, out_dtype=BF16, epilogue=_ep_gelu, name="proj_u")
            v = proj(w_uv[:, d:], tn=d, out_dtype=BF16, epilogue=_ep_gelu_ln, name="proj_v")
            dtb = _group_cols(dt_bias[i])
            a_neg = _group_cols(-jnp.exp(a_log[i]))
            dsk = jnp.repeat(d_skip[i], SSD_HEAD).reshape(SSD_GROUPS, 1, GROUP_W)
            nrm = ssd_norm[i].reshape(SSD_GROUPS, 1, GROUP_W)
            s0 = state_ssd[:, i].reshape(n_dec, 2, SSD_GROUPS, HEADS_PER_GROUP, D_STATE, SSD_HEAD)
            s0 = s0.transpose(0, 2, 1, 4, 3, 5).reshape(n_dec, SSD_GROUPS, 2, D_STATE, GROUP_W)
            y_ctx, s_fin = _ssd_scan(xbc, dt_raw, sz, dtb, a_neg, dsk, nrm, None, row0=0, n_seq=n_ctx,
                                     seq_len=ctx_len, want_final=True)
            (y_dec,) = _ssd_scan(xbc, dt_raw, sz, dtb, a_neg, dsk, nrm, s0, row0=tok.t_ctx, n_seq=n_dec,
                                 seq_len=dec_len, want_final=False)
            s_fin = s_fin.reshape(n_ctx, SSD_GROUPS, 2, D_STATE, HEADS_PER_GROUP, SSD_HEAD)
            new_ssd.append(s_fin.transpose(0, 2, 1, 4, 3, 5).reshape(n_ctx, 2, h_ssd, D_STATE, SSD_HEAD))
            y = jnp.concatenate([y_ctx, y_dec])
            b_full = jnp.repeat(b_sp[i].T, LANES, axis=1)
            w_o = w_out_a[i].astype(BF16)
            x = _out_proj_a(x, g1, y, u, v, w_sp[i].astype(BF16), b_full, w_o[:d], w_o[d:], tok, bm=512)
        else:
            hk = H_RET * DK_RET
            hv = H_RET * DV_RET
            w_in = w_in_c[i].astype(BF16)
            rope_specs = (pl.BlockSpec((bm_proj, DK_RET // 2), lambda r, j: (r, 0)),) * 2
            proj = functools.partial(_norm_proj, x, gam_mix, sc1, sh1, tok=tok, bm=bm_proj)
            q = proj(w_in[:, :hk], tn=512, out_dtype=BF16, epilogue=functools.partial(_ep_rope, scale=1.0),
                     extra=(cos, sin), extra_specs=rope_specs, name="proj_q")
            kk = proj(w_in[:, hk:2 * hk], tn=512, out_dtype=BF16,
                      epilogue=functools.partial(_ep_rope, scale=DK_RET ** -0.5),
                      extra=(cos, sin), extra_specs=rope_specs, name="proj_k")
            vv = proj(w_in[:, 2 * hk:2 * hk + hv], tn=512, out_dtype=BF16, epilogue=_ep_plain, name="proj_rv")
            sg = proj(w_in[:, 2 * hk + hv:], tn=512, out_dtype=BF16, epilogue=_ep_silu, name="proj_rg")
            log_g = jax.nn.log_sigmoid(decay_logit[i].astype(F32))
            y_ctx, s_fin = _ret_scan(log_g, q, kk, vv, sg, None, row0=0, n_seq=n_ctx, seq_len=ctx_len, want_final=True)
            (y_dec,) = _ret_scan(log_g, q, kk, vv, sg, state_ret[:, i], row0=tok.t_ctx, n_seq=n_dec, seq_len=dec_len,
                                 want_final=False)
            new_ret.append(s_fin)
            y = jnp.concatenate([y_ctx, y_dec])
            x = _out_proj_c(x, g1, y, w_out_c[i].astype(BF16), tok, bm=512)

        w_r = jnp.pad(w_router[l], ((0, 0), (0, LANES - N_EXPERTS)))
        b_r = jnp.pad(b_router[l], (0, LANES - N_EXPERTS), constant_values=-1e30)[None]
        last = l == depth - 1
        res = _moe(x, norm_ffn[l][None], sc2, sh2, g2, w_r, b_r, wg_all, wl_all, bg_all, bl_all, wd_all, bd_all, l,
                   norm_final[None], tok, final_norm=last)
        x = res[0]
        if last:
            y_final = res[1]

    y_prompt = y_final[:tok.t_ctx].reshape(n_ctx, ctx_len, d)
    y_sample = y_final[tok.t_ctx:].reshape(n_dec, dec_len, d)
    return (y_prompt, y_sample, jnp.stack(new_ssd, axis=1), jnp.stack(new_ret, axis=1))
```

```python
import functools
import math

import jax
import jax.numpy as jnp
from jax import lax
from jax.experimental import pallas as pl
from jax.experimental.pallas import tpu as pltpu

D_MODEL = 1024
GRID_W = 64
CHUNK = 128
SSD_HEAD = 64
SSD_GROUPS = 2
D_STATE = 128
GROUP_W = D_MODEL // SSD_GROUPS
HEADS_PER_GROUP = GROUP_W // SSD_HEAD
SG_GROUPS = 8
H_RET = 4
DK_RET = D_MODEL // H_RET
DV_RET = 2 * DK_RET
ROPE_BASE = 10000.0
N_EXPERTS = 32
TOP_K = 4
SWIGLU_LIMIT = 7.0
SWIGLU_ALPHA = 1.702
EPS = 1e-6

LANES = 128
MOD_ROWS = 8
MOE_ROWS = 256
VMEM_LIMIT = 56 * 1024 * 1024

F32 = jnp.float32
BF16 = jnp.bfloat16
HI = lax.Precision.HIGHEST


def _cparams(sem):
    return pltpu.CompilerParams(dimension_semantics=sem, vmem_limit_bytes=VMEM_LIMIT)


def _silu(x):
    return x * (1.0 / (1.0 + jnp.exp(-x)))


def _gelu_tanh(x):
    return 0.5 * x * (1.0 + jnp.tanh(math.sqrt(2.0 / math.pi) * (x + 0.044715 * (x * x * x))))


def _softplus(x):
    return jnp.maximum(x, 0.0) + jnp.log(1.0 + jnp.exp(-jnp.abs(x)))


def _mod_kernel(c_ref, w_ref, b_ref, o_ref):
    a = _silu(c_ref[...])
    o_ref[...] = jnp.dot(a, w_ref[...], precision=HI, preferred_element_type=F32) + b_ref[...]


def _modulation(cvecs, w_mod, b_mod):
    depth, d, n = w_mod.shape
    tn = 1536
    return pl.pallas_call(
        _mod_kernel,
        grid=(depth, n // tn),
        in_specs=[
            pl.BlockSpec((MOD_ROWS, d), lambda l, j: (0, 0)),
            pl.BlockSpec((None, d, tn), lambda l, j: (l, 0, j)),
            pl.BlockSpec((None, 1, tn), lambda l, j: (l, 0, j)),
        ],
        out_specs=pl.BlockSpec((None, MOD_ROWS, tn), lambda l, j: (l, 0, j)),
        out_shape=jax.ShapeDtypeStruct((depth, MOD_ROWS, n), F32),
        compiler_params=_cparams(("arbitrary", "arbitrary")),
        name="modulation",
    )(cvecs, w_mod, b_mod.reshape(depth, 1, n))


class _Tokens:
    def __init__(self, n_ctx_seq, ctx_len, n_dec_seq, dec_len):
        self.n_ctx_seq, self.ctx_len = n_ctx_seq, ctx_len
        self.n_dec_seq, self.dec_len = n_dec_seq, dec_len
        self.t_ctx = n_ctx_seq * ctx_len
        self.t_dec = n_dec_seq * dec_len
        self.total = self.t_ctx + self.t_dec

    def group_of_tile(self, bm):
        assert self.t_ctx % bm == 0 and self.dec_len % bm == 0
        n_ctx_tiles = self.t_ctx // bm
        per_seq = self.dec_len // bm

        def group(i):
            return jnp.where(i < n_ctx_tiles, 0, 1 + (i - n_ctx_tiles) // per_seq)

        return group


def _modulated_norm(x, gam, sc, sh):
    ms = jnp.mean(x * x, axis=-1, keepdims=True)
    return (x * lax.rsqrt(ms + EPS) * gam) * (1.0 + sc) + sh


def _proj_kernel(*refs, epilogue, n_extra, tok, bm):
    x_ref, gam_ref, sc_ref, sh_ref, w_ref = refs[:5]
    extra = refs[5:5 + n_extra]
    o_ref = refs[5 + n_extra]
    h_scr = refs[6 + n_extra]
    i = pl.program_id(0)

    @pl.when(pl.program_id(1) == 0)
    def _():
        h_scr[...] = _modulated_norm(x_ref[...], gam_ref[...], sc_ref[...], sh_ref[...]).astype(BF16)

    acc = jnp.dot(h_scr[...], w_ref[...], preferred_element_type=F32)
    o_ref[...] = epilogue(acc, i, tok, bm, *extra).astype(o_ref.dtype)


def _ep_plain(acc, i, tok, bm):
    return acc


def _ep_silu(acc, i, tok, bm):
    return _silu(acc)


def _ep_gelu(acc, i, tok, bm):
    return _gelu_tanh(acc)


def _ep_gelu_ln(acc, i, tok, bm):
    g = _gelu_tanh(acc)
    mu = jnp.mean(g, axis=-1, keepdims=True)
    gc = g - mu
    return gc * lax.rsqrt(jnp.mean(gc * gc, axis=-1, keepdims=True) + 1e-5)


def _ep_conv_silu(acc, i, tok, bm, cw_ref, cb_ref):
    n = acc.shape[0]
    seq = jnp.where(i * bm < tok.t_ctx, tok.ctx_len, tok.dec_len)
    t = lax.broadcasted_iota(jnp.int32, (n, 1), 0) & (seq - 1)
    cw = cw_ref[...]
    y = acc * cw[2:3, :] + cb_ref[...]
    y = y + jnp.where(t >= 2, pltpu.roll(acc, 2, axis=0), 0.0) * cw[0:1, :]
    y = y + jnp.where(t >= 1, pltpu.roll(acc, 1, axis=0), 0.0) * cw[1:2, :]
    y = y + jnp.where(t < seq - 1, pltpu.roll(acc, n - 1, axis=0), 0.0) * cw[3:4, :]
    return _silu(y)


def _ep_rope(acc, i, tok, bm, cos_ref, sin_ref, *, scale):
    cs, sn = cos_ref[...], sin_ref[...]
    half = DK_RET // 2
    outs = []
    for h in range(acc.shape[1] // DK_RET):
        x1 = acc[:, h * DK_RET:h * DK_RET + half] * scale
        x2 = acc[:, h * DK_RET + half:(h + 1) * DK_RET] * scale
        outs.append(x1 * cs - x2 * sn)
        outs.append(x2 * cs + x1 * sn)
    return jnp.concatenate(outs, axis=1)


def _norm_proj(x, gam, sc, sh, w, tok, *, bm, tn, out_dtype, epilogue, extra=(), extra_specs=(), name):
    t, d = x.shape
    n = w.shape[1]
    group = tok.group_of_tile(bm)
    kern = functools.partial(_proj_kernel, epilogue=epilogue, n_extra=len(extra), tok=tok, bm=bm)
    return pl.pallas_call(
        kern,
        grid=(t // bm, n // tn),
        in_specs=[
            pl.BlockSpec((bm, d), lambda i, j: (i, 0)),
            pl.BlockSpec((1, d), lambda i, j: (0, 0)),
            pl.BlockSpec((None, 1, d), lambda i, j: (group(i), 0, 0)),
            pl.BlockSpec((None, 1, d), lambda i, j: (group(i), 0, 0)),
            pl.BlockSpec((d, tn), lambda i, j: (0, j)),
            *extra_specs,
        ],
        out_specs=pl.BlockSpec((bm, tn), lambda i, j: (i, j)),
        out_shape=jax.ShapeDtypeStruct((t, n), out_dtype),
        scratch_shapes=[pltpu.VMEM((bm, d), BF16)],
        compiler_params=_cparams(("arbitrary", "arbitrary")),
        name=name,
    )(x, gam, sc, sh, w, *extra)


def _ssd_kernel(*refs, n_chunks, has_init, has_final):
    xs_ref, b_ref, c_ref, dt_ref, sz_ref, dtb_ref, a_ref, dsk_ref, nrm_ref = refs[:9]
    k = 9
    s0_ref = None
    if has_init:
        s0_ref = refs[k]
        k += 1
    y_ref = refs[k]
    k += 1
    sfin_ref = None
    if has_final:
        sfin_ref = refs[k]
        k += 1
    yacc, s_scr = refs[k], refs[k + 1]

    q = CHUNK
    row = lax.broadcasted_iota(jnp.int32, (q, q), 0)
    col = lax.broadcasted_iota(jnp.int32, (q, q), 1)
    lane = lax.broadcasted_iota(jnp.int32, (1, q), 1)
    left = lane < SSD_HEAD
    dtb = dtb_ref[...]
    a_neg = a_ref[...]
    n_pairs = HEADS_PER_GROUP // 2

    def chunk(ci, direction):
        r0 = pl.multiple_of(ci * q, q)
        xs = xs_ref[pl.ds(r0, q), :]
        bm_ = b_ref[pl.ds(r0, q), :]
        cm = c_ref[pl.ds(r0, q), :]
        dt = _softplus(dt_ref[pl.ds(r0, q), :] + dtb)
        la = dt * a_neg
        if direction == 0:
            tri = (col <= row).astype(F32)
            keep = col <= row
            last = q - 1
        else:
            tri = (col >= row).astype(F32)
            keep = col >= row
            last = 0
        cum = jnp.dot(tri, la, precision=HI, preferred_element_type=F32)
        cum_t = cum.T
        dt_t = dt.T
        tot_t = jnp.broadcast_to(cum_t[:, last:last + 1], (q, q))
        w_t = dt_t * jnp.exp(tot_t - cum_t)
        g = lax.dot_general(cm, bm_, (((1,), (1,)), ((), ())), preferred_element_type=F32)
        b_t = bm_.astype(F32).T
        cm_f = cm.astype(F32)
        outs = []
        for p in range(n_pairs):
            xs_p = xs[:, p * LANES:(p + 1) * LANES]
            s_p = s_scr[:, p * LANES:(p + 1) * LANES]
            s_b = s_p.astype(BF16)
            zero = jnp.zeros_like(xs_p)
            zero_s = jnp.zeros_like(s_b)
            lhs, rhs, lhs_s, rhs_s, decs = [], [], [], [], []
            for hh in range(2):
                cidx = direction * HEADS_PER_GROUP + 2 * p + hh
                cum_b = jnp.broadcast_to(cum[:, cidx:cidx + 1], (q, q))
                dec = jnp.exp(jnp.where(keep, cum_b - cum_t[cidx:cidx + 1, :], -jnp.inf))
                scores = g * dec * dt_t[cidx:cidx + 1, :]
                lhs += [scores.astype(BF16), (cm_f * jnp.exp(cum_b)).astype(BF16)]
                sel = left if hh == 0 else jnp.logical_not(left)
                rhs += [jnp.where(sel, xs_p, zero), jnp.where(sel, s_b, zero_s)]
                lhs_s.append((b_t * w_t[cidx:cidx + 1, :]).astype(BF16))
                rhs_s.append(jnp.where(sel, xs_p, zero))
                decs.append(jnp.exp(cum_t[cidx:cidx + 1, last:last + 1]))
            y_p = jnp.dot(jnp.concatenate(lhs, axis=1), jnp.concatenate(rhs, axis=0),
                          preferred_element_type=F32)
            upd = jnp.dot(jnp.concatenate(lhs_s, axis=1), jnp.concatenate(rhs_s, axis=0),
                          preferred_element_type=F32)
            s_scr[:, p * LANES:(p + 1) * LANES] = s_p * jnp.where(left, decs[0], decs[1]) + upd
            outs.append(y_p)
        return jnp.concatenate(outs, axis=1)

    def init_state(direction):
        if has_init:
            s_scr[...] = s0_ref[direction]
        else:
            s_scr[...] = jnp.zeros_like(s_scr)

    init_state(0)

    def fwd_body(ci, carry):
        r0 = pl.multiple_of(ci * q, q)
        yacc[pl.ds(r0, q), :] = chunk(ci, 0)
        return carry

    lax.fori_loop(0, n_chunks, fwd_body, 0)
    if has_final:
        sfin_ref[0] = s_scr[...]
    init_state(1)

    def bwd_body(k2, carry):
        ci = n_chunks - 1 - k2
        r0 = pl.multiple_of(ci * q, q)
        y = yacc[pl.ds(r0, q), :] + chunk(ci, 1)
        y = y + dsk_ref[...] * xs_ref[pl.ds(r0, q), :].astype(F32)
        y = y * sz_ref[pl.ds(r0, q), :].astype(F32)
        y = y * lax.rsqrt(jnp.mean(y * y, axis=-1, keepdims=True) + EPS)
        y_ref[pl.ds(r0, q), :] = (y * nrm_ref[...]).astype(y_ref.dtype)
        return carry

    lax.fori_loop(0, n_chunks, bwd_body, 0)
    if has_final:
        sfin_ref[1] = s_scr[...]


def _ssd_scan(xbc, dt_raw, sz, dtb, a_neg, dsk, nrm, s0, *, row0, n_seq, seq_len, want_final):
    assert row0 % seq_len == 0
    b0 = row0 // seq_len
    has_init = s0 is not None
    gw = GROUP_W
    n_xs = D_MODEL // gw
    b_blk0 = D_MODEL // D_STATE
    c_blk0 = b_blk0 + SSD_GROUPS
    in_specs = [
        pl.BlockSpec((seq_len, gw), lambda b, g: (b0 + b, g)),
        pl.BlockSpec((seq_len, D_STATE), lambda b, g: (b0 + b, b_blk0 + g)),
        pl.BlockSpec((seq_len, D_STATE), lambda b, g: (b0 + b, c_blk0 + g)),
        pl.BlockSpec((seq_len, LANES), lambda b, g: (b0 + b, g)),
        pl.BlockSpec((seq_len, gw), lambda b, g: (b0 + b, g)),
        pl.BlockSpec((None, 1, LANES), lambda b, g: (g, 0, 0)),
        pl.BlockSpec((None, 1, LANES), lambda b, g: (g, 0, 0)),
        pl.BlockSpec((None, 1, gw), lambda b, g: (g, 0, 0)),
        pl.BlockSpec((None, 1, gw), lambda b, g: (g, 0, 0)),
    ]
    args = [xbc, xbc, xbc, dt_raw, sz, dtb, a_neg, dsk, nrm]
    del n_xs
    if has_init:
        in_specs.append(pl.BlockSpec((None, None, 2, D_STATE, gw), lambda b, g: (b, g, 0, 0, 0)))
        args.append(s0)
    t = xbc.shape[0]
    out_shape = [jax.ShapeDtypeStruct((n_seq * seq_len, D_MODEL), BF16)]
    out_specs = [pl.BlockSpec((seq_len, gw), lambda b, g: (b, g))]
    if want_final:
        out_shape.append(jax.ShapeDtypeStruct((n_seq, SSD_GROUPS, 2, D_STATE, gw), F32))
        out_specs.append(pl.BlockSpec((None, None, 2, D_STATE, gw), lambda b, g: (b, g, 0, 0, 0)))
    del t
    kern = functools.partial(_ssd_kernel, n_chunks=seq_len // CHUNK, has_init=has_init, has_final=want_final)
    return pl.pallas_call(
        kern,
        grid=(n_seq, SSD_GROUPS),
        in_specs=in_specs,
        out_specs=out_specs,
        out_shape=out_shape,
        scratch_shapes=[pltpu.VMEM((seq_len, gw), F32), pltpu.VMEM((D_STATE, gw), F32)],
        compiler_params=_cparams(("arbitrary", "arbitrary")),
        name="ssd_scan_ctx" if want_final else "ssd_scan_dec",
    )(*args)


def _ret_kernel(*refs, n_chunks, has_init, has_final):
    lg_ref, q_ref, k_ref, v_ref, sg_ref = refs[:5]
    k = 5
    s0_ref = None
    if has_init:
        s0_ref = refs[k]
        k += 1
    y_ref = refs[k]
    k += 1
    sfin_ref = None
    if has_final:
        sfin_ref = refs[k]
        k += 1
    yacc, s_scr = refs[k], refs[k + 1]

    qn = CHUNK
    h = pl.program_id(1)
    row = lax.broadcasted_iota(jnp.int32, (qn, qn), 0)
    col = lax.broadcasted_iota(jnp.int32, (qn, qn), 1)
    rowk = lax.broadcasted_iota(jnp.int32, (qn, DK_RET), 0).astype(F32)

    def tables(direction):
        lg = lg_ref[direction, h]
        if direction == 0:
            keep = col <= row
            dist = (row - col).astype(F32)
            e_q = jnp.exp(lg * (rowk + 1.0))
            w_k = jnp.exp(lg * (qn - 1.0 - rowk))
        else:
            keep = col >= row
            dist = (col - row).astype(F32)
            e_q = jnp.exp(lg * (qn - rowk))
            w_k = jnp.exp(lg * rowk)
        dmat = jnp.where(keep, jnp.exp(lg * dist), 0.0)
        return dmat, e_q, w_k, jnp.exp(jnp.full((1, 1), float(qn), F32) * lg)

    def chunk(ci, tabs):
        dmat, e_q, w_k, dec = tabs
        r0 = pl.multiple_of(ci * qn, qn)
        qc = q_ref[pl.ds(r0, qn), :]
        kc = k_ref[pl.ds(r0, qn), :]
        vc = v_ref[pl.ds(r0, qn), :]
        scores = lax.dot_general(qc, kc, (((1,), (1,)), ((), ())), preferred_element_type=F32) * dmat
        s_old = s_scr[...]
        lhs = jnp.concatenate([scores.astype(BF16), (qc.astype(F32) * e_q).astype(BF16)], axis=1)
        rhs = jnp.concatenate([vc, s_old.astype(BF16)], axis=0)
        y = jnp.dot(lhs, rhs, preferred_element_type=F32)
        kw_t = (kc.astype(F32) * w_k).T.astype(BF16)
        s_scr[...] = s_old * dec + jnp.dot(kw_t, vc, preferred_element_type=F32)
        return y

    def init_state(direction):
        if has_init:
            s_scr[...] = s0_ref[direction]
        else:
            s_scr[...] = jnp.zeros_like(s_scr)

    init_state(0)
    tabs_f = tables(0)

    def fwd_body(ci, carry):
        r0 = pl.multiple_of(ci * qn, qn)
        yacc[pl.ds(r0, qn), :] = chunk(ci, tabs_f)
        return carry

    lax.fori_loop(0, n_chunks, fwd_body, 0)
    if has_final:
        sfin_ref[0] = s_scr[...]
    init_state(1)
    tabs_b = tables(1)

    def bwd_body(k2, carry):
        ci = n_chunks - 1 - k2
        r0 = pl.multiple_of(ci * qn, qn)
        y = yacc[pl.ds(r0, qn), :] + chunk(ci, tabs_b)
        y = y * lax.rsqrt(jnp.mean(y * y, axis=-1, keepdims=True) + EPS)
        y_ref[pl.ds(r0, qn), :] = (y * sg_ref[pl.ds(r0, qn), :].astype(F32)).astype(y_ref.dtype)
        return carry

    lax.fori_loop(0, n_chunks, bwd_body, 0)
    if has_final:
        sfin_ref[1] = s_scr[...]


def _ret_scan(log_g, q, k, v, sg, s0, *, row0, n_seq, seq_len, want_final):
    assert row0 % seq_len == 0
    b0 = row0 // seq_len
    has_init = s0 is not None
    in_specs = [
        pl.BlockSpec(memory_space=pltpu.SMEM),
        pl.BlockSpec((seq_len, DK_RET), lambda b, h: (b0 + b, h)),
        pl.BlockSpec((seq_len, DK_RET), lambda b, h: (b0 + b, h)),
        pl.BlockSpec((seq_len, DV_RET), lambda b, h: (b0 + b, h)),
        pl.BlockSpec((seq_len, DV_RET), lambda b, h: (b0 + b, h)),
    ]
    args = [log_g, q, k, v, sg]
    if has_init:
        in_specs.append(pl.BlockSpec((None, 2, None, DK_RET, DV_RET), lambda b, h: (b, 0, h, 0, 0)))
        args.append(s0)
    out_shape = [jax.ShapeDtypeStruct((n_seq * seq_len, H_RET * DV_RET), BF16)]
    out_specs = [pl.BlockSpec((seq_len, DV_RET), lambda b, h: (b, h))]
    if want_final:
        out_shape.append(jax.ShapeDtypeStruct((n_seq, 2, H_RET, DK_RET, DV_RET), F32))
        out_specs.append(pl.BlockSpec((None, 2, None, DK_RET, DV_RET), lambda b, h: (b, 0, h, 0, 0)))
    kern = functools.partial(_ret_kernel, n_chunks=seq_len // CHUNK, has_init=has_init, has_final=want_final)
    return pl.pallas_call(
        kern,
        grid=(n_seq, H_RET),
        in_specs=in_specs,
        out_specs=out_specs,
        out_shape=out_shape,
        scratch_shapes=[pltpu.VMEM((seq_len, DV_RET), F32), pltpu.VMEM((DK_RET, DV_RET), F32)],
        compiler_params=_cparams(("arbitrary", "arbitrary")),
        name="ret_scan_ctx" if want_final else "ret_scan_dec",
    )(*args)


def _out_a_kernel(x_ref, gate_ref, y_ref, u_ref, v_ref, wsp_ref, bsp_ref, w1_ref, w2_ref, o_ref, sgu_scr):
    bm = x_ref.shape[0]
    for ci in range(bm // CHUNK):
        rows = slice(ci * CHUNK, (ci + 1) * CHUNK)
        for g in range(SG_GROUPS):
            cols = slice(g * LANES, (g + 1) * LANES)
            mix = jnp.dot(wsp_ref[g], v_ref[rows, cols], preferred_element_type=F32) + bsp_ref[:, cols]
            sgu_scr[rows, cols] = (u_ref[rows, cols].astype(F32) * mix).astype(BF16)
    out = jnp.dot(y_ref[...], w1_ref[...], preferred_element_type=F32)
    out = out + jnp.dot(sgu_scr[...], w2_ref[...], preferred_element_type=F32)
    o_ref[...] = x_ref[...] + gate_ref[...] * out


def _out_proj_a(x, gate, y, u, v, w_sp, b_full, w1, w2, tok, *, bm):
    t, d = x.shape
    group = tok.group_of_tile(bm)
    return pl.pallas_call(
        _out_a_kernel,
        grid=(t // bm,),
        in_specs=[
            pl.BlockSpec((bm, d), lambda i: (i, 0)),
            pl.BlockSpec((None, 1, d), lambda i: (group(i), 0, 0)),
            pl.BlockSpec((bm, d), lambda i: (i, 0)),
            pl.BlockSpec((bm, d), lambda i: (i, 0)),
            pl.BlockSpec((bm, d), lambda i: (i, 0)),
            pl.BlockSpec((SG_GROUPS, CHUNK, CHUNK), lambda i: (0, 0, 0)),
            pl.BlockSpec((CHUNK, d), lambda i: (0, 0)),
            pl.BlockSpec((d, d), lambda i: (0, 0)),
            pl.BlockSpec((d, d), lambda i: (0, 0)),
        ],
        out_specs=pl.BlockSpec((bm, d), lambda i: (i, 0)),
        out_shape=jax.ShapeDtypeStruct((t, d), F32),
        scratch_shapes=[pltpu.VMEM((bm, d), BF16)],
        compiler_params=_cparams(("arbitrary",)),
        name="out_proj_a",
    )(x, gate, y, u, v, w_sp, b_full, w1, w2)


def _out_c_kernel(x_ref, gate_ref, y_ref, w_ref, o_ref):
    out = jnp.dot(y_ref[...], w_ref[...], preferred_element_type=F32)
    o_ref[...] = x_ref[...] + gate_ref[...] * out


def _out_proj_c(x, gate, y, w, tok, *, bm):
    t, d = x.shape
    kdim = y.shape[1]
    group = tok.group_of_tile(bm)
    return pl.pallas_call(
        _out_c_kernel,
        grid=(t // bm,),
        in_specs=[
            pl.BlockSpec((bm, d), lambda i: (i, 0)),
            pl.BlockSpec((None, 1, d), lambda i: (group(i), 0, 0)),
            pl.BlockSpec((bm, kdim), lambda i: (i, 0)),
            pl.BlockSpec((kdim, d), lambda i: (0, 0)),
        ],
        out_specs=pl.BlockSpec((bm, d), lambda i: (i, 0)),
        out_shape=jax.ShapeDtypeStruct((t, d), F32),
        compiler_params=_cparams(("arbitrary",)),
        name="out_proj_c",
    )(x, gate, y, w)


def _split_kernel(w_ref, g_ref, l_ref):
    w = w_ref[...]
    k, n2 = w.shape
    half = LANES // 2
    lane = lax.broadcasted_iota(jnp.int32, (k, LANES), 1)
    first = lane < half
    idx = jnp.where(first, 2 * lane, 2 * (lane - half) + 1)
    gs, ls = [], []
    for j in range(n2 // (2 * LANES)):
        a = jnp.take_along_axis(w[:, (2 * j) * LANES:(2 * j + 1) * LANES], idx, axis=1)
        b = jnp.take_along_axis(w[:, (2 * j + 1) * LANES:(2 * j + 2) * LANES], idx, axis=1)
        gs.append(jnp.where(first, a, pltpu.roll(b, half, axis=1)))
        ls.append(jnp.where(first, pltpu.roll(a, half, axis=1), b))
    g_ref[...] = jnp.concatenate(gs, axis=1).astype(BF16)
    l_ref[...] = jnp.concatenate(ls, axis=1).astype(BF16)


def _split_gate_lin(w_gu):
    dl, e, k, n2 = w_gu.shape
    tn = 256
    spec_out = pl.BlockSpec((None, None, k, tn), lambda a, b, j: (a, b, 0, j))
    return pl.pallas_call(
        _split_kernel,
        grid=(dl, e, n2 // (2 * tn)),
        in_specs=[pl.BlockSpec((None, None, k, 2 * tn), lambda a, b, j: (a, b, 0, j))],
        out_specs=[spec_out, spec_out],
        out_shape=[jax.ShapeDtypeStruct((dl, e, k, n2 // 2), BF16)] * 2,
        compiler_params=_cparams(("arbitrary",) * 3),
        name="split_gate_lin",
    )(w_gu)


def _router_kernel(x_ref, gam_ref, sc_ref, sh_ref, wr_ref, br_ref, h_ref, e_ref, r_ref, g_ref, cnt_ref, cnt_scr):
    i = pl.program_id(0)
    bm = x_ref.shape[0]

    @pl.when(i == 0)
    def _():
        cnt_scr[...] = jnp.zeros_like(cnt_scr)

    h = _modulated_norm(x_ref[...], gam_ref[...], sc_ref[...], sh_ref[...])
    h_ref[...] = h
    logits = jnp.dot(h, wr_ref[...], precision=HI, preferred_element_type=F32) + br_ref[...]
    lane = lax.broadcasted_iota(jnp.int32, logits.shape, 1).astype(F32)
    vals, idxs = [], []
    work = logits
    for _ in range(TOP_K):
        m = jnp.max(work, axis=-1, keepdims=True)
        idx = jnp.min(jnp.where(work == m, lane, float(LANES)), axis=-1, keepdims=True)
        vals.append(m)
        idxs.append(idx)
        work = jnp.where(lane == idx, -jnp.inf, work)
    exps = [jnp.exp(v - vals[0]) for v in vals]
    inv = 1.0 / functools.reduce(lambda a, b: a + b, exps)
    hot = functools.reduce(jnp.logical_or, [lane == idx for idx in idxs])
    hot_f = hot.astype(F32)
    ri = lax.broadcasted_iota(jnp.int32, (bm, bm), 0)
    ci = lax.broadcasted_iota(jnp.int32, (bm, bm), 1)
    before = (ci < ri).astype(BF16)
    rank_all = cnt_scr[...] + jnp.dot(before, hot_f.astype(BF16), preferred_element_type=F32)
    e_out = jnp.zeros(logits.shape, F32)
    r_out = jnp.zeros(logits.shape, F32)
    g_out = jnp.zeros(logits.shape, F32)
    for k in range(TOP_K):
        rk = jnp.sum(jnp.where(lane == idxs[k], rank_all, 0.0), axis=-1, keepdims=True)
        e_out = jnp.where(lane == float(k), idxs[k], e_out)
        r_out = jnp.where(lane == float(k), rk, r_out)
        g_out = jnp.where(lane == float(k), exps[k] * inv, g_out)
    e_ref[...] = e_out.astype(jnp.int32)
    r_ref[...] = r_out.astype(jnp.int32)
    g_ref[...] = g_out
    cnt_scr[...] = cnt_scr[...] + jnp.sum(hot_f, axis=0, keepdims=True)
    cnt_ref[...] = cnt_scr[...]


def _router(x, gam, sc, sh, w_r, b_r, tok, *, bm):
    t, d = x.shape
    group = tok.group_of_tile(bm)
    row_spec = pl.BlockSpec((bm, LANES), lambda i: (i, 0))
    return pl.pallas_call(
        _router_kernel,
        grid=(t // bm,),
        in_specs=[
            pl.BlockSpec((bm, d), lambda i: (i, 0)),
            pl.BlockSpec((1, d), lambda i: (0, 0)),
            pl.BlockSpec((None, 1, d), lambda i: (group(i), 0, 0)),
            pl.BlockSpec((None, 1, d), lambda i: (group(i), 0, 0)),
            pl.BlockSpec((d, LANES), lambda i: (0, 0)),
            pl.BlockSpec((1, LANES), lambda i: (0, 0)),
        ],
        out_specs=[pl.BlockSpec((bm, d), lambda i: (i, 0)), row_spec, row_spec, row_spec,
                   pl.BlockSpec((1, LANES), lambda i: (0, 0))],
        out_shape=[jax.ShapeDtypeStruct((t, d), F32), jax.ShapeDtypeStruct((t, LANES), jnp.int32),
                   jax.ShapeDtypeStruct((t, LANES), jnp.int32), jax.ShapeDtypeStruct((t, LANES), F32),
                   jax.ShapeDtypeStruct((1, LANES), F32)],
        scratch_shapes=[pltpu.VMEM((1, LANES), F32)],
        compiler_params=_cparams(("arbitrary",)),
        name="moe_router",
    )(x, gam, sc, sh, w_r, b_r)


SLABS = D_MODEL // LANES


def _to_row_tiles(ref, x):
    for s in range(SLABS):
        ref[:, s, :] = x[:, s * LANES:(s + 1) * LANES]


def _from_row_tiles(ref):
    return jnp.concatenate([ref[:, s, :] for s in range(SLABS)], axis=1)


def _dispatch_kernel(pe_ref, dest_ref, h_ref, xs_hbm, stage, zbuf, sem, zsem):
    bm = h_ref.shape[0]

    @pl.when(pl.program_id(0) == 0)
    def _():
        zbuf[...] = jnp.zeros_like(zbuf)

        def tail_copy(e):
            return pltpu.make_async_copy(zbuf, xs_hbm.at[pl.ds(pe_ref[e + 1] - MOE_ROWS, MOE_ROWS)], zsem)

        for e in range(N_EXPERTS):
            @pl.when(pe_ref[e + 1] > pe_ref[e])
            def _():
                tail_copy(e).start()
        for e in range(N_EXPERTS):
            @pl.when(pe_ref[e + 1] > pe_ref[e])
            def _():
                tail_copy(e).wait()

        def spare_copy(b):
            return pltpu.make_async_copy(zbuf, xs_hbm.at[pl.ds(pl.multiple_of(b * MOE_ROWS, MOE_ROWS), MOE_ROWS)], zsem)

        first_spare = pe_ref[N_EXPERTS] // MOE_ROWS
        n_blk = xs_hbm.shape[0] // MOE_ROWS
        lax.fori_loop(first_spare, n_blk, lambda b, c: (spare_copy(b).start(), c)[1], 0)
        lax.fori_loop(first_spare, n_blk, lambda b, c: (spare_copy(b).wait(), c)[1], 0)

    _to_row_tiles(stage, h_ref[...])

    def issue(r, carry):
        for k in range(TOP_K):
            pltpu.make_async_copy(stage.at[r], xs_hbm.at[dest_ref[r * TOP_K + k]], sem).start()
        return carry

    lax.fori_loop(0, bm, issue, 0, unroll=2)
    for k in range(TOP_K):
        pltpu.make_async_copy(stage, xs_hbm.at[pl.ds(0, bm)], sem).wait()


def _dispatch(pad_bounds, dest, h, n_rows, *, bm):
    t, d = h.shape
    n_tiles = t // bm
    grid_spec = pltpu.PrefetchScalarGridSpec(
        num_scalar_prefetch=1,
        grid=(n_tiles,),
        in_specs=[
            pl.BlockSpec((None, None, bm * TOP_K), lambda i, pe: (i, 0, 0), memory_space=pltpu.SMEM),
            pl.BlockSpec((bm, d), lambda i, pe: (i, 0)),
        ],
        out_specs=pl.BlockSpec(memory_space=pl.ANY),
        scratch_shapes=[pltpu.VMEM((bm, SLABS, LANES), F32), pltpu.VMEM((MOE_ROWS, SLABS, LANES), F32),
                        pltpu.SemaphoreType.DMA, pltpu.SemaphoreType.DMA],
    )
    return pl.pallas_call(
        _dispatch_kernel,
        grid_spec=grid_spec,
        out_shape=jax.ShapeDtypeStruct((n_rows, SLABS, LANES), F32),
        compiler_params=_cparams(("arbitrary",)),
        name="moe_dispatch",
    )(pad_bounds, dest.reshape(n_tiles, 1, bm * TOP_K), h)


def _expert_kernel(be_ref, na_ref, x_ref, wg_ref, wl_ref, bg_ref, bl_ref, wd_ref, bd_ref, o_ref):
    @pl.when(pl.program_id(0) < na_ref[0])
    def _():
        x = _from_row_tiles(x_ref).astype(BF16)
        hg = jnp.dot(x, wg_ref[...], preferred_element_type=F32) + bg_ref[...]
        hl = jnp.dot(x, wl_ref[...], preferred_element_type=F32) + bl_ref[...]
        glu = jnp.minimum(hg, SWIGLU_LIMIT)
        lin = jnp.clip(hl, -SWIGLU_LIMIT, SWIGLU_LIMIT)
        act = glu * (1.0 / (1.0 + jnp.exp(-SWIGLU_ALPHA * glu))) * (lin + 1.0)
        y = jnp.dot(act.astype(BF16), wd_ref[...], preferred_element_type=F32) + bd_ref[...]
        _to_row_tiles(o_ref, y)

    @pl.when(pl.program_id(0) >= na_ref[0])
    def _():
        o_ref[...] = jnp.zeros_like(o_ref)


def _experts(blk_expert, n_active, xs, wg, wl, bg, bl, wd, bd, layer):
    n_rows = xs.shape[0]
    d, dff = wg.shape[2], wg.shape[3]
    n_blk = n_rows // MOE_ROWS
    wmap = lambda i, be, na: (layer, be[i], 0, 0)
    rmap = lambda i, be, na: (jnp.minimum(i, na[0] - 1), 0, 0)
    grid_spec = pltpu.PrefetchScalarGridSpec(
        num_scalar_prefetch=2,
        grid=(n_blk,),
        in_specs=[
            pl.BlockSpec((MOE_ROWS, SLABS, LANES), rmap),
            pl.BlockSpec((None, None, d, dff), wmap),
            pl.BlockSpec((None, None, d, dff), wmap),
            pl.BlockSpec((None, None, 1, dff), wmap),
            pl.BlockSpec((None, None, 1, dff), wmap),
            pl.BlockSpec((None, None, dff, d), wmap),
            pl.BlockSpec((None, None, 1, d), wmap),
        ],
        out_specs=pl.BlockSpec((MOE_ROWS, SLABS, LANES), lambda i, be, na: (i, 0, 0)),
    )
    return pl.pallas_call(
        _expert_kernel,
        grid_spec=grid_spec,
        out_shape=jax.ShapeDtypeStruct((n_rows, SLABS, LANES), F32),
        compiler_params=_cparams(("arbitrary",)),
        name="moe_experts",
    )(blk_expert, n_active, xs, wg, wl, bg, bl, wd, bd)


def _combine_kernel(dest_ref, g_ref, x_ref, gate_ref, gam_ref, ys_hbm, o_ref, *rest, final_norm):
    if final_norm:
        of_ref, buf, acc_scr, sem = rest
    else:
        buf, acc_scr, sem = rest
    bm = x_ref.shape[0]

    def issue(r, carry):
        for k in range(TOP_K):
            pltpu.make_async_copy(ys_hbm.at[dest_ref[r * TOP_K + k]], buf.at[k, r], sem).start()
        return carry

    lax.fori_loop(0, bm, issue, 0, unroll=2)
    for k in range(TOP_K):
        pltpu.make_async_copy(ys_hbm.at[pl.ds(0, bm)], buf.at[k], sem).wait()

    def mix(r, carry):
        acc = g_ref[r * TOP_K] * buf[0, r]
        for k in range(1, TOP_K):
            acc = acc + g_ref[r * TOP_K + k] * buf[k, r]
        acc_scr[r] = acc
        return carry

    lax.fori_loop(0, bm, mix, 0, unroll=8)
    xn = x_ref[...] + gate_ref[...] * _from_row_tiles(acc_scr)
    o_ref[...] = xn
    if final_norm:
        ms = jnp.mean(xn * xn, axis=-1, keepdims=True)
        of_ref[...] = xn * lax.rsqrt(ms + EPS) * gam_ref[...]


def _combine(dest, x, gate, gates, gam_final, ys, tok, *, bm, final_norm):
    t, d = x.shape
    group = tok.group_of_tile(bm)
    n_tiles = t // bm
    out_shape = [jax.ShapeDtypeStruct((t, d), F32)]
    out_specs = [pl.BlockSpec((bm, d), lambda i: (i, 0))]
    if final_norm:
        out_shape.append(jax.ShapeDtypeStruct((t, d), F32))
        out_specs.append(pl.BlockSpec((bm, d), lambda i: (i, 0)))
    return pl.pallas_call(
        functools.partial(_combine_kernel, final_norm=final_norm),
        grid=(n_tiles,),
        in_specs=[
            pl.BlockSpec((None, None, bm * TOP_K), lambda i: (i, 0, 0), memory_space=pltpu.SMEM),
            pl.BlockSpec((None, None, bm * TOP_K), lambda i: (i, 0, 0), memory_space=pltpu.SMEM),
            pl.BlockSpec((bm, d), lambda i: (i, 0)),
            pl.BlockSpec((None, 1, d), lambda i: (group(i), 0, 0)),
            pl.BlockSpec((1, d), lambda i: (0, 0)),
            pl.BlockSpec(memory_space=pl.ANY),
        ],
        out_specs=out_specs,
        out_shape=out_shape,
        scratch_shapes=[pltpu.VMEM((TOP_K, bm, SLABS, LANES), F32), pltpu.VMEM((bm, SLABS, LANES), F32),
                        pltpu.SemaphoreType.DMA],
        compiler_params=_cparams(("arbitrary",)),
        name="moe_combine_final" if final_norm else "moe_combine",
    )(dest.reshape(n_tiles, 1, bm * TOP_K), gates[:, :TOP_K].reshape(n_tiles, 1, bm * TOP_K), x, gate, gam_final, ys)


def _moe(x, gam, sc, sh, gate, w_r, b_r, wg, wl, bg, bl, wd, bd, layer, gam_final, tok, *, final_norm):
    t, d = x.shape
    h, e_out, r_out, gates, counts = _router(x, gam, sc, sh, w_r, b_r, tok, bm=512)
    counts = counts[0, :N_EXPERTS].astype(jnp.int32)
    padded = (counts + MOE_ROWS - 1) // MOE_ROWS * MOE_ROWS
    pad_end = jnp.cumsum(padded)
    pad_start = pad_end - padded
    e_sel = e_out[:, :TOP_K]
    dest = pad_start[e_sel] + r_out[:, :TOP_K]
    n_rows = t * TOP_K + N_EXPERTS * MOE_ROWS
    n_blk = n_rows // MOE_ROWS
    blk_start = jnp.arange(n_blk, dtype=jnp.int32) * MOE_ROWS
    blk_expert = jnp.minimum(jnp.sum((pad_end[None, :] <= blk_start[:, None]).astype(jnp.int32), axis=1),
                             N_EXPERTS - 1)
    n_active = (pad_end[-1:] // MOE_ROWS).astype(jnp.int32)
    pad_bounds = jnp.concatenate([jnp.zeros((1,), jnp.int32), pad_end.astype(jnp.int32)])
    xs = _dispatch(pad_bounds, dest, h, n_rows, bm=256)
    ys = _experts(blk_expert, n_active, xs, wg, wl, bg, bl, wd, bd, layer)
    return _combine(dest, x, gate, gates, gam_final, ys, tok, bm=256, final_norm=final_norm)


def _rope_tables(tok):
    rows = tok.dec_len // GRID_W
    r = jnp.repeat(jnp.arange(rows), GRID_W).astype(F32)
    cidx = jnp.tile(jnp.arange(GRID_W), rows).astype(F32)
    n_freq = DK_RET // 4
    inv = ROPE_BASE ** (-jnp.arange(n_freq, dtype=F32) / n_freq)
    ang = jnp.concatenate([r[:, None] * inv, cidx[:, None] * inv], axis=-1)
    cos = jnp.concatenate([jnp.ones((tok.t_ctx, DK_RET // 2), F32), jnp.tile(jnp.cos(ang), (tok.n_dec_seq, 1))])
    sin = jnp.concatenate([jnp.zeros((tok.t_ctx, DK_RET // 2), F32), jnp.tile(jnp.sin(ang), (tok.n_dec_seq, 1))])
    return cos, sin


def _group_cols(p):
    h = p.shape[1]
    a = p.reshape(2, SSD_GROUPS, h // SSD_GROUPS).transpose(1, 0, 2).reshape(SSD_GROUPS, -1)
    return jnp.pad(a, ((0, 0), (0, LANES - a.shape[1])))[:, None, :]


def kernel(x_prompt, x_sample, state_ssd, state_ret, c, c_ctx, w_mod, b_mod, norm_mix, norm_ffn, w_in_a, conv_w, conv_b, dt_bias, a_log, d_skip, ssd_norm, w_sp, b_sp, w_out_a, w_in_c, decay_logit, w_out_c, w_router, b_router, w_gu, b_gu, w_down, b_down, norm_final):
    n_ctx, ctx_len, d = x_prompt.shape
    n_dec, dec_len, _ = x_sample.shape
    tok = _Tokens(n_ctx, ctx_len, n_dec, dec_len)
    depth = w_mod.shape[0]
    x = jnp.concatenate([x_prompt.reshape(tok.t_ctx, d), x_sample.reshape(tok.t_dec, d)])

    cvecs = jnp.concatenate([c_ctx[None], c, jnp.zeros((MOD_ROWS - 1 - n_dec, d), F32)])
    mod = _modulation(cvecs, w_mod, b_mod)
    mod = mod.reshape(depth, MOD_ROWS, 6, 1, d).transpose(0, 2, 1, 3, 4)

    h_ssd = a_log.shape[2]
    xbc_w = d + 2 * SSD_GROUPS * D_STATE
    o1, o2, o3 = d, d + xbc_w, d + xbc_w + 2 * h_ssd
    cos, sin = _rope_tables(tok)
    wg_all, wl_all = _split_gate_lin(w_gu)
    bg_all = b_gu[:, :, None, 0::2]
    bl_all = b_gu[:, :, None, 1::2]
    wd_all = w_down.astype(BF16)
    bd_all = b_down[:, :, None, :]
    new_ssd, new_ret = [], []
    y_final = None
    bm_proj = 512

    for l in range(depth):
        sh1, sc1, g1, sh2, sc2, g2 = (mod[l, j] for j in range(6))
        gam_mix = norm_mix[l][None]
        i = l // 2
        if l % 2 == 0:
            w_in = w_in_a[i]
            w_z = w_in[:, :o1].astype(BF16)
            w_xbc = w_in[:, o1:o2].astype(BF16)
            w_dt = w_in[:, o2:o3].reshape(d, 2, SSD_GROUPS, h_ssd // SSD_GROUPS).transpose(0, 2, 1, 3)
            w_dt = jnp.pad(w_dt.reshape(d, SSD_GROUPS, -1), ((0, 0), (0, 0), (0, LANES - 2 * h_ssd // SSD_GROUPS)))
            w_dt = w_dt.reshape(d, SSD_GROUPS * LANES).astype(BF16)
            w_uv = w_in[:, o3:].astype(BF16)
            proj = functools.partial(_norm_proj, x, gam_mix, sc1, sh1, tok=tok, bm=bm_proj)
            sz = proj(w_z, tn=512, out_dtype=BF16, epilogue=_ep_silu, name="proj_z")
            xbc = _norm_proj(x, gam_mix, sc1, sh1, w_xbc, tok=tok, bm=dec_len, tn=256, out_dtype=BF16,
                             epilogue=_ep_conv_silu, extra=(conv_w[i], conv_b[i][None]),
                             extra_specs=(pl.BlockSpec((4, 256), lambda r, j: (0, j)),
                                          pl.BlockSpec((1, 256), lambda r, j: (0, j))), name="proj_xbc")
            dt_raw = proj(w_dt, tn=SSD_GROUPS * LANES, out_dtype=F32, epilogue=_ep_plain, name="proj_dt")
            u = proj(w_uv[:, :d], tn=512, out_dtype=BF16, epilogue=_ep_gelu, name="proj_u")
            v = proj(w_uv[:, d:], tn=d, out_dtype=BF16, epilogue=_ep_gelu_ln, name="proj_v")
            dtb = _group_cols(dt_bias[i])
            a_neg = _group_cols(-jnp.exp(a_log[i]))
            dsk = jnp.repeat(d_skip[i], SSD_HEAD).reshape(SSD_GROUPS, 1, GROUP_W)
            nrm = ssd_norm[i].reshape(SSD_GROUPS, 1, GROUP_W)
            s0 = state_ssd[:, i].reshape(n_dec, 2, SSD_GROUPS, HEADS_PER_GROUP, D_STATE, SSD_HEAD)
            s0 = s0.transpose(0, 2, 1, 4, 3, 5).reshape(n_dec, SSD_GROUPS, 2, D_STATE, GROUP_W)
            y_ctx, s_fin = _ssd_scan(xbc, dt_raw, sz, dtb, a_neg, dsk, nrm, None, row0=0, n_seq=n_ctx,
                                     seq_len=ctx_len, want_final=True)
            (y_dec,) = _ssd_scan(xbc, dt_raw, sz, dtb, a_neg, dsk, nrm, s0, row0=tok.t_ctx, n_seq=n_dec,
                                 seq_len=dec_len, want_final=False)
            s_fin = s_fin.reshape(n_ctx, SSD_GROUPS, 2, D_STATE, HEADS_PER_GROUP, SSD_HEAD)
            new_ssd.append(s_fin.transpose(0, 2, 1, 4, 3, 5).reshape(n_ctx, 2, h_ssd, D_STATE, SSD_HEAD))
            y = jnp.concatenate([y_ctx, y_dec])
            b_full = jnp.repeat(b_sp[i].T, LANES, axis=1)
            w_o = w_out_a[i].astype(BF16)
            x = _out_proj_a(x, g1, y, u, v, w_sp[i].astype(BF16), b_full, w_o[:d], w_o[d:], tok, bm=512)
        else:
            hk = H_RET * DK_RET
            hv = H_RET * DV_RET
            w_in = w_in_c[i].astype(BF16)
            rope_specs = (pl.BlockSpec((bm_proj, DK_RET // 2), lambda r, j: (r, 0)),) * 2
            proj = functools.partial(_norm_proj, x, gam_mix, sc1, sh1, tok=tok, bm=bm_proj)
            q = proj(w_in[:, :hk], tn=512, out_dtype=BF16, epilogue=functools.partial(_ep_rope, scale=1.0),
                     extra=(cos, sin), extra_specs=rope_specs, name="proj_q")
            kk = proj(w_in[:, hk:2 * hk], tn=512, out_dtype=BF16,
                      epilogue=functools.partial(_ep_rope, scale=DK_RET ** -0.5),
                      extra=(cos, sin), extra_specs=rope_specs, name="proj_k")
            vv = proj(w_in[:, 2 * hk:2 * hk + hv], tn=512, out_dtype=BF16, epilogue=_ep_plain, name="proj_rv")
            sg = proj(w_in[:, 2 * hk + hv:], tn=512, out_dtype=BF16, epilogue=_ep_silu, name="proj_rg")
            log_g = jax.nn.log_sigmoid(decay_logit[i].astype(F32))
            y_ctx, s_fin = _ret_scan(log_g, q, kk, vv, sg, None, row0=0, n_seq=n_ctx, seq_len=ctx_len, want_final=True)
            (y_dec,) = _ret_scan(log_g, q, kk, vv, sg, state_ret[:, i], row0=tok.t_ctx, n_seq=n_dec, seq_len=dec_len,
                                 want_final=False)
            new_ret.append(s_fin)
            y = jnp.concatenate([y_ctx, y_dec])
            x = _out_proj_c(x, g1, y, w_out_c[i].astype(BF16), tok, bm=512)

        w_r = jnp.pad(w_router[l], ((0, 0), (0, LANES - N_EXPERTS)))
        b_r = jnp.pad(b_router[l], (0, LANES - N_EXPERTS), constant_values=-1e30)[None]
        last = l == depth - 1
        res = _moe(x, norm_ffn[l][None], sc2, sh2, g2, w_r, b_r, wg_all, wl_all, bg_all, bl_all, wd_all, bd_all, l,
                   norm_final[None], tok, final_norm=last)
        x = res[0]
        if last:
            y_final = res[1]

    y_prompt = y_final[:tok.t_ctx].reshape(n_ctx, ctx_len, d)
    y_sample = y_final[tok.t_ctx:].reshape(n_dec, dec_len, d)
    return (y_prompt, y_sample, jnp.stack(new_ssd, axis=1), jnp.stack(new_ret, axis=1))
```

```python
import functools
import math

import jax
import jax.numpy as jnp
from jax import lax
from jax.experimental import pallas as pl
from jax.experimental.pallas import tpu as pltpu

D_MODEL = 1024
GRID_W = 64
CHUNK = 128
SSD_HEAD = 64
SSD_GROUPS = 2
D_STATE = 128
GROUP_W = D_MODEL // SSD_GROUPS
HEADS_PER_GROUP = GROUP_W // SSD_HEAD
SG_GROUPS = 8
H_RET = 4
DK_RET = D_MODEL // H_RET
DV_RET = 2 * DK_RET
ROPE_BASE = 10000.0
N_EXPERTS = 32
TOP_K = 4
SWIGLU_LIMIT = 7.0
SWIGLU_ALPHA = 1.702
EPS = 1e-6

LANES = 128
MOD_ROWS = 8
MOE_ROWS = 256
VMEM_LIMIT = 56 * 1024 * 1024

F32 = jnp.float32
BF16 = jnp.bfloat16
HI = lax.Precision.HIGHEST


def _cparams(sem):
    return pltpu.CompilerParams(dimension_semantics=sem, vmem_limit_bytes=VMEM_LIMIT)


def _silu(x):
    return x * (1.0 / (1.0 + jnp.exp(-x)))


def _gelu_tanh(x):
    return 0.5 * x * (1.0 + jnp.tanh(math.sqrt(2.0 / math.pi) * (x + 0.044715 * (x * x * x))))


def _softplus(x):
    return jnp.maximum(x, 0.0) + jnp.log(1.0 + jnp.exp(-jnp.abs(x)))


def _mod_kernel(c_ref, w_ref, b_ref, o_ref):
    a = _silu(c_ref[...])
    o_ref[...] = jnp.dot(a, w_ref[...], precision=HI, preferred_element_type=F32) + b_ref[...]


def _modulation(cvecs, w_mod, b_mod):
    depth, d, n = w_mod.shape
    tn = 1536
    return pl.pallas_call(
        _mod_kernel,
        grid=(depth, n // tn),
        in_specs=[
            pl.BlockSpec((MOD_ROWS, d), lambda l, j: (0, 0)),
            pl.BlockSpec((None, d, tn), lambda l, j: (l, 0, j)),
            pl.BlockSpec((None, 1, tn), lambda l, j: (l, 0, j)),
        ],
        out_specs=pl.BlockSpec((None, MOD_ROWS, tn), lambda l, j: (l, 0, j)),
        out_shape=jax.ShapeDtypeStruct((depth, MOD_ROWS, n), F32),
        compiler_params=_cparams(("arbitrary", "arbitrary")),
        name="modulation",
    )(cvecs, w_mod, b_mod.reshape(depth, 1, n))


class _Tokens:
    def __init__(self, n_ctx_seq, ctx_len, n_dec_seq, dec_len):
        self.n_ctx_seq, self.ctx_len = n_ctx_seq, ctx_len
        self.n_dec_seq, self.dec_len = n_dec_seq, dec_len
        self.t_ctx = n_ctx_seq * ctx_len
        self.t_dec = n_dec_seq * dec_len
        self.total = self.t_ctx + self.t_dec

    def group_of_tile(self, bm):
        assert self.t_ctx % bm == 0 and self.dec_len % bm == 0
        n_ctx_tiles = self.t_ctx // bm
        per_seq = self.dec_len // bm

        def group(i):
            return jnp.where(i < n_ctx_tiles, 0, 1 + (i - n_ctx_tiles) // per_seq)

        return group


def _modulated_norm(x, gam, sc, sh):
    ms = jnp.mean(x * x, axis=-1, keepdims=True)
    return (x * lax.rsqrt(ms + EPS) * gam) * (1.0 + sc) + sh


def _proj_kernel(*refs, epilogue, n_extra, tok, bm):
    x_ref, gam_ref, sc_ref, sh_ref, w_ref = refs[:5]
    extra = refs[5:5 + n_extra]
    o_ref = refs[5 + n_extra]
    h_scr = refs[6 + n_extra]
    i = pl.program_id(0)

    @pl.when(pl.program_id(1) == 0)
    def _():
        h_scr[...] = _modulated_norm(x_ref[...], gam_ref[...], sc_ref[...], sh_ref[...]).astype(BF16)

    acc = jnp.dot(h_scr[...], w_ref[...], preferred_element_type=F32)
    o_ref[...] = epilogue(acc, i, tok, bm, *extra).astype(o_ref.dtype)


def _ep_plain(acc, i, tok, bm):
    return acc


def _ep_silu(acc, i, tok, bm):
    return _silu(acc)


def _ep_gelu(acc, i, tok, bm):
    return _gelu_tanh(acc)


def _ep_gelu_ln(acc, i, tok, bm):
    g = _gelu_tanh(acc)
    mu = jnp.mean(g, axis=-1, keepdims=True)
    gc = g - mu
    return gc * lax.rsqrt(jnp.mean(gc * gc, axis=-1, keepdims=True) + 1e-5)


def _ep_conv_silu(acc, i, tok, bm, cw_ref, cb_ref):
    n = acc.shape[0]
    seq = jnp.where(i * bm < tok.t_ctx, tok.ctx_len, tok.dec_len)
    t = lax.broadcasted_iota(jnp.int32, (n, 1), 0) & (seq - 1)
    cw = cw_ref[...]
    y = acc * cw[2:3, :] + cb_ref[...]
    y = y + jnp.where(t >= 2, pltpu.roll(acc, 2, axis=0), 0.0) * cw[0:1, :]
    y = y + jnp.where(t >= 1, pltpu.roll(acc, 1, axis=0), 0.0) * cw[1:2, :]
    y = y + jnp.where(t < seq - 1, pltpu.roll(acc, n - 1, axis=0), 0.0) * cw[3:4, :]
    return _silu(y)


def _ep_rope(acc, i, tok, bm, cos_ref, sin_ref, *, scale):
    cs, sn = cos_ref[...], sin_ref[...]
    half = DK_RET // 2
    outs = []
    for h in range(acc.shape[1] // DK_RET):
        x1 = acc[:, h * DK_RET:h * DK_RET + half] * scale
        x2 = acc[:, h * DK_RET + half:(h + 1) * DK_RET] * scale
        outs.append(x1 * cs - x2 * sn)
        outs.append(x2 * cs + x1 * sn)
    return jnp.concatenate(outs, axis=1)


def _norm_proj(x, gam, sc, sh, w, tok, *, bm, tn, out_dtype, epilogue, extra=(), extra_specs=(), name):
    t, d = x.shape
    n = w.shape[1]
    group = tok.group_of_tile(bm)
    kern = functools.partial(_proj_kernel, epilogue=epilogue, n_extra=len(extra), tok=tok, bm=bm)
    return pl.pallas_call(
        kern,
        grid=(t // bm, n // tn),
        in_specs=[
            pl.BlockSpec((bm, d), lambda i, j: (i, 0)),
            pl.BlockSpec((1, d), lambda i, j: (0, 0)),
            pl.BlockSpec((None, 1, d), lambda i, j: (group(i), 0, 0)),
            pl.BlockSpec((None, 1, d), lambda i, j: (group(i), 0, 0)),
            pl.BlockSpec((d, tn), lambda i, j: (0, j)),
            *extra_specs,
        ],
        out_specs=pl.BlockSpec((bm, tn), lambda i, j: (i, j)),
        out_shape=jax.ShapeDtypeStruct((t, n), out_dtype),
        scratch_shapes=[pltpu.VMEM((bm, d), BF16)],
        compiler_params=_cparams(("arbitrary", "arbitrary")),
        name=name,
    )(x, gam, sc, sh, w, *extra)


def _proj_multi_kernel(*refs, pieces, n_extra, tok, bm):
    x_ref, gam_ref, sc_ref, sh_ref, w_ref = refs[:5]
    extra = refs[5:5 + n_extra]
    outs = refs[5 + n_extra:5 + n_extra + len(pieces)]
    h_scr = refs[5 + n_extra + len(pieces)]
    i = pl.program_id(0)
    h_scr[...] = _modulated_norm(x_ref[...], gam_ref[...], sc_ref[...], sh_ref[...]).astype(BF16)
    c0 = 0
    for (width, chunk, _, epilogue, uses_extra), o_ref in zip(pieces, outs):
        for j in range(width // chunk):
            acc = jnp.dot(h_scr[...], w_ref[:, c0 + j * chunk:c0 + (j + 1) * chunk], preferred_element_type=F32)
            res = epilogue(acc, i, tok, bm, *(extra if uses_extra else ()))
            o_ref[:, j * chunk:(j + 1) * chunk] = res.astype(o_ref.dtype)
        c0 += width


def _norm_proj_multi(x, gam, sc, sh, w, tok, pieces, *, bm, extra=(), extra_specs=(), name):
    t, d = x.shape
    n = w.shape[1]
    assert n == sum(p[0] for p in pieces)
    group = tok.group_of_tile(bm)
    kern = functools.partial(_proj_multi_kernel, pieces=tuple(pieces), n_extra=len(extra), tok=tok, bm=bm)
    return pl.pallas_call(
        kern,
        grid=(t // bm,),
        in_specs=[
            pl.BlockSpec((bm, d), lambda i: (i, 0)),
            pl.BlockSpec((1, d), lambda i: (0, 0)),
            pl.BlockSpec((None, 1, d), lambda i: (group(i), 0, 0)),
            pl.BlockSpec((None, 1, d), lambda i: (group(i), 0, 0)),
            pl.BlockSpec((d, n), lambda i: (0, 0), pipeline_mode=pl.Buffered(1)),
            *extra_specs,
        ],
        out_specs=[pl.BlockSpec((bm, p[0]), lambda i: (i, 0)) for p in pieces],
        out_shape=[jax.ShapeDtypeStruct((t, p[0]), p[2]) for p in pieces],
        scratch_shapes=[pltpu.VMEM((bm, d), BF16)],
        compiler_params=_cparams(("arbitrary",)),
        name=name,
    )(x, gam, sc, sh, w, *extra)


def _ssd_kernel(*refs, n_chunks, has_init, has_final):
    xs_ref, b_ref, c_ref, dt_ref, sz_ref, dtb_ref, a_ref, dsk_ref, nrm_ref = refs[:9]
    k = 9
    s0_ref = None
    if has_init:
        s0_ref = refs[k]
        k += 1
    y_ref = refs[k]
    k += 1
    sfin_ref = None
    if has_final:
        sfin_ref = refs[k]
        k += 1
    yacc, s_scr = refs[k], refs[k + 1]

    q = CHUNK
    row = lax.broadcasted_iota(jnp.int32, (q, q), 0)
    col = lax.broadcasted_iota(jnp.int32, (q, q), 1)
    lane = lax.broadcasted_iota(jnp.int32, (1, q), 1)
    left = lane < SSD_HEAD
    dtb = dtb_ref[...]
    a_neg = a_ref[...]
    n_pairs = HEADS_PER_GROUP // 2

    def chunk(ci, direction):
        r0 = pl.multiple_of(ci * q, q)
        xs = xs_ref[pl.ds(r0, q), :]
        bm_ = b_ref[pl.ds(r0, q), :]
        cm = c_ref[pl.ds(r0, q), :]
        dt = _softplus(dt_ref[pl.ds(r0, q), :] + dtb)
        la = dt * a_neg
        if direction == 0:
            tri = (col <= row).astype(F32)
            keep = col <= row
            last = q - 1
        else:
            tri = (col >= row).astype(F32)
            keep = col >= row
            last = 0
        cum = jnp.dot(tri, la, precision=HI, preferred_element_type=F32)
        cum_t = cum.T
        dt_t = dt.T
        tot_t = jnp.broadcast_to(cum_t[:, last:last + 1], (q, q))
        w_t = dt_t * jnp.exp(tot_t - cum_t)
        g = lax.dot_general(cm, bm_, (((1,), (1,)), ((), ())), preferred_element_type=F32)
        b_t = bm_.astype(F32).T
        cm_f = cm.astype(F32)
        outs = []
        for p in range(n_pairs):
            xs_p = xs[:, p * LANES:(p + 1) * LANES]
            s_p = s_scr[:, p * LANES:(p + 1) * LANES]
            s_b = s_p.astype(BF16)
            zero = jnp.zeros_like(xs_p)
            zero_s = jnp.zeros_like(s_b)
            lhs, rhs, lhs_s, rhs_s, decs = [], [], [], [], []
            for hh in range(2):
                cidx = direction * HEADS_PER_GROUP + 2 * p + hh
                cum_b = jnp.broadcast_to(cum[:, cidx:cidx + 1], (q, q))
                dec = jnp.exp(jnp.where(keep, cum_b - cum_t[cidx:cidx + 1, :], -jnp.inf))
                scores = g * dec * dt_t[cidx:cidx + 1, :]
                lhs += [scores.astype(BF16), (cm_f * jnp.exp(cum_b)).astype(BF16)]
                sel = left if hh == 0 else jnp.logical_not(left)
                rhs += [jnp.where(sel, xs_p, zero), jnp.where(sel, s_b, zero_s)]
                lhs_s.append((b_t * w_t[cidx:cidx + 1, :]).astype(BF16))
                rhs_s.append(jnp.where(sel, xs_p, zero))
                decs.append(jnp.exp(cum_t[cidx:cidx + 1, last:last + 1]))
            y_p = jnp.dot(jnp.concatenate(lhs, axis=1), jnp.concatenate(rhs, axis=0),
                          preferred_element_type=F32)
            upd = jnp.dot(jnp.concatenate(lhs_s, axis=1), jnp.concatenate(rhs_s, axis=0),
                          preferred_element_type=F32)
            s_scr[:, p * LANES:(p + 1) * LANES] = s_p * jnp.where(left, decs[0], decs[1]) + upd
            outs.append(y_p)
        return jnp.concatenate(outs, axis=1)

    def init_state(direction):
        if has_init:
            s_scr[...] = s0_ref[direction]
        else:
            s_scr[...] = jnp.zeros_like(s_scr)

    init_state(0)

    def fwd_body(ci, carry):
        r0 = pl.multiple_of(ci * q, q)
        yacc[pl.ds(r0, q), :] = chunk(ci, 0)
        return carry

    lax.fori_loop(0, n_chunks, fwd_body, 0)
    if has_final:
        sfin_ref[0] = s_scr[...]
    init_state(1)

    def bwd_body(k2, carry):
        ci = n_chunks - 1 - k2
        r0 = pl.multiple_of(ci * q, q)
        y = yacc[pl.ds(r0, q), :] + chunk(ci, 1)
        y = y + dsk_ref[...] * xs_ref[pl.ds(r0, q), :].astype(F32)
        y = y * sz_ref[pl.ds(r0, q), :].astype(F32)
        y = y * lax.rsqrt(jnp.mean(y * y, axis=-1, keepdims=True) + EPS)
        y_ref[pl.ds(r0, q), :] = (y * nrm_ref[...]).astype(y_ref.dtype)
        return carry

    lax.fori_loop(0, n_chunks, bwd_body, 0)
    if has_final:
        sfin_ref[1] = s_scr[...]


def _ssd_scan(xbc, dt_raw, sz, dtb, a_neg, dsk, nrm, s0, *, row0, n_seq, seq_len, want_final):
    assert row0 % seq_len == 0
    b0 = row0 // seq_len
    has_init = s0 is not None
    gw = GROUP_W
    n_xs = D_MODEL // gw
    b_blk0 = D_MODEL // D_STATE
    c_blk0 = b_blk0 + SSD_GROUPS
    in_specs = [
        pl.BlockSpec((seq_len, gw), lambda b, g: (b0 + b, g)),
        pl.BlockSpec((seq_len, D_STATE), lambda b, g: (b0 + b, b_blk0 + g)),
        pl.BlockSpec((seq_len, D_STATE), lambda b, g: (b0 + b, c_blk0 + g)),
        pl.BlockSpec((seq_len, LANES), lambda b, g: (b0 + b, g)),
        pl.BlockSpec((seq_len, gw), lambda b, g: (b0 + b, g)),
        pl.BlockSpec((None, 1, LANES), lambda b, g: (g, 0, 0)),
        pl.BlockSpec((None, 1, LANES), lambda b, g: (g, 0, 0)),
        pl.BlockSpec((None, 1, gw), lambda b, g: (g, 0, 0)),
        pl.BlockSpec((None, 1, gw), lambda b, g: (g, 0, 0)),
    ]
    args = [xbc, xbc, xbc, dt_raw, sz, dtb, a_neg, dsk, nrm]
    del n_xs
    if has_init:
        in_specs.append(pl.BlockSpec((None, None, 2, D_STATE, gw), lambda b, g: (b, g, 0, 0, 0)))
        args.append(s0)
    t = xbc.shape[0]
    out_shape = [jax.ShapeDtypeStruct((n_seq * seq_len, D_MODEL), BF16)]
    out_specs = [pl.BlockSpec((seq_len, gw), lambda b, g: (b, g))]
    if want_final:
        out_shape.append(jax.ShapeDtypeStruct((n_seq, SSD_GROUPS, 2, D_STATE, gw), F32))
        out_specs.append(pl.BlockSpec((None, None, 2, D_STATE, gw), lambda b, g: (b, g, 0, 0, 0)))
    del t
    kern = functools.partial(_ssd_kernel, n_chunks=seq_len // CHUNK, has_init=has_init, has_final=want_final)
    return pl.pallas_call(
        kern,
        grid=(n_seq, SSD_GROUPS),
        in_specs=in_specs,
        out_specs=out_specs,
        out_shape=out_shape,
        scratch_shapes=[pltpu.VMEM((seq_len, gw), F32), pltpu.VMEM((D_STATE, gw), F32)],
        compiler_params=_cparams(("arbitrary", "arbitrary")),
        name="ssd_scan_ctx" if want_final else "ssd_scan_dec",
    )(*args)


def _ret_kernel(*refs, n_chunks, has_init, has_final):
    lg_ref, q_ref, k_ref, v_ref, sg_ref = refs[:5]
    k = 5
    s0_ref = None
    if has_init:
        s0_ref = refs[k]
        k += 1
    y_ref = refs[k]
    k += 1
    sfin_ref = None
    if has_final:
        sfin_ref = refs[k]
        k += 1
    yacc, s_scr = refs[k], refs[k + 1]

    qn = CHUNK
    h = pl.program_id(1)
    row = lax.broadcasted_iota(jnp.int32, (qn, qn), 0)
    col = lax.broadcasted_iota(jnp.int32, (qn, qn), 1)
    rowk = lax.broadcasted_iota(jnp.int32, (qn, DK_RET), 0).astype(F32)

    def tables(direction):
        lg = lg_ref[direction, h]
        if direction == 0:
            keep = col <= row
            dist = (row - col).astype(F32)
            e_q = jnp.exp(lg * (rowk + 1.0))
            w_k = jnp.exp(lg * (qn - 1.0 - rowk))
        else:
            keep = col >= row
            dist = (col - row).astype(F32)
            e_q = jnp.exp(lg * (qn - rowk))
            w_k = jnp.exp(lg * rowk)
        dmat = jnp.where(keep, jnp.exp(lg * dist), 0.0)
        return dmat, e_q, w_k, jnp.exp(jnp.full((1, 1), float(qn), F32) * lg)

    def chunk(ci, tabs):
        dmat, e_q, w_k, dec = tabs
        r0 = pl.multiple_of(ci * qn, qn)
        qc = q_ref[pl.ds(r0, qn), :]
        kc = k_ref[pl.ds(r0, qn), :]
        vc = v_ref[pl.ds(r0, qn), :]
        scores = lax.dot_general(qc, kc, (((1,), (1,)), ((), ())), preferred_element_type=F32) * dmat
        s_old = s_scr[...]
        lhs = jnp.concatenate([scores.astype(BF16), (qc.astype(F32) * e_q).astype(BF16)], axis=1)
        rhs = jnp.concatenate([vc, s_old.astype(BF16)], axis=0)
        y = jnp.dot(lhs, rhs, preferred_element_type=F32)
        kw_t = (kc.astype(F32) * w_k).T.astype(BF16)
        s_scr[...] = s_old * dec + jnp.dot(kw_t, vc, preferred_element_type=F32)
        return y

    def init_state(direction):
        if has_init:
            s_scr[...] = s0_ref[direction]
        else:
            s_scr[...] = jnp.zeros_like(s_scr)

    init_state(0)
    tabs_f = tables(0)

    def fwd_body(ci, carry):
        r0 = pl.multiple_of(ci * qn, qn)
        yacc[pl.ds(r0, qn), :] = chunk(ci, tabs_f)
        return carry

    lax.fori_loop(0, n_chunks, fwd_body, 0)
    if has_final:
        sfin_ref[0] = s_scr[...]
    init_state(1)
    tabs_b = tables(1)

    def bwd_body(k2, carry):
        ci = n_chunks - 1 - k2
        r0 = pl.multiple_of(ci * qn, qn)
        y = yacc[pl.ds(r0, qn), :] + chunk(ci, tabs_b)
        y = y * lax.rsqrt(jnp.mean(y * y, axis=-1, keepdims=True) + EPS)
        y_ref[pl.ds(r0, qn), :] = (y * sg_ref[pl.ds(r0, qn), :].astype(F32)).astype(y_ref.dtype)
        return carry

    lax.fori_loop(0, n_chunks, bwd_body, 0)
    if has_final:
        sfin_ref[1] = s_scr[...]


def _ret_scan(log_g, q, k, v, sg, s0, *, row0, n_seq, seq_len, want_final):
    assert row0 % seq_len == 0
    b0 = row0 // seq_len
    has_init = s0 is not None
    in_specs = [
        pl.BlockSpec(memory_space=pltpu.SMEM),
        pl.BlockSpec((seq_len, DK_RET), lambda b, h: (b0 + b, h)),
        pl.BlockSpec((seq_len, DK_RET), lambda b, h: (b0 + b, h)),
        pl.BlockSpec((seq_len, DV_RET), lambda b, h: (b0 + b, h)),
        pl.BlockSpec((seq_len, DV_RET), lambda b, h: (b0 + b, h)),
    ]
    args = [log_g, q, k, v, sg]
    if has_init:
        in_specs.append(pl.BlockSpec((None, 2, None, DK_RET, DV_RET), lambda b, h: (b, 0, h, 0, 0)))
        args.append(s0)
    out_shape = [jax.ShapeDtypeStruct((n_seq * seq_len, H_RET * DV_RET), BF16)]
    out_specs = [pl.BlockSpec((seq_len, DV_RET), lambda b, h: (b, h))]
    if want_final:
        out_shape.append(jax.ShapeDtypeStruct((n_seq, 2, H_RET, DK_RET, DV_RET), F32))
        out_specs.append(pl.BlockSpec((None, 2, None, DK_RET, DV_RET), lambda b, h: (b, 0, h, 0, 0)))
    kern = functools.partial(_ret_kernel, n_chunks=seq_len // CHUNK, has_init=has_init, has_final=want_final)
    return pl.pallas_call(
        kern,
        grid=(n_seq, H_RET),
        in_specs=in_specs,
        out_specs=out_specs,
        out_shape=out_shape,
        scratch_shapes=[pltpu.VMEM((seq_len, DV_RET), F32), pltpu.VMEM((DK_RET, DV_RET), F32)],
        compiler_params=_cparams(("arbitrary", "arbitrary")),
        name="ret_scan_ctx" if want_final else "ret_scan_dec",
    )(*args)


def _out_a_kernel(x_ref, gate_ref, y_ref, u_ref, v_ref, wsp_ref, bsp_ref, w1_ref, w2_ref, o_ref, sgu_scr):
    bm = x_ref.shape[0]
    for ci in range(bm // CHUNK):
        rows = slice(ci * CHUNK, (ci + 1) * CHUNK)
        for g in range(SG_GROUPS):
            cols = slice(g * LANES, (g + 1) * LANES)
            mix = jnp.dot(wsp_ref[g], v_ref[rows, cols], preferred_element_type=F32) + bsp_ref[:, cols]
            sgu_scr[rows, cols] = (u_ref[rows, cols].astype(F32) * mix).astype(BF16)
    out = jnp.dot(y_ref[...], w1_ref[...], preferred_element_type=F32)
    out = out + jnp.dot(sgu_scr[...], w2_ref[...], preferred_element_type=F32)
    o_ref[...] = x_ref[...] + gate_ref[...] * out


def _out_proj_a(x, gate, y, u, v, w_sp, b_full, w1, w2, tok, *, bm):
    t, d = x.shape
    group = tok.group_of_tile(bm)
    return pl.pallas_call(
        _out_a_kernel,
        grid=(t // bm,),
        in_specs=[
            pl.BlockSpec((bm, d), lambda i: (i, 0)),
            pl.BlockSpec((None, 1, d), lambda i: (group(i), 0, 0)),
            pl.BlockSpec((bm, d), lambda i: (i, 0)),
            pl.BlockSpec((bm, d), lambda i: (i, 0)),
            pl.BlockSpec((bm, d), lambda i: (i, 0)),
            pl.BlockSpec((SG_GROUPS, CHUNK, CHUNK), lambda i: (0, 0, 0)),
            pl.BlockSpec((CHUNK, d), lambda i: (0, 0)),
            pl.BlockSpec((d, d), lambda i: (0, 0)),
            pl.BlockSpec((d, d), lambda i: (0, 0)),
        ],
        out_specs=pl.BlockSpec((bm, d), lambda i: (i, 0)),
        out_shape=jax.ShapeDtypeStruct((t, d), F32),
        scratch_shapes=[pltpu.VMEM((bm, d), BF16)],
        compiler_params=_cparams(("arbitrary",)),
        name="out_proj_a",
    )(x, gate, y, u, v, w_sp, b_full, w1, w2)


def _out_c_kernel(x_ref, gate_ref, y_ref, w_ref, o_ref):
    out = jnp.dot(y_ref[...], w_ref[...], preferred_element_type=F32)
    o_ref[...] = x_ref[...] + gate_ref[...] * out


def _out_proj_c(x, gate, y, w, tok, *, bm):
    t, d = x.shape
    kdim = y.shape[1]
    group = tok.group_of_tile(bm)
    return pl.pallas_call(
        _out_c_kernel,
        grid=(t // bm,),
        in_specs=[
            pl.BlockSpec((bm, d), lambda i: (i, 0)),
            pl.BlockSpec((None, 1, d), lambda i: (group(i), 0, 0)),
            pl.BlockSpec((bm, kdim), lambda i: (i, 0)),
            pl.BlockSpec((kdim, d), lambda i: (0, 0)),
        ],
        out_specs=pl.BlockSpec((bm, d), lambda i: (i, 0)),
        out_shape=jax.ShapeDtypeStruct((t, d), F32),
        compiler_params=_cparams(("arbitrary",)),
        name="out_proj_c",
    )(x, gate, y, w)


def _split_kernel(w_ref, g_ref, l_ref):
    w = w_ref[...]
    k, n2 = w.shape
    half = LANES // 2
    lane = lax.broadcasted_iota(jnp.int32, (k, LANES), 1)
    first = lane < half
    idx = jnp.where(first, 2 * lane, 2 * (lane - half) + 1)
    gs, ls = [], []
    for j in range(n2 // (2 * LANES)):
        a = jnp.take_along_axis(w[:, (2 * j) * LANES:(2 * j + 1) * LANES], idx, axis=1)
        b = jnp.take_along_axis(w[:, (2 * j + 1) * LANES:(2 * j + 2) * LANES], idx, axis=1)
        gs.append(jnp.where(first, a, pltpu.roll(b, half, axis=1)))
        ls.append(jnp.where(first, pltpu.roll(a, half, axis=1), b))
    g_ref[...] = jnp.concatenate(gs, axis=1).astype(BF16)
    l_ref[...] = jnp.concatenate(ls, axis=1).astype(BF16)


def _split_gate_lin(w_gu):
    dl, e, k, n2 = w_gu.shape
    tn = 256
    spec_out = pl.BlockSpec((None, None, k, tn), lambda a, b, j: (a, b, 0, j))
    return pl.pallas_call(
        _split_kernel,
        grid=(dl, e, n2 // (2 * tn)),
        in_specs=[pl.BlockSpec((None, None, k, 2 * tn), lambda a, b, j: (a, b, 0, j))],
        out_specs=[spec_out, spec_out],
        out_shape=[jax.ShapeDtypeStruct((dl, e, k, n2 // 2), BF16)] * 2,
        compiler_params=_cparams(("arbitrary",) * 3),
        name="split_gate_lin",
    )(w_gu)


def _router_kernel(x_ref, gam_ref, sc_ref, sh_ref, wr_ref, br_ref, h_ref, e_ref, r_ref, g_ref, cnt_ref, cnt_scr):
    i = pl.program_id(0)
    bm = x_ref.shape[0]

    @pl.when(i == 0)
    def _():
        cnt_scr[...] = jnp.zeros_like(cnt_scr)

    h = _modulated_norm(x_ref[...], gam_ref[...], sc_ref[...], sh_ref[...])
    h_ref[...] = h
    logits = jnp.dot(h, wr_ref[...], precision=HI, preferred_element_type=F32) + br_ref[...]
    lane = lax.broadcasted_iota(jnp.int32, logits.shape, 1).astype(F32)
    vals, idxs = [], []
    work = logits
    for _ in range(TOP_K):
        m = jnp.max(work, axis=-1, keepdims=True)
        idx = jnp.min(jnp.where(work == m, lane, float(LANES)), axis=-1, keepdims=True)
        vals.append(m)
        idxs.append(idx)
        work = jnp.where(lane == idx, -jnp.inf, work)
    exps = [jnp.exp(v - vals[0]) for v in vals]
    inv = 1.0 / functools.reduce(lambda a, b: a + b, exps)
    hot = functools.reduce(jnp.logical_or, [lane == idx for idx in idxs])
    hot_f = hot.astype(F32)
    ri = lax.broadcasted_iota(jnp.int32, (bm, bm), 0)
    ci = lax.broadcasted_iota(jnp.int32, (bm, bm), 1)
    before = (ci < ri).astype(BF16)
    rank_all = cnt_scr[...] + jnp.dot(before, hot_f.astype(BF16), preferred_element_type=F32)
    e_out = jnp.zeros(logits.shape, F32)
    r_out = jnp.zeros(logits.shape, F32)
    g_out = jnp.zeros(logits.shape, F32)
    for k in range(TOP_K):
        rk = jnp.sum(jnp.where(lane == idxs[k], rank_all, 0.0), axis=-1, keepdims=True)
        e_out = jnp.where(lane == float(k), idxs[k], e_out)
        r_out = jnp.where(lane == float(k), rk, r_out)
        g_out = jnp.where(lane == float(k), exps[k] * inv, g_out)
    e_ref[...] = e_out.astype(jnp.int32)
    r_ref[...] = r_out.astype(jnp.int32)
    g_ref[...] = g_out
    cnt_scr[...] = cnt_scr[...] + jnp.sum(hot_f, axis=0, keepdims=True)
    cnt_ref[...] = cnt_scr[...]


def _router(x, gam, sc, sh, w_r, b_r, tok, *, bm):
    t, d = x.shape
    group = tok.group_of_tile(bm)
    row_spec = pl.BlockSpec((bm, LANES), lambda i: (i, 0))
    return pl.pallas_call(
        _router_kernel,
        grid=(t // bm,),
        in_specs=[
            pl.BlockSpec((bm, d), lambda i: (i, 0)),
            pl.BlockSpec((1, d), lambda i: (0, 0)),
            pl.BlockSpec((None, 1, d), lambda i: (group(i), 0, 0)),
            pl.BlockSpec((None, 1, d), lambda i: (group(i), 0, 0)),
            pl.BlockSpec((d, LANES), lambda i: (0, 0)),
            pl.BlockSpec((1, LANES), lambda i: (0, 0)),
        ],
        out_specs=[pl.BlockSpec((bm, d), lambda i: (i, 0)), row_spec, row_spec, row_spec,
                   pl.BlockSpec((1, LANES), lambda i: (0, 0))],
        out_shape=[jax.ShapeDtypeStruct((t, d), F32), jax.ShapeDtypeStruct((t, LANES), jnp.int32),
                   jax.ShapeDtypeStruct((t, LANES), jnp.int32), jax.ShapeDtypeStruct((t, LANES), F32),
                   jax.ShapeDtypeStruct((1, LANES), F32)],
        scratch_shapes=[pltpu.VMEM((1, LANES), F32)],
        compiler_params=_cparams(("arbitrary",)),
        name="moe_router",
    )(x, gam, sc, sh, w_r, b_r)


SLABS = D_MODEL // LANES


def _to_row_tiles(ref, x):
    for s in range(SLABS):
        ref[:, s, :] = x[:, s * LANES:(s + 1) * LANES]


def _from_row_tiles(ref):
    return jnp.concatenate([ref[:, s, :] for s in range(SLABS)], axis=1)


def _dispatch_kernel(pe_ref, dest_ref, h_ref, xs_hbm, stage, zbuf, sem, zsem):
    bm = h_ref.shape[0]

    @pl.when(pl.program_id(0) == 0)
    def _():
        zbuf[...] = jnp.zeros_like(zbuf)

        def tail_copy(e):
            return pltpu.make_async_copy(zbuf, xs_hbm.at[pl.ds(pe_ref[e + 1] - MOE_ROWS, MOE_ROWS)], zsem)

        for e in range(N_EXPERTS):
            @pl.when(pe_ref[e + 1] > pe_ref[e])
            def _():
                tail_copy(e).start()
        for e in range(N_EXPERTS):
            @pl.when(pe_ref[e + 1] > pe_ref[e])
            def _():
                tail_copy(e).wait()

        def spare_copy(b):
            return pltpu.make_async_copy(zbuf, xs_hbm.at[pl.ds(pl.multiple_of(b * MOE_ROWS, MOE_ROWS), MOE_ROWS)], zsem)

        first_spare = pe_ref[N_EXPERTS] // MOE_ROWS
        n_blk = xs_hbm.shape[0] // MOE_ROWS
        lax.fori_loop(first_spare, n_blk, lambda b, c: (spare_copy(b).start(), c)[1], 0)
        lax.fori_loop(first_spare, n_blk, lambda b, c: (spare_copy(b).wait(), c)[1], 0)

    _to_row_tiles(stage, h_ref[...])

    def issue(r, carry):
        for k in range(TOP_K):
            pltpu.make_async_copy(stage.at[r], xs_hbm.at[dest_ref[r * TOP_K + k]], sem).start(priority=k % 2)
        return carry

    lax.fori_loop(0, bm, issue, 0, unroll=2)
    for k in range(TOP_K):
        pltpu.make_async_copy(stage, xs_hbm.at[pl.ds(0, bm)], sem).wait()


def _dispatch(pad_bounds, dest, h, n_rows, *, bm):
    t, d = h.shape
    n_tiles = t // bm
    grid_spec = pltpu.PrefetchScalarGridSpec(
        num_scalar_prefetch=1,
        grid=(n_tiles,),
        in_specs=[
            pl.BlockSpec((None, None, bm * TOP_K), lambda i, pe: (i, 0, 0), memory_space=pltpu.SMEM),
            pl.BlockSpec((bm, d), lambda i, pe: (i, 0)),
        ],
        out_specs=pl.BlockSpec(memory_space=pl.ANY),
        scratch_shapes=[pltpu.VMEM((bm, SLABS, LANES), F32), pltpu.VMEM((MOE_ROWS, SLABS, LANES), F32),
                        pltpu.SemaphoreType.DMA, pltpu.SemaphoreType.DMA],
    )
    return pl.pallas_call(
        _dispatch_kernel,
        grid_spec=grid_spec,
        out_shape=jax.ShapeDtypeStruct((n_rows, SLABS, LANES), F32),
        compiler_params=_cparams(("arbitrary",)),
        name="moe_dispatch",
    )(pad_bounds, dest.reshape(n_tiles, 1, bm * TOP_K), h)


def _expert_kernel(be_ref, na_ref, x_ref, wg_ref, wl_ref, bg_ref, bl_ref, wd_ref, bd_ref, o_ref):
    @pl.when(pl.program_id(0) < na_ref[0])
    def _():
        x = _from_row_tiles(x_ref).astype(BF16)
        hg = jnp.dot(x, wg_ref[...], preferred_element_type=F32) + bg_ref[...]
        hl = jnp.dot(x, wl_ref[...], preferred_element_type=F32) + bl_ref[...]
        glu = jnp.minimum(hg, SWIGLU_LIMIT)
        lin = jnp.clip(hl, -SWIGLU_LIMIT, SWIGLU_LIMIT)
        act = glu * (1.0 / (1.0 + jnp.exp(-SWIGLU_ALPHA * glu))) * (lin + 1.0)
        y = jnp.dot(act.astype(BF16), wd_ref[...], preferred_element_type=F32) + bd_ref[...]
        _to_row_tiles(o_ref, y)

    @pl.when(pl.program_id(0) >= na_ref[0])
    def _():
        o_ref[...] = jnp.zeros_like(o_ref)


def _experts(blk_expert, n_active, xs, wg, wl, bg, bl, wd, bd, layer):
    n_rows = xs.shape[0]
    d, dff = wg.shape[2], wg.shape[3]
    n_blk = n_rows // MOE_ROWS
    wmap = lambda i, be, na: (layer, be[i], 0, 0)
    rmap = lambda i, be, na: (jnp.minimum(i, na[0] - 1), 0, 0)
    grid_spec = pltpu.PrefetchScalarGridSpec(
        num_scalar_prefetch=2,
        grid=(n_blk,),
        in_specs=[
            pl.BlockSpec((MOE_ROWS, SLABS, LANES), rmap),
            pl.BlockSpec((None, None, d, dff), wmap),
            pl.BlockSpec((None, None, d, dff), wmap),
            pl.BlockSpec((None, None, 1, dff), wmap),
            pl.BlockSpec((None, None, 1, dff), wmap),
            pl.BlockSpec((None, None, dff, d), wmap),
            pl.BlockSpec((None, None, 1, d), wmap),
        ],
        out_specs=pl.BlockSpec((MOE_ROWS, SLABS, LANES), lambda i, be, na: (i, 0, 0)),
    )
    return pl.pallas_call(
        _expert_kernel,
        grid_spec=grid_spec,
        out_shape=jax.ShapeDtypeStruct((n_rows, SLABS, LANES), F32),
        compiler_params=_cparams(("arbitrary",)),
        name="moe_experts",
    )(blk_expert, n_active, xs, wg, wl, bg, bl, wd, bd)


def _combine_kernel(dest_ref, g_ref, x_ref, gate_ref, gam_ref, ys_hbm, o_ref, *rest, final_norm, n_ctx_tiles):
    if final_norm:
        of_ref, buf, acc_scr, sem = rest
    else:
        buf, acc_scr, sem = rest
    bm = x_ref.shape[0]

    def issue(r, carry):
        for k in range(TOP_K):
            pltpu.make_async_copy(ys_hbm.at[dest_ref[r * TOP_K + k]], buf.at[k, r], sem).start(priority=k % 2)
        return carry

    lax.fori_loop(0, bm, issue, 0, unroll=2)
    for k in range(TOP_K):
        pltpu.make_async_copy(ys_hbm.at[pl.ds(0, bm)], buf.at[k], sem).wait()

    def mix(r, carry):
        acc = g_ref[r * TOP_K] * buf[0, r]
        for k in range(1, TOP_K):
            acc = acc + g_ref[r * TOP_K + k] * buf[k, r]
        acc_scr[r] = acc
        return carry

    lax.fori_loop(0, bm, mix, 0, unroll=8)
    xn = x_ref[...] + gate_ref[...] * _from_row_tiles(acc_scr)
    if not final_norm:
        o_ref[...] = xn
    else:
        ms = jnp.mean(xn * xn, axis=-1, keepdims=True)
        yn = xn * lax.rsqrt(ms + EPS) * gam_ref[...]
        is_ctx = pl.program_id(0) < n_ctx_tiles

        @pl.when(is_ctx)
        def _():
            o_ref[...] = yn

        @pl.when(jnp.logical_not(is_ctx))
        def _():
            of_ref[...] = yn


def _combine(dest, x, gate, gates, gam_final, ys, tok, *, bm, final_norm):
    t, d = x.shape
    group = tok.group_of_tile(bm)
    n_tiles = t // bm
    n_ctx_tiles = tok.t_ctx // bm
    if final_norm:
        out_shape = [jax.ShapeDtypeStruct((tok.t_ctx, d), F32), jax.ShapeDtypeStruct((tok.t_dec, d), F32)]
        out_specs = [pl.BlockSpec((bm, d), lambda i: (jnp.minimum(i, n_ctx_tiles - 1), 0)),
                     pl.BlockSpec((bm, d), lambda i: (jnp.maximum(i - n_ctx_tiles, 0), 0))]
    else:
        out_shape = [jax.ShapeDtypeStruct((t, d), F32)]
        out_specs = [pl.BlockSpec((bm, d), lambda i: (i, 0))]
    return pl.pallas_call(
        functools.partial(_combine_kernel, final_norm=final_norm, n_ctx_tiles=n_ctx_tiles),
        grid=(n_tiles,),
        in_specs=[
            pl.BlockSpec((None, None, bm * TOP_K), lambda i: (i, 0, 0), memory_space=pltpu.SMEM),
            pl.BlockSpec((None, None, bm * TOP_K), lambda i: (i, 0, 0), memory_space=pltpu.SMEM),
            pl.BlockSpec((bm, d), lambda i: (i, 0)),
            pl.BlockSpec((None, 1, d), lambda i: (group(i), 0, 0)),
            pl.BlockSpec((1, d), lambda i: (0, 0)),
            pl.BlockSpec(memory_space=pl.ANY),
        ],
        out_specs=out_specs,
        out_shape=out_shape,
        scratch_shapes=[pltpu.VMEM((TOP_K, bm, SLABS, LANES), F32), pltpu.VMEM((bm, SLABS, LANES), F32),
                        pltpu.SemaphoreType.DMA],
        compiler_params=_cparams(("arbitrary",)),
        name="moe_combine_final" if final_norm else "moe_combine",
    )(dest.reshape(n_tiles, 1, bm * TOP_K), gates[:, :TOP_K].reshape(n_tiles, 1, bm * TOP_K), x, gate, gam_final, ys)


def _moe(x, gam, sc, sh, gate, w_r, b_r, wg, wl, bg, bl, wd, bd, layer, gam_final, tok, *, final_norm):
    t, d = x.shape
    h, e_out, r_out, gates, counts = _router(x, gam, sc, sh, w_r, b_r, tok, bm=512)
    counts = counts[0, :N_EXPERTS].astype(jnp.int32)
    padded = (counts + MOE_ROWS - 1) // MOE_ROWS * MOE_ROWS
    pad_end = jnp.cumsum(padded)
    pad_start = pad_end - padded
    e_sel = e_out[:, :TOP_K]
    dest = pad_start[e_sel] + r_out[:, :TOP_K]
    n_rows = t * TOP_K + N_EXPERTS * MOE_ROWS
    n_blk = n_rows // MOE_ROWS
    blk_start = jnp.arange(n_blk, dtype=jnp.int32) * MOE_ROWS
    blk_expert = jnp.minimum(jnp.sum((pad_end[None, :] <= blk_start[:, None]).astype(jnp.int32), axis=1),
                             N_EXPERTS - 1)
    n_active = (pad_end[-1:] // MOE_ROWS).astype(jnp.int32)
    pad_bounds = jnp.concatenate([jnp.zeros((1,), jnp.int32), pad_end.astype(jnp.int32)])
    xs = _dispatch(pad_bounds, dest, h, n_rows, bm=256)
    ys = _experts(blk_expert, n_active, xs, wg, wl, bg, bl, wd, bd, layer)
    return _combine(dest, x, gate, gates, gam_final, ys, tok, bm=256, final_norm=final_norm)


def _rope_tables(tok):
    rows = tok.dec_len // GRID_W
    r = jnp.repeat(jnp.arange(rows), GRID_W).astype(F32)
    cidx = jnp.tile(jnp.arange(GRID_W), rows).astype(F32)
    n_freq = DK_RET // 4
    inv = ROPE_BASE ** (-jnp.arange(n_freq, dtype=F32) / n_freq)
    ang = jnp.concatenate([r[:, None] * inv, cidx[:, None] * inv], axis=-1)
    cos = jnp.concatenate([jnp.ones((tok.t_ctx, DK_RET // 2), F32), jnp.tile(jnp.cos(ang), (tok.n_dec_seq, 1))])
    sin = jnp.concatenate([jnp.zeros((tok.t_ctx, DK_RET // 2), F32), jnp.tile(jnp.sin(ang), (tok.n_dec_seq, 1))])
    return cos, sin


def _group_cols(p):
    h = p.shape[1]
    a = p.reshape(2, SSD_GROUPS, h // SSD_GROUPS).transpose(1, 0, 2).reshape(SSD_GROUPS, -1)
    return jnp.pad(a, ((0, 0), (0, LANES - a.shape[1])))[:, None, :]


def kernel(x_prompt, x_sample, state_ssd, state_ret, c, c_ctx, w_mod, b_mod, norm_mix, norm_ffn, w_in_a, conv_w, conv_b, dt_bias, a_log, d_skip, ssd_norm, w_sp, b_sp, w_out_a, w_in_c, decay_logit, w_out_c, w_router, b_router, w_gu, b_gu, w_down, b_down, norm_final):
    n_ctx, ctx_len, d = x_prompt.shape
    n_dec, dec_len, _ = x_sample.shape
    tok = _Tokens(n_ctx, ctx_len, n_dec, dec_len)
    depth = w_mod.shape[0]
    x = jnp.concatenate([x_prompt.reshape(tok.t_ctx, d), x_sample.reshape(tok.t_dec, d)])

    cvecs = jnp.concatenate([c_ctx[None], c, jnp.zeros((MOD_ROWS - 1 - n_dec, d), F32)])
    mod = _modulation(cvecs, w_mod, b_mod)
    mod = mod.reshape(depth, MOD_ROWS, 6, 1, d).transpose(0, 2, 1, 3, 4)

    h_ssd = a_log.shape[2]
    xbc_w = d + 2 * SSD_GROUPS * D_STATE
    o1, o2, o3 = d, d + xbc_w, d + xbc_w + 2 * h_ssd
    cos, sin = _rope_tables(tok)
    wg_all, wl_all = _split_gate_lin(w_gu)
    bg_all = b_gu[:, :, None, 0::2]
    bl_all = b_gu[:, :, None, 1::2]
    wd_all = w_down.astype(BF16)
    bd_all = b_down[:, :, None, :]
    new_ssd, new_ret = [], []
    y_final = None
    bm_proj = 512

    for l in range(depth):
        sh1, sc1, g1, sh2, sc2, g2 = (mod[l, j] for j in range(6))
        gam_mix = norm_mix[l][None]
        i = l // 2
        if l % 2 == 0:
            w_in = w_in_a[i]
            w_z = w_in[:, :o1].astype(BF16)
            w_xbc = w_in[:, o1:o2].astype(BF16)
            w_dt = w_in[:, o2:o3].reshape(d, 2, SSD_GROUPS, h_ssd // SSD_GROUPS).transpose(0, 2, 1, 3)
            w_dt = jnp.pad(w_dt.reshape(d, SSD_GROUPS, -1), ((0, 0), (0, 0), (0, LANES - 2 * h_ssd // SSD_GROUPS)))
            w_dt = w_dt.reshape(d, SSD_GROUPS * LANES).astype(BF16)
            w_uv = w_in[:, o3:].astype(BF16)
            xbc = _norm_proj(x, gam_mix, sc1, sh1, w_xbc, tok=tok, bm=dec_len, tn=256, out_dtype=BF16,
                             epilogue=_ep_conv_silu, extra=(conv_w[i], conv_b[i][None]),
                             extra_specs=(pl.BlockSpec((4, 256), lambda r, j: (0, j)),
                                          pl.BlockSpec((1, 256), lambda r, j: (0, j))), name="proj_xbc")
            sz, dt_raw, u, v = _norm_proj_multi(
                x, gam_mix, sc1, sh1, jnp.concatenate([w_z, w_dt, w_uv], axis=1), tok,
                [(d, 512, BF16, _ep_silu, False), (SSD_GROUPS * LANES, SSD_GROUPS * LANES, F32, _ep_plain, False),
                 (d, 512, BF16, _ep_gelu, False), (d, d, BF16, _ep_gelu_ln, False)],
                bm=bm_proj, name="proj_zdtuv")
            dtb = _group_cols(dt_bias[i])
            a_neg = _group_cols(-jnp.exp(a_log[i]))
            dsk = jnp.repeat(d_skip[i], SSD_HEAD).reshape(SSD_GROUPS, 1, GROUP_W)
            nrm = ssd_norm[i].reshape(SSD_GROUPS, 1, GROUP_W)
            s0 = state_ssd[:, i].reshape(n_dec, 2, SSD_GROUPS, HEADS_PER_GROUP, D_STATE, SSD_HEAD)
            s0 = s0.transpose(0, 2, 1, 4, 3, 5).reshape(n_dec, SSD_GROUPS, 2, D_STATE, GROUP_W)
            y_ctx, s_fin = _ssd_scan(xbc, dt_raw, sz, dtb, a_neg, dsk, nrm, None, row0=0, n_seq=n_ctx,
                                     seq_len=ctx_len, want_final=True)
            (y_dec,) = _ssd_scan(xbc, dt_raw, sz, dtb, a_neg, dsk, nrm, s0, row0=tok.t_ctx, n_seq=n_dec,
                                 seq_len=dec_len, want_final=False)
            s_fin = s_fin.reshape(n_ctx, SSD_GROUPS, 2, D_STATE, HEADS_PER_GROUP, SSD_HEAD)
            new_ssd.append(s_fin.transpose(0, 2, 1, 4, 3, 5).reshape(n_ctx, 2, h_ssd, D_STATE, SSD_HEAD))
            y = jnp.concatenate([y_ctx, y_dec])
            b_full = jnp.repeat(b_sp[i].T, LANES, axis=1)
            w_o = w_out_a[i].astype(BF16)
            x = _out_proj_a(x, g1, y, u, v, w_sp[i].astype(BF16), b_full, w_o[:d], w_o[d:], tok, bm=512)
        else:
            hk = H_RET * DK_RET
            hv = H_RET * DV_RET
            w_in = w_in_c[i].astype(BF16)
            rope_specs = (pl.BlockSpec((bm_proj, DK_RET // 2), lambda r: (r, 0)),) * 2
            q, kk, vv, sg = _norm_proj_multi(
                x, gam_mix, sc1, sh1, w_in, tok,
                [(hk, 512, BF16, functools.partial(_ep_rope, scale=1.0), True),
                 (hk, 512, BF16, functools.partial(_ep_rope, scale=DK_RET ** -0.5), True),
                 (hv, 512, BF16, _ep_plain, False), (hv, 512, BF16, _ep_silu, False)],
                bm=bm_proj, extra=(cos, sin), extra_specs=rope_specs, name="proj_qkvg")
            log_g = jax.nn.log_sigmoid(decay_logit[i].astype(F32))
            y_ctx, s_fin = _ret_scan(log_g, q, kk, vv, sg, None, row0=0, n_seq=n_ctx, seq_len=ctx_len, want_final=True)
            (y_dec,) = _ret_scan(log_g, q, kk, vv, sg, state_ret[:, i], row0=tok.t_ctx, n_seq=n_dec, seq_len=dec_len,
                                 want_final=False)
            new_ret.append(s_fin)
            y = jnp.concatenate([y_ctx, y_dec])
            x = _out_proj_c(x, g1, y, w_out_c[i].astype(BF16), tok, bm=512)

        w_r = jnp.pad(w_router[l], ((0, 0), (0, LANES - N_EXPERTS)))
        b_r = jnp.pad(b_router[l], (0, LANES - N_EXPERTS), constant_values=-1e30)[None]
        last = l == depth - 1
        res = _moe(x, norm_ffn[l][None], sc2, sh2, g2, w_r, b_r, wg_all, wl_all, bg_all, bl_all, wd_all, bd_all, l,
                   norm_final[None], tok, final_norm=last)
        if last:
            y_final = res
        else:
            x = res[0]

    y_prompt = y_final[0].reshape(n_ctx, ctx_len, d)
    y_sample = y_final[1].reshape(n_dec, dec_len, d)
    return (y_prompt, y_sample, jnp.stack(new_ssd, axis=1), jnp.stack(new_ret, axis=1))
```

```python
import functools
import math

import jax
import jax.numpy as jnp
from jax import lax
from jax.experimental import pallas as pl
from jax.experimental.pallas import tpu as pltpu

D_MODEL = 1024
GRID_W = 64
CHUNK = 128
SSD_HEAD = 64
SSD_GROUPS = 2
D_STATE = 128
GROUP_W = D_MODEL // SSD_GROUPS
HEADS_PER_GROUP = GROUP_W // SSD_HEAD
SG_GROUPS = 8
H_RET = 4
DK_RET = D_MODEL // H_RET
DV_RET = 2 * DK_RET
ROPE_BASE = 10000.0
N_EXPERTS = 32
TOP_K = 4
SWIGLU_LIMIT = 7.0
SWIGLU_ALPHA = 1.702
EPS = 1e-6

LANES = 128
MOD_ROWS = 8
MOE_ROWS = 256
VMEM_LIMIT = 56 * 1024 * 1024

F32 = jnp.float32
BF16 = jnp.bfloat16
HI = lax.Precision.HIGHEST


def _cparams(sem):
    return pltpu.CompilerParams(dimension_semantics=sem, vmem_limit_bytes=VMEM_LIMIT)


def _silu(x):
    return x * (1.0 / (1.0 + jnp.exp(-x)))


def _gelu_tanh(x):
    return 0.5 * x * (1.0 + jnp.tanh(math.sqrt(2.0 / math.pi) * (x + 0.044715 * (x * x * x))))


def _softplus(x):
    return jnp.maximum(x, 0.0) + jnp.log(1.0 + jnp.exp(-jnp.abs(x)))


def _mod_kernel(c_ref, w_ref, b_ref, o_ref):
    a = _silu(c_ref[...])
    o_ref[...] = jnp.dot(a, w_ref[...], precision=HI, preferred_element_type=F32) + b_ref[...]


def _modulation(cvecs, w_mod, b_mod):
    depth, d, n = w_mod.shape
    tn = 1536
    return pl.pallas_call(
        _mod_kernel,
        grid=(depth, n // tn),
        in_specs=[
            pl.BlockSpec((MOD_ROWS, d), lambda l, j: (0, 0)),
            pl.BlockSpec((None, d, tn), lambda l, j: (l, 0, j)),
            pl.BlockSpec((None, 1, tn), lambda l, j: (l, 0, j)),
        ],
        out_specs=pl.BlockSpec((None, MOD_ROWS, tn), lambda l, j: (l, 0, j)),
        out_shape=jax.ShapeDtypeStruct((depth, MOD_ROWS, n), F32),
        compiler_params=_cparams(("arbitrary", "arbitrary")),
        name="modulation",
    )(cvecs, w_mod, b_mod.reshape(depth, 1, n))


class _Tokens:
    def __init__(self, n_ctx_seq, ctx_len, n_dec_seq, dec_len):
        self.n_ctx_seq, self.ctx_len = n_ctx_seq, ctx_len
        self.n_dec_seq, self.dec_len = n_dec_seq, dec_len
        self.t_ctx = n_ctx_seq * ctx_len
        self.t_dec = n_dec_seq * dec_len
        self.total = self.t_ctx + self.t_dec

    def group_of_tile(self, bm):
        assert self.t_ctx % bm == 0 and self.dec_len % bm == 0
        n_ctx_tiles = self.t_ctx // bm
        per_seq = self.dec_len // bm

        def group(i):
            return jnp.where(i < n_ctx_tiles, 0, 1 + (i - n_ctx_tiles) // per_seq)

        return group


def _modulated_norm(x, gam, sc, sh):
    ms = jnp.mean(x * x, axis=-1, keepdims=True)
    return (x * lax.rsqrt(ms + EPS) * gam) * (1.0 + sc) + sh


def _proj_kernel(*refs, epilogue, n_extra, tok, bm):
    x_ref, gam_ref, sc_ref, sh_ref, w_ref = refs[:5]
    extra = refs[5:5 + n_extra]
    o_ref = refs[5 + n_extra]
    h_scr = refs[6 + n_extra]
    i = pl.program_id(0)

    @pl.when(pl.program_id(1) == 0)
    def _():
        h_scr[...] = _modulated_norm(x_ref[...], gam_ref[...], sc_ref[...], sh_ref[...]).astype(BF16)

    acc = jnp.dot(h_scr[...], w_ref[...], preferred_element_type=F32)
    o_ref[...] = epilogue(acc, i, tok, bm, *extra).astype(o_ref.dtype)


def _ep_plain(acc, i, tok, bm):
    return acc


def _ep_silu(acc, i, tok, bm):
    return _silu(acc)


def _ep_gelu(acc, i, tok, bm):
    return _gelu_tanh(acc)


def _ep_softplus_bias(acc, i, tok, bm, bias_ref):
    return _softplus(acc + bias_ref[...])


def _ep_gelu_ln(acc, i, tok, bm):
    g = _gelu_tanh(acc)
    mu = jnp.mean(g, axis=-1, keepdims=True)
    gc = g - mu
    return gc * lax.rsqrt(jnp.mean(gc * gc, axis=-1, keepdims=True) + 1e-5)


def _ep_conv_silu(acc, i, tok, bm, cw_ref, cb_ref):
    n = acc.shape[0]
    seq = jnp.where(i * bm < tok.t_ctx, tok.ctx_len, tok.dec_len)
    t = lax.broadcasted_iota(jnp.int32, (n, 1), 0) & (seq - 1)
    cw = cw_ref[...]
    y = acc * cw[2:3, :] + cb_ref[...]
    y = y + jnp.where(t >= 2, pltpu.roll(acc, 2, axis=0), 0.0) * cw[0:1, :]
    y = y + jnp.where(t >= 1, pltpu.roll(acc, 1, axis=0), 0.0) * cw[1:2, :]
    y = y + jnp.where(t < seq - 1, pltpu.roll(acc, n - 1, axis=0), 0.0) * cw[3:4, :]
    return _silu(y)


def _ep_rope(acc, i, tok, bm, cos_ref, sin_ref, *, scale):
    cs, sn = cos_ref[...], sin_ref[...]
    half = DK_RET // 2
    outs = []
    for h in range(acc.shape[1] // DK_RET):
        x1 = acc[:, h * DK_RET:h * DK_RET + half] * scale
        x2 = acc[:, h * DK_RET + half:(h + 1) * DK_RET] * scale
        outs.append(x1 * cs - x2 * sn)
        outs.append(x2 * cs + x1 * sn)
    return jnp.concatenate(outs, axis=1)


def _norm_proj(x, gam, sc, sh, w, tok, *, bm, tn, out_dtype, epilogue, extra=(), extra_specs=(), name):
    t, d = x.shape
    n = w.shape[1]
    group = tok.group_of_tile(bm)
    kern = functools.partial(_proj_kernel, epilogue=epilogue, n_extra=len(extra), tok=tok, bm=bm)
    return pl.pallas_call(
        kern,
        grid=(t // bm, n // tn),
        in_specs=[
            pl.BlockSpec((bm, d), lambda i, j: (i, 0)),
            pl.BlockSpec((1, d), lambda i, j: (0, 0)),
            pl.BlockSpec((None, 1, d), lambda i, j: (group(i), 0, 0)),
            pl.BlockSpec((None, 1, d), lambda i, j: (group(i), 0, 0)),
            pl.BlockSpec((d, tn), lambda i, j: (0, j)),
            *extra_specs,
        ],
        out_specs=pl.BlockSpec((bm, tn), lambda i, j: (i, j)),
        out_shape=jax.ShapeDtypeStruct((t, n), out_dtype),
        scratch_shapes=[pltpu.VMEM((bm, d), BF16)],
        compiler_params=_cparams(("arbitrary", "arbitrary")),
        name=name,
    )(x, gam, sc, sh, w, *extra)


def _proj_multi_kernel(*refs, pieces, n_extra, tok, bm):
    x_ref, gam_ref, sc_ref, sh_ref, w_ref = refs[:5]
    extra = refs[5:5 + n_extra]
    outs = refs[5 + n_extra:5 + n_extra + len(pieces)]
    h_scr = refs[5 + n_extra + len(pieces)]
    i = pl.program_id(0)
    h_scr[...] = _modulated_norm(x_ref[...], gam_ref[...], sc_ref[...], sh_ref[...]).astype(BF16)
    c0 = 0
    for (width, chunk, _, epilogue, uses_extra), o_ref in zip(pieces, outs):
        for j in range(width // chunk):
            acc = jnp.dot(h_scr[...], w_ref[:, c0 + j * chunk:c0 + (j + 1) * chunk], preferred_element_type=F32)
            res = epilogue(acc, i, tok, bm, *(extra if uses_extra else ()))
            o_ref[:, j * chunk:(j + 1) * chunk] = res.astype(o_ref.dtype)
        c0 += width


def _norm_proj_multi(x, gam, sc, sh, w, tok, pieces, *, bm, extra=(), extra_specs=(), name):
    t, d = x.shape
    n = w.shape[1]
    assert n == sum(p[0] for p in pieces)
    group = tok.group_of_tile(bm)
    kern = functools.partial(_proj_multi_kernel, pieces=tuple(pieces), n_extra=len(extra), tok=tok, bm=bm)
    return pl.pallas_call(
        kern,
        grid=(t // bm,),
        in_specs=[
            pl.BlockSpec((bm, d), lambda i: (i, 0)),
            pl.BlockSpec((1, d), lambda i: (0, 0)),
            pl.BlockSpec((None, 1, d), lambda i: (group(i), 0, 0)),
            pl.BlockSpec((None, 1, d), lambda i: (group(i), 0, 0)),
            pl.BlockSpec((d, n), lambda i: (0, 0), pipeline_mode=pl.Buffered(1)),
            *extra_specs,
        ],
        out_specs=[pl.BlockSpec((bm, p[0]), lambda i: (i, 0)) for p in pieces],
        out_shape=[jax.ShapeDtypeStruct((t, p[0]), p[2]) for p in pieces],
        scratch_shapes=[pltpu.VMEM((bm, d), BF16)],
        compiler_params=_cparams(("arbitrary",)),
        name=name,
    )(x, gam, sc, sh, w, *extra)


def _ssd_kernel(*refs, n_chunks, has_init, has_final):
    xs_ref, b_ref, c_ref, dt_ref, sz_ref, a_ref, dsk_ref, nrm_ref = refs[:8]
    k = 8
    s0_ref = None
    if has_init:
        s0_ref = refs[k]
        k += 1
    y_ref = refs[k]
    k += 1
    sfin_ref = None
    if has_final:
        sfin_ref = refs[k]
        k += 1
    y_scrs = (refs[k], refs[k + 1])
    s_scrs = (refs[k + 2], refs[k + 3])

    q = CHUNK
    row = lax.broadcasted_iota(jnp.int32, (q, q), 0)
    col = lax.broadcasted_iota(jnp.int32, (q, q), 1)
    lane = lax.broadcasted_iota(jnp.int32, (1, q), 1)
    left = lane < SSD_HEAD
    a_neg = a_ref[...]
    n_pairs = HEADS_PER_GROUP // 2
    keeps = (col <= row, col >= row)
    tris = tuple(kp.astype(F32).astype(BF16) for kp in keeps)

    def masked_sums(tri, x):
        hi = x.astype(BF16)
        r1 = x - hi.astype(F32)
        mid = r1.astype(BF16)
        lo = (r1 - mid.astype(F32)).astype(BF16)
        return (jnp.dot(tri, hi, preferred_element_type=F32) + jnp.dot(tri, mid, preferred_element_type=F32)
                + jnp.dot(tri, lo, preferred_element_type=F32))

    def chunk(ci, direction):
        s_scr = s_scrs[direction]
        r0 = pl.multiple_of(ci * q, q)
        xs = xs_ref[pl.ds(r0, q), :]
        bm_ = b_ref[pl.ds(r0, q), :]
        cm = c_ref[pl.ds(r0, q), :]
        dt = dt_ref[pl.ds(r0, q), :]
        la = dt * a_neg
        keep, tri = keeps[direction], tris[direction]
        last = q - 1 if direction == 0 else 0
        cum = masked_sums(tri, la)
        cum_t = cum.T
        dt_t = dt.T
        tot_t = jnp.broadcast_to(cum_t[:, last:last + 1], (q, q))
        w_t = dt_t * jnp.exp(tot_t - cum_t)
        g = lax.dot_general(cm, bm_, (((1,), (1,)), ((), ())), preferred_element_type=F32)
        b_t = bm_.astype(F32).T
        cm_f = cm.astype(F32)
        outs = []
        for p in range(n_pairs):
            xs_p = xs[:, p * LANES:(p + 1) * LANES]
            s_p = s_scr[:, p * LANES:(p + 1) * LANES]
            s_b = s_p.astype(BF16)
            zero = jnp.zeros_like(xs_p)
            zero_s = jnp.zeros_like(s_b)
            lhs, rhs, lhs_s, rhs_s, decs = [], [], [], [], []
            for hh in range(2):
                cidx = direction * HEADS_PER_GROUP + 2 * p + hh
                cum_b = jnp.broadcast_to(cum[:, cidx:cidx + 1], (q, q))
                dec = jnp.exp(jnp.where(keep, cum_b - cum_t[cidx:cidx + 1, :], -jnp.inf))
                scores = g * dec * dt_t[cidx:cidx + 1, :]
                lhs += [scores.astype(BF16), (cm_f * jnp.exp(cum_b)).astype(BF16)]
                sel = left if hh == 0 else jnp.logical_not(left)
                rhs += [jnp.where(sel, xs_p, zero), jnp.where(sel, s_b, zero_s)]
                lhs_s.append((b_t * w_t[cidx:cidx + 1, :]).astype(BF16))
                rhs_s.append(jnp.where(sel, xs_p, zero))
                decs.append(jnp.exp(cum_t[cidx:cidx + 1, last:last + 1]))
            y_p = jnp.dot(jnp.concatenate(lhs, axis=1), jnp.concatenate(rhs, axis=0),
                          preferred_element_type=F32)
            upd = jnp.dot(jnp.concatenate(lhs_s, axis=1), jnp.concatenate(rhs_s, axis=0),
                          preferred_element_type=F32)
            s_scr[:, p * LANES:(p + 1) * LANES] = s_p * jnp.where(left, decs[0], decs[1]) + upd
            outs.append(y_p)
        return jnp.concatenate(outs, axis=1)

    for direction in range(2):
        if has_init:
            s_scrs[direction][...] = s0_ref[direction]
        else:
            s_scrs[direction][...] = jnp.zeros_like(s_scrs[direction])

    def scan_body(j, carry):
        for direction, ci in ((0, j), (1, n_chunks - 1 - j)):
            r0 = pl.multiple_of(ci * q, q)
            y_scrs[direction][pl.ds(r0, q), :] = chunk(ci, direction)
        return carry

    lax.fori_loop(0, n_chunks, scan_body, 0)
    if has_final:
        for direction in range(2):
            sfin_ref[direction] = s_scrs[direction][...]

    def finish_body(ci, carry):
        r0 = pl.multiple_of(ci * q, q)
        y = y_scrs[0][pl.ds(r0, q), :] + y_scrs[1][pl.ds(r0, q), :]
        y = y + dsk_ref[...] * xs_ref[pl.ds(r0, q), :].astype(F32)
        y = y * sz_ref[pl.ds(r0, q), :].astype(F32)
        y = y * lax.rsqrt(jnp.mean(y * y, axis=-1, keepdims=True) + EPS)
        y_ref[pl.ds(r0, q), :] = (y * nrm_ref[...]).astype(y_ref.dtype)
        return carry

    lax.fori_loop(0, n_chunks, finish_body, 0)


def _ssd_scan(xbc, dt, sz, a_neg, dsk, nrm, s0, *, row0, n_seq, seq_len, want_final):
    assert row0 % seq_len == 0
    b0 = row0 // seq_len
    has_init = s0 is not None
    gw = GROUP_W
    b_blk0 = D_MODEL // D_STATE
    c_blk0 = b_blk0 + SSD_GROUPS
    in_specs = [
        pl.BlockSpec((seq_len, gw), lambda b, g: (b0 + b, g)),
        pl.BlockSpec((seq_len, D_STATE), lambda b, g: (b0 + b, b_blk0 + g)),
        pl.BlockSpec((seq_len, D_STATE), lambda b, g: (b0 + b, c_blk0 + g)),
        pl.BlockSpec((seq_len, LANES), lambda b, g: (b0 + b, g)),
        pl.BlockSpec((seq_len, gw), lambda b, g: (b0 + b, g)),
        pl.BlockSpec((None, 1, LANES), lambda b, g: (g, 0, 0)),
        pl.BlockSpec((None, 1, gw), lambda b, g: (g, 0, 0)),
        pl.BlockSpec((None, 1, gw), lambda b, g: (g, 0, 0)),
    ]
    args = [xbc, xbc, xbc, dt, sz, a_neg, dsk, nrm]
    if has_init:
        in_specs.append(pl.BlockSpec((None, None, 2, D_STATE, gw), lambda b, g: (b, g, 0, 0, 0)))
        args.append(s0)
    out_shape = [jax.ShapeDtypeStruct((n_seq * seq_len, D_MODEL), BF16)]
    out_specs = [pl.BlockSpec((seq_len, gw), lambda b, g: (b, g))]
    if want_final:
        out_shape.append(jax.ShapeDtypeStruct((n_seq, SSD_GROUPS, 2, D_STATE, gw), F32))
        out_specs.append(pl.BlockSpec((None, None, 2, D_STATE, gw), lambda b, g: (b, g, 0, 0, 0)))
    kern = functools.partial(_ssd_kernel, n_chunks=seq_len // CHUNK, has_init=has_init, has_final=want_final)
    return pl.pallas_call(
        kern,
        grid=(n_seq, SSD_GROUPS),
        in_specs=in_specs,
        out_specs=out_specs,
        out_shape=out_shape,
        scratch_shapes=[pltpu.VMEM((seq_len, gw), F32), pltpu.VMEM((seq_len, gw), F32),
                        pltpu.VMEM((D_STATE, gw), F32), pltpu.VMEM((D_STATE, gw), F32)],
        compiler_params=_cparams(("arbitrary", "arbitrary")),
        name="ssd_scan_ctx" if want_final else "ssd_scan_dec",
    )(*args)


def _ret_kernel(*refs, n_chunks, has_init, has_final):
    lg_ref, q_ref, k_ref, v_ref, sg_ref = refs[:5]
    k = 5
    s0_ref = None
    if has_init:
        s0_ref = refs[k]
        k += 1
    y_ref = refs[k]
    k += 1
    sfin_ref = None
    if has_final:
        sfin_ref = refs[k]
        k += 1
    y_scrs = (refs[k], refs[k + 1])
    s_scrs = (refs[k + 2], refs[k + 3])

    qn = CHUNK
    h = pl.program_id(1)
    row = lax.broadcasted_iota(jnp.int32, (qn, qn), 0)
    col = lax.broadcasted_iota(jnp.int32, (qn, qn), 1)
    rowk = lax.broadcasted_iota(jnp.int32, (qn, DK_RET), 0).astype(F32)

    def tables(direction):
        lg = lg_ref[direction, h]
        if direction == 0:
            keep = col <= row
            dist = (row - col).astype(F32)
            e_q = jnp.exp(lg * (rowk + 1.0))
            w_k = jnp.exp(lg * (qn - 1.0 - rowk))
        else:
            keep = col >= row
            dist = (col - row).astype(F32)
            e_q = jnp.exp(lg * (qn - rowk))
            w_k = jnp.exp(lg * rowk)
        dmat = jnp.where(keep, jnp.exp(lg * dist), 0.0)
        return dmat, e_q, w_k, jnp.exp(jnp.full((1, 1), float(qn), F32) * lg)

    def chunk(ci, direction, tabs):
        dmat, e_q, w_k, dec = tabs
        s_scr = s_scrs[direction]
        r0 = pl.multiple_of(ci * qn, qn)
        qc = q_ref[pl.ds(r0, qn), :]
        kc = k_ref[pl.ds(r0, qn), :]
        vc = v_ref[pl.ds(r0, qn), :]
        scores = lax.dot_general(qc, kc, (((1,), (1,)), ((), ())), preferred_element_type=F32) * dmat
        s_old = s_scr[...]
        lhs = jnp.concatenate([scores.astype(BF16), (qc.astype(F32) * e_q).astype(BF16)], axis=1)
        rhs = jnp.concatenate([vc, s_old.astype(BF16)], axis=0)
        y = jnp.dot(lhs, rhs, preferred_element_type=F32)
        kw_t = (kc.astype(F32) * w_k).T.astype(BF16)
        s_scr[...] = s_old * dec + jnp.dot(kw_t, vc, preferred_element_type=F32)
        return y

    for direction in range(2):
        if has_init:
            s_scrs[direction][...] = s0_ref[direction]
        else:
            s_scrs[direction][...] = jnp.zeros_like(s_scrs[direction])
    tabs = (tables(0), tables(1))

    def scan_body(j, carry):
        for direction, ci in ((0, j), (1, n_chunks - 1 - j)):
            r0 = pl.multiple_of(ci * qn, qn)
            y_scrs[direction][pl.ds(r0, qn), :] = chunk(ci, direction, tabs[direction])
        return carry

    lax.fori_loop(0, n_chunks, scan_body, 0)
    if has_final:
        for direction in range(2):
            sfin_ref[direction] = s_scrs[direction][...]

    def finish_body(ci, carry):
        r0 = pl.multiple_of(ci * qn, qn)
        y = y_scrs[0][pl.ds(r0, qn), :] + y_scrs[1][pl.ds(r0, qn), :]
        y = y * lax.rsqrt(jnp.mean(y * y, axis=-1, keepdims=True) + EPS)
        y_ref[pl.ds(r0, qn), :] = (y * sg_ref[pl.ds(r0, qn), :].astype(F32)).astype(y_ref.dtype)
        return carry

    lax.fori_loop(0, n_chunks, finish_body, 0)


def _ret_scan(log_g, q, k, v, sg, s0, *, row0, n_seq, seq_len, want_final):
    assert row0 % seq_len == 0
    b0 = row0 // seq_len
    has_init = s0 is not None
    in_specs = [
        pl.BlockSpec(memory_space=pltpu.SMEM),
        pl.BlockSpec((seq_len, DK_RET), lambda b, h: (b0 + b, h)),
        pl.BlockSpec((seq_len, DK_RET), lambda b, h: (b0 + b, h)),
        pl.BlockSpec((seq_len, DV_RET), lambda b, h: (b0 + b, h)),
        pl.BlockSpec((seq_len, DV_RET), lambda b, h: (b0 + b, h)),
    ]
    args = [log_g, q, k, v, sg]
    if has_init:
        in_specs.append(pl.BlockSpec((None, 2, None, DK_RET, DV_RET), lambda b, h: (b, 0, h, 0, 0)))
        args.append(s0)
    out_shape = [jax.ShapeDtypeStruct((n_seq * seq_len, H_RET * DV_RET), BF16)]
    out_specs = [pl.BlockSpec((seq_len, DV_RET), lambda b, h: (b, h))]
    if want_final:
        out_shape.append(jax.ShapeDtypeStruct((n_seq, 2, H_RET, DK_RET, DV_RET), F32))
        out_specs.append(pl.BlockSpec((None, 2, None, DK_RET, DV_RET), lambda b, h: (b, 0, h, 0, 0)))
    kern = functools.partial(_ret_kernel, n_chunks=seq_len // CHUNK, has_init=has_init, has_final=want_final)
    return pl.pallas_call(
        kern,
        grid=(n_seq, H_RET),
        in_specs=in_specs,
        out_specs=out_specs,
        out_shape=out_shape,
        scratch_shapes=[pltpu.VMEM((seq_len, DV_RET), F32), pltpu.VMEM((seq_len, DV_RET), F32),
                        pltpu.VMEM((DK_RET, DV_RET), F32), pltpu.VMEM((DK_RET, DV_RET), F32)],
        compiler_params=_cparams(("arbitrary", "arbitrary")),
        name="ret_scan_ctx" if want_final else "ret_scan_dec",
    )(*args)


def _mixer_rows(tok, bm):
    n_ctx_tiles = tok.t_ctx // bm

    def specs(width):
        return [pl.BlockSpec((bm, width), lambda i: (jnp.minimum(i, n_ctx_tiles - 1), 0)),
                pl.BlockSpec((bm, width), lambda i: (jnp.maximum(i - n_ctx_tiles, 0), 0))]

    def select(yc_ref, yd_ref):
        return jnp.where(pl.program_id(0) < n_ctx_tiles, yc_ref[...], yd_ref[...])

    return specs, select


def _out_a_kernel(x_ref, gate_ref, yc_ref, yd_ref, u_ref, v_ref, wsp_ref, bsp_ref, w1_ref, w2_ref, o_ref, sgu_scr,
                  *, select):
    bm = x_ref.shape[0]
    for ci in range(bm // CHUNK):
        rows = slice(ci * CHUNK, (ci + 1) * CHUNK)
        for g in range(SG_GROUPS):
            cols = slice(g * LANES, (g + 1) * LANES)
            mix = jnp.dot(wsp_ref[g], v_ref[rows, cols], preferred_element_type=F32) + bsp_ref[:, cols]
            sgu_scr[rows, cols] = (u_ref[rows, cols].astype(F32) * mix).astype(BF16)
    out = jnp.dot(select(yc_ref, yd_ref), w1_ref[...], preferred_element_type=F32)
    out = out + jnp.dot(sgu_scr[...], w2_ref[...], preferred_element_type=F32)
    o_ref[...] = x_ref[...] + gate_ref[...] * out


def _out_proj_a(x, gate, y_ctx, y_dec, u, v, w_sp, b_full, w1, w2, tok, *, bm):
    t, d = x.shape
    group = tok.group_of_tile(bm)
    y_specs, select = _mixer_rows(tok, bm)
    return pl.pallas_call(
        functools.partial(_out_a_kernel, select=select),
        grid=(t // bm,),
        in_specs=[
            pl.BlockSpec((bm, d), lambda i: (i, 0)),
            pl.BlockSpec((None, 1, d), lambda i: (group(i), 0, 0)),
            *y_specs(d),
            pl.BlockSpec((bm, d), lambda i: (i, 0)),
            pl.BlockSpec((bm, d), lambda i: (i, 0)),
            pl.BlockSpec((SG_GROUPS, CHUNK, CHUNK), lambda i: (0, 0, 0)),
            pl.BlockSpec((CHUNK, d), lambda i: (0, 0)),
            pl.BlockSpec((d, d), lambda i: (0, 0)),
            pl.BlockSpec((d, d), lambda i: (0, 0)),
        ],
        out_specs=pl.BlockSpec((bm, d), lambda i: (i, 0)),
        out_shape=jax.ShapeDtypeStruct((t, d), F32),
        scratch_shapes=[pltpu.VMEM((bm, d), BF16)],
        compiler_params=_cparams(("arbitrary",)),
        name="out_proj_a",
    )(x, gate, y_ctx, y_dec, u, v, w_sp, b_full, w1, w2)


def _out_c_kernel(x_ref, gate_ref, yc_ref, yd_ref, w_ref, o_ref, *, select):
    out = jnp.dot(select(yc_ref, yd_ref), w_ref[...], preferred_element_type=F32)
    o_ref[...] = x_ref[...] + gate_ref[...] * out


def _out_proj_c(x, gate, y_ctx, y_dec, w, tok, *, bm):
    t, d = x.shape
    kdim = y_ctx.shape[1]
    group = tok.group_of_tile(bm)
    y_specs, select = _mixer_rows(tok, bm)
    return pl.pallas_call(
        functools.partial(_out_c_kernel, select=select),
        grid=(t // bm,),
        in_specs=[
            pl.BlockSpec((bm, d), lambda i: (i, 0)),
            pl.BlockSpec((None, 1, d), lambda i: (group(i), 0, 0)),
            *y_specs(kdim),
            pl.BlockSpec((kdim, d), lambda i: (0, 0)),
        ],
        out_specs=pl.BlockSpec((bm, d), lambda i: (i, 0)),
        out_shape=jax.ShapeDtypeStruct((t, d), F32),
        compiler_params=_cparams(("arbitrary",)),
        name="out_proj_c",
    )(x, gate, y_ctx, y_dec, w)


def _split_kernel(w_ref, g_ref, l_ref):
    w = w_ref[...]
    k, n2 = w.shape
    half = LANES // 2
    lane = lax.broadcasted_iota(jnp.int32, (k, LANES), 1)
    first = lane < half
    idx = jnp.where(first, 2 * lane, 2 * (lane - half) + 1)
    gs, ls = [], []
    for j in range(n2 // (2 * LANES)):
        a = jnp.take_along_axis(w[:, (2 * j) * LANES:(2 * j + 1) * LANES], idx, axis=1)
        b = jnp.take_along_axis(w[:, (2 * j + 1) * LANES:(2 * j + 2) * LANES], idx, axis=1)
        gs.append(jnp.where(first, a, pltpu.roll(b, half, axis=1)))
        ls.append(jnp.where(first, pltpu.roll(a, half, axis=1), b))
    g_ref[...] = jnp.concatenate(gs, axis=1).astype(BF16)
    l_ref[...] = jnp.concatenate(ls, axis=1).astype(BF16)


def _split_gate_lin(w_gu):
    dl, e, k, n2 = w_gu.shape
    tn = 256
    spec_out = pl.BlockSpec((None, None, k, tn), lambda a, b, j: (a, b, 0, j))
    return pl.pallas_call(
        _split_kernel,
        grid=(dl, e, n2 // (2 * tn)),
        in_specs=[pl.BlockSpec((None, None, k, 2 * tn), lambda a, b, j: (a, b, 0, j))],
        out_specs=[spec_out, spec_out],
        out_shape=[jax.ShapeDtypeStruct((dl, e, k, n2 // 2), BF16)] * 2,
        compiler_params=_cparams(("arbitrary",) * 3),
        name="split_gate_lin",
    )(w_gu)


def _router_kernel(x_ref, gam_ref, sc_ref, sh_ref, wr_ref, br_ref, h_ref, e_ref, r_ref, g_ref, cnt_ref, cnt_scr):
    i = pl.program_id(0)
    bm = x_ref.shape[0]

    @pl.when(i == 0)
    def _():
        cnt_scr[...] = jnp.zeros_like(cnt_scr)

    h = _modulated_norm(x_ref[...], gam_ref[...], sc_ref[...], sh_ref[...])
    h_ref[...] = h
    logits = jnp.dot(h, wr_ref[...], precision=HI, preferred_element_type=F32) + br_ref[...]
    lane = lax.broadcasted_iota(jnp.int32, logits.shape, 1).astype(F32)
    vals, idxs = [], []
    work = logits
    for _ in range(TOP_K):
        m = jnp.max(work, axis=-1, keepdims=True)
        idx = jnp.min(jnp.where(work == m, lane, float(LANES)), axis=-1, keepdims=True)
        vals.append(m)
        idxs.append(idx)
        work = jnp.where(lane == idx, -jnp.inf, work)
    exps = [jnp.exp(v - vals[0]) for v in vals]
    inv = 1.0 / functools.reduce(lambda a, b: a + b, exps)
    hot = functools.reduce(jnp.logical_or, [lane == idx for idx in idxs])
    hot_f = hot.astype(F32)
    ri = lax.broadcasted_iota(jnp.int32, (bm, bm), 0)
    ci = lax.broadcasted_iota(jnp.int32, (bm, bm), 1)
    before = (ci < ri).astype(BF16)
    rank_all = cnt_scr[...] + jnp.dot(before, hot_f.astype(BF16), preferred_element_type=F32)
    e_out = jnp.zeros(logits.shape, F32)
    r_out = jnp.zeros(logits.shape, F32)
    g_out = jnp.zeros(logits.shape, F32)
    for k in range(TOP_K):
        rk = jnp.sum(jnp.where(lane == idxs[k], rank_all, 0.0), axis=-1, keepdims=True)
        e_out = jnp.where(lane == float(k), idxs[k], e_out)
        r_out = jnp.where(lane == float(k), rk, r_out)
        g_out = jnp.where(lane == float(k), exps[k] * inv, g_out)
    e_ref[...] = e_out.astype(jnp.int32)
    r_ref[...] = r_out.astype(jnp.int32)
    g_ref[...] = g_out
    cnt_scr[...] = cnt_scr[...] + jnp.sum(hot_f, axis=0, keepdims=True)
    cnt_ref[...] = cnt_scr[...]


def _router(x, gam, sc, sh, w_r, b_r, tok, *, bm):
    t, d = x.shape
    group = tok.group_of_tile(bm)
    row_spec = pl.BlockSpec((bm, LANES), lambda i: (i, 0))
    return pl.pallas_call(
        _router_kernel,
        grid=(t // bm,),
        in_specs=[
            pl.BlockSpec((bm, d), lambda i: (i, 0)),
            pl.BlockSpec((1, d), lambda i: (0, 0)),
            pl.BlockSpec((None, 1, d), lambda i: (group(i), 0, 0)),
            pl.BlockSpec((None, 1, d), lambda i: (group(i), 0, 0)),
            pl.BlockSpec((d, LANES), lambda i: (0, 0)),
            pl.BlockSpec((1, LANES), lambda i: (0, 0)),
        ],
        out_specs=[pl.BlockSpec((bm, d), lambda i: (i, 0)), row_spec, row_spec, row_spec,
                   pl.BlockSpec((1, LANES), lambda i: (0, 0))],
        out_shape=[jax.ShapeDtypeStruct((t, d), F32), jax.ShapeDtypeStruct((t, LANES), jnp.int32),
                   jax.ShapeDtypeStruct((t, LANES), jnp.int32), jax.ShapeDtypeStruct((t, LANES), F32),
                   jax.ShapeDtypeStruct((1, LANES), F32)],
        scratch_shapes=[pltpu.VMEM((1, LANES), F32)],
        compiler_params=_cparams(("arbitrary",)),
        name="moe_router",
    )(x, gam, sc, sh, w_r, b_r)


def _dispatch_kernel(pe_ref, dest_ref, h_ref, xs_hbm, zbuf, sem, zsem):
    bm = h_ref.shape[0]

    @pl.when(pl.program_id(0) == 0)
    def _():
        zbuf[...] = jnp.zeros_like(zbuf)

        def tail_copy(e):
            start = pl.multiple_of(pe_ref[e + 1] - MOE_ROWS, MOE_ROWS)
            return pltpu.make_async_copy(zbuf, xs_hbm.at[pl.ds(start, MOE_ROWS)], zsem)

        for e in range(N_EXPERTS):
            @pl.when(pe_ref[e + 1] > pe_ref[e])
            def _():
                tail_copy(e).start()
        for e in range(N_EXPERTS):
            @pl.when(pe_ref[e + 1] > pe_ref[e])
            def _():
                tail_copy(e).wait()

        def spare_copy(b):
            return pltpu.make_async_copy(zbuf, xs_hbm.at[pl.ds(pl.multiple_of(b * MOE_ROWS, MOE_ROWS), MOE_ROWS)], zsem)

        first_spare = pe_ref[N_EXPERTS] // MOE_ROWS
        n_blk = xs_hbm.shape[0] // MOE_ROWS
        lax.fori_loop(first_spare, n_blk, lambda b, c: (spare_copy(b).start(), c)[1], 0)
        lax.fori_loop(first_spare, n_blk, lambda b, c: (spare_copy(b).wait(), c)[1], 0)

    def issue(r, carry):
        for k in range(TOP_K):
            pltpu.make_async_copy(h_ref.at[pl.ds(r, 1)], xs_hbm.at[pl.ds(dest_ref[r * TOP_K + k], 1)],
                                  sem).start(priority=k % 2)
        return carry

    lax.fori_loop(0, bm, issue, 0, unroll=2)
    for k in range(TOP_K):
        pltpu.make_async_copy(h_ref, xs_hbm.at[pl.ds(0, bm)], sem).wait()


def _dispatch(pad_bounds, dest, h, n_rows, *, bm):
    t, d = h.shape
    n_tiles = t // bm
    grid_spec = pltpu.PrefetchScalarGridSpec(
        num_scalar_prefetch=1,
        grid=(n_tiles,),
        in_specs=[
            pl.BlockSpec((None, None, bm * TOP_K), lambda i, pe: (i, 0, 0), memory_space=pltpu.SMEM),
            pl.BlockSpec((bm, d), lambda i, pe: (i, 0)),
        ],
        out_specs=pl.BlockSpec(memory_space=pl.ANY),
        scratch_shapes=[pltpu.VMEM((MOE_ROWS, d), F32), pltpu.SemaphoreType.DMA, pltpu.SemaphoreType.DMA],
    )
    return pl.pallas_call(
        _dispatch_kernel,
        grid_spec=grid_spec,
        out_shape=jax.ShapeDtypeStruct((n_rows, d), F32),
        compiler_params=_cparams(("arbitrary",)),
        name="moe_dispatch",
    )(pad_bounds, dest.reshape(n_tiles, 1, bm * TOP_K), h)


def _expert_kernel(be_ref, na_ref, x_ref, wg_ref, wl_ref, bg_ref, bl_ref, wd_ref, bd_ref, o_ref):
    @pl.when(pl.program_id(0) < na_ref[0])
    def _():
        x = x_ref[...].astype(BF16)
        hg = jnp.dot(x, wg_ref[...], preferred_element_type=F32) + bg_ref[...]
        hl = jnp.dot(x, wl_ref[...], preferred_element_type=F32) + bl_ref[...]
        glu = jnp.minimum(hg, SWIGLU_LIMIT)
        lin = jnp.clip(hl, -SWIGLU_LIMIT, SWIGLU_LIMIT)
        act = glu * (1.0 / (1.0 + jnp.exp(-SWIGLU_ALPHA * glu))) * (lin + 1.0)
        o_ref[...] = jnp.dot(act.astype(BF16), wd_ref[...], preferred_element_type=F32) + bd_ref[...]

    @pl.when(pl.program_id(0) >= na_ref[0])
    def _():
        o_ref[...] = jnp.zeros_like(o_ref)


def _experts(blk_expert, n_active, xs, wg, wl, bg, bl, wd, bd, layer):
    n_rows = xs.shape[0]
    d, dff = wg.shape[2], wg.shape[3]
    n_blk = n_rows // MOE_ROWS
    wmap = lambda i, be, na: (layer, be[i], 0, 0)
    rmap = lambda i, be, na: (jnp.minimum(i, na[0] - 1), 0)
    grid_spec = pltpu.PrefetchScalarGridSpec(
        num_scalar_prefetch=2,
        grid=(n_blk,),
        in_specs=[
            pl.BlockSpec((MOE_ROWS, d), rmap),
            pl.BlockSpec((None, None, d, dff), wmap),
            pl.BlockSpec((None, None, d, dff), wmap),
            pl.BlockSpec((None, None, 1, dff), wmap),
            pl.BlockSpec((None, None, 1, dff), wmap),
            pl.BlockSpec((None, None, dff, d), wmap),
            pl.BlockSpec((None, None, 1, d), wmap),
        ],
        out_specs=pl.BlockSpec((MOE_ROWS, d), lambda i, be, na: (i, 0)),
    )
    return pl.pallas_call(
        _expert_kernel,
        grid_spec=grid_spec,
        out_shape=jax.ShapeDtypeStruct((n_rows, d), F32),
        compiler_params=_cparams(("arbitrary",)),
        name="moe_experts",
    )(blk_expert, n_active, xs, wg, wl, bg, bl, wd, bd)


def _combine_kernel(dest_ref, x_ref, gate_ref, g_ref, gam_ref, ys_hbm, o_ref, *rest, final_norm, n_ctx_tiles):
    if final_norm:
        of_ref, buf, sem = rest
    else:
        buf, sem = rest
    bm = x_ref.shape[0]

    def issue(r, carry):
        for k in range(TOP_K):
            pltpu.make_async_copy(ys_hbm.at[pl.ds(dest_ref[r * TOP_K + k], 1)], buf.at[k, pl.ds(r, 1)],
                                  sem).start(priority=k % 2)
        return carry

    lax.fori_loop(0, bm, issue, 0, unroll=2)
    for k in range(TOP_K):
        pltpu.make_async_copy(ys_hbm.at[pl.ds(0, bm)], buf.at[k], sem).wait()
    gates = g_ref[...]
    acc = gates[:, 0:1] * buf[0]
    for k in range(1, TOP_K):
        acc = acc + gates[:, k:k + 1] * buf[k]
    xn = x_ref[...] + gate_ref[...] * acc
    if not final_norm:
        o_ref[...] = xn
    else:
        ms = jnp.mean(xn * xn, axis=-1, keepdims=True)
        yn = xn * lax.rsqrt(ms + EPS) * gam_ref[...]
        is_ctx = pl.program_id(0) < n_ctx_tiles

        @pl.when(is_ctx)
        def _():
            o_ref[...] = yn

        @pl.when(jnp.logical_not(is_ctx))
        def _():
            of_ref[...] = yn


def _combine(dest, x, gate, gates, gam_final, ys, tok, *, bm, final_norm):
    t, d = x.shape
    group = tok.group_of_tile(bm)
    n_tiles = t // bm
    n_ctx_tiles = tok.t_ctx // bm
    if final_norm:
        out_shape = [jax.ShapeDtypeStruct((tok.t_ctx, d), F32), jax.ShapeDtypeStruct((tok.t_dec, d), F32)]
        out_specs = [pl.BlockSpec((bm, d), lambda i: (jnp.minimum(i, n_ctx_tiles - 1), 0)),
                     pl.BlockSpec((bm, d), lambda i: (jnp.maximum(i - n_ctx_tiles, 0), 0))]
    else:
        out_shape = [jax.ShapeDtypeStruct((t, d), F32)]
        out_specs = [pl.BlockSpec((bm, d), lambda i: (i, 0))]
    return pl.pallas_call(
        functools.partial(_combine_kernel, final_norm=final_norm, n_ctx_tiles=n_ctx_tiles),
        grid=(n_tiles,),
        in_specs=[
            pl.BlockSpec((None, None, bm * TOP_K), lambda i: (i, 0, 0), memory_space=pltpu.SMEM),
            pl.BlockSpec((bm, d), lambda i: (i, 0)),
            pl.BlockSpec((None, 1, d), lambda i: (group(i), 0, 0)),
            pl.BlockSpec((bm, LANES), lambda i: (i, 0)),
            pl.BlockSpec((1, d), lambda i: (0, 0)),
            pl.BlockSpec(memory_space=pl.ANY),
        ],
        out_specs=out_specs,
        out_shape=out_shape,
        scratch_shapes=[pltpu.VMEM((TOP_K, bm, d), F32), pltpu.SemaphoreType.DMA],
        compiler_params=_cparams(("arbitrary",)),
        name="moe_combine_final" if final_norm else "moe_combine",
    )(dest.reshape(n_tiles, 1, bm * TOP_K), x, gate, gates, gam_final, ys)


def _moe(x, gam, sc, sh, gate, w_r, b_r, wg, wl, bg, bl, wd, bd, layer, gam_final, tok, *, final_norm):
    t, d = x.shape
    h, e_out, r_out, gates, counts = _router(x, gam, sc, sh, w_r, b_r, tok, bm=512)
    counts = counts[0, :N_EXPERTS].astype(jnp.int32)
    padded = (counts + MOE_ROWS - 1) // MOE_ROWS * MOE_ROWS
    pad_end = jnp.cumsum(padded)
    pad_start = pad_end - padded
    e_sel = e_out[:, :TOP_K]
    dest = pad_start[e_sel] + r_out[:, :TOP_K]
    n_rows = t * TOP_K + N_EXPERTS * MOE_ROWS
    n_blk = n_rows // MOE_ROWS
    blk_start = jnp.arange(n_blk, dtype=jnp.int32) * MOE_ROWS
    blk_expert = jnp.minimum(jnp.sum((pad_end[None, :] <= blk_start[:, None]).astype(jnp.int32), axis=1),
                             N_EXPERTS - 1)
    n_active = (pad_end[-1:] // MOE_ROWS).astype(jnp.int32)
    pad_bounds = jnp.concatenate([jnp.zeros((1,), jnp.int32), pad_end.astype(jnp.int32)])
    xs = _dispatch(pad_bounds, dest, h, n_rows, bm=256)
    ys = _experts(blk_expert, n_active, xs, wg, wl, bg, bl, wd, bd, layer)
    return _combine(dest, x, gate, gates, gam_final, ys, tok, bm=256, final_norm=final_norm)


def _rope_tables(tok):
    rows = tok.dec_len // GRID_W
    r = jnp.repeat(jnp.arange(rows), GRID_W).astype(F32)
    cidx = jnp.tile(jnp.arange(GRID_W), rows).astype(F32)
    n_freq = DK_RET // 4
    inv = ROPE_BASE ** (-jnp.arange(n_freq, dtype=F32) / n_freq)
    ang = jnp.concatenate([r[:, None] * inv, cidx[:, None] * inv], axis=-1)
    cos = jnp.concatenate([jnp.ones((tok.t_ctx, DK_RET // 2), F32), jnp.tile(jnp.cos(ang), (tok.n_dec_seq, 1))])
    sin = jnp.concatenate([jnp.zeros((tok.t_ctx, DK_RET // 2), F32), jnp.tile(jnp.sin(ang), (tok.n_dec_seq, 1))])
    return cos, sin


def _group_cols(p):
    h = p.shape[1]
    a = p.reshape(2, SSD_GROUPS, h // SSD_GROUPS).transpose(1, 0, 2).reshape(SSD_GROUPS, -1)
    return jnp.pad(a, ((0, 0), (0, LANES - a.shape[1])))[:, None, :]


def kernel(x_prompt, x_sample, state_ssd, state_ret, c, c_ctx, w_mod, b_mod, norm_mix, norm_ffn, w_in_a, conv_w, conv_b, dt_bias, a_log, d_skip, ssd_norm, w_sp, b_sp, w_out_a, w_in_c, decay_logit, w_out_c, w_router, b_router, w_gu, b_gu, w_down, b_down, norm_final):
    n_ctx, ctx_len, d = x_prompt.shape
    n_dec, dec_len, _ = x_sample.shape
    tok = _Tokens(n_ctx, ctx_len, n_dec, dec_len)
    depth = w_mod.shape[0]
    x = jnp.concatenate([x_prompt.reshape(tok.t_ctx, d), x_sample.reshape(tok.t_dec, d)])

    cvecs = jnp.concatenate([c_ctx[None], c, jnp.zeros((MOD_ROWS - 1 - n_dec, d), F32)])
    mod = _modulation(cvecs, w_mod, b_mod)
    mod = mod.reshape(depth, MOD_ROWS, 6, 1, d).transpose(0, 2, 1, 3, 4)

    h_ssd = a_log.shape[2]
    xbc_w = d + 2 * SSD_GROUPS * D_STATE
    o1, o2, o3 = d, d + xbc_w, d + xbc_w + 2 * h_ssd
    cos, sin = _rope_tables(tok)
    wg_all, wl_all = _split_gate_lin(w_gu)
    bg_all = b_gu[:, :, None, 0::2]
    bl_all = b_gu[:, :, None, 1::2]
    wd_all = w_down.astype(BF16)
    bd_all = b_down[:, :, None, :]
    new_ssd, new_ret = [], []
    y_final = None
    bm_proj = 512

    for l in range(depth):
        sh1, sc1, g1, sh2, sc2, g2 = (mod[l, j] for j in range(6))
        gam_mix = norm_mix[l][None]
        i = l // 2
        if l % 2 == 0:
            w_in = w_in_a[i]
            w_z = w_in[:, :o1].astype(BF16)
            w_xbc = w_in[:, o1:o2].astype(BF16)
            w_dt = w_in[:, o2:o3].reshape(d, 2, SSD_GROUPS, h_ssd // SSD_GROUPS).transpose(0, 2, 1, 3)
            w_dt = jnp.pad(w_dt.reshape(d, SSD_GROUPS, -1), ((0, 0), (0, 0), (0, LANES - 2 * h_ssd // SSD_GROUPS)))
            w_dt = w_dt.reshape(d, SSD_GROUPS * LANES).astype(BF16)
            w_uv = w_in[:, o3:].astype(BF16)
            xbc = _norm_proj(x, gam_mix, sc1, sh1, w_xbc, tok=tok, bm=dec_len, tn=256, out_dtype=BF16,
                             epilogue=_ep_conv_silu, extra=(conv_w[i], conv_b[i][None]),
                             extra_specs=(pl.BlockSpec((4, 256), lambda r, j: (0, j)),
                                          pl.BlockSpec((1, 256), lambda r, j: (0, j))), name="proj_xbc")
            dtb = _group_cols(dt_bias[i]).reshape(1, SSD_GROUPS * LANES)
            sz, dt, u, v = _norm_proj_multi(
                x, gam_mix, sc1, sh1, jnp.concatenate([w_z, w_dt, w_uv], axis=1), tok,
                [(d, 512, BF16, _ep_silu, False),
                 (SSD_GROUPS * LANES, SSD_GROUPS * LANES, F32, _ep_softplus_bias, True),
                 (d, 512, BF16, _ep_gelu, False), (d, d, BF16, _ep_gelu_ln, False)],
                bm=bm_proj, extra=(dtb,), extra_specs=(pl.BlockSpec((1, SSD_GROUPS * LANES), lambda r: (0, 0)),),
                name="proj_zdtuv")
            a_neg = _group_cols(-jnp.exp(a_log[i]))
            dsk = jnp.repeat(d_skip[i], SSD_HEAD).reshape(SSD_GROUPS, 1, GROUP_W)
            nrm = ssd_norm[i].reshape(SSD_GROUPS, 1, GROUP_W)
            s0 = state_ssd[:, i].reshape(n_dec, 2, SSD_GROUPS, HEADS_PER_GROUP, D_STATE, SSD_HEAD)
            s0 = s0.transpose(0, 2, 1, 4, 3, 5).reshape(n_dec, SSD_GROUPS, 2, D_STATE, GROUP_W)
            y_ctx, s_fin = _ssd_scan(xbc, dt, sz, a_neg, dsk, nrm, None, row0=0, n_seq=n_ctx,
                                     seq_len=ctx_len, want_final=True)
            (y_dec,) = _ssd_scan(xbc, dt, sz, a_neg, dsk, nrm, s0, row0=tok.t_ctx, n_seq=n_dec,
                                 seq_len=dec_len, want_final=False)
            s_fin = s_fin.reshape(n_ctx, SSD_GROUPS, 2, D_STATE, HEADS_PER_GROUP, SSD_HEAD)
            new_ssd.append(s_fin.transpose(0, 2, 1, 4, 3, 5).reshape(n_ctx, 2, h_ssd, D_STATE, SSD_HEAD))
            b_full = jnp.repeat(b_sp[i].T, LANES, axis=1)
            w_o = w_out_a[i].astype(BF16)
            x = _out_proj_a(x, g1, y_ctx, y_dec, u, v, w_sp[i].astype(BF16), b_full, w_o[:d], w_o[d:], tok, bm=512)
        else:
            hk = H_RET * DK_RET
            hv = H_RET * DV_RET
            w_in = w_in_c[i].astype(BF16)
            rope_specs = (pl.BlockSpec((bm_proj, DK_RET // 2), lambda r: (r, 0)),) * 2
            q, kk, vv, sg = _norm_proj_multi(
                x, gam_mix, sc1, sh1, w_in, tok,
                [(hk, 512, BF16, functools.partial(_ep_rope, scale=1.0), True),
                 (hk, 512, BF16, functools.partial(_ep_rope, scale=DK_RET ** -0.5), True),
                 (hv, 512, BF16, _ep_plain, False), (hv, 512, BF16, _ep_silu, False)],
                bm=bm_proj, extra=(cos, sin), extra_specs=rope_specs, name="proj_qkvg")
            log_g = jax.nn.log_sigmoid(decay_logit[i].astype(F32))
            y_ctx, s_fin = _ret_scan(log_g, q, kk, vv, sg, None, row0=0, n_seq=n_ctx, seq_len=ctx_len, want_final=True)
            (y_dec,) = _ret_scan(log_g, q, kk, vv, sg, state_ret[:, i], row0=tok.t_ctx, n_seq=n_dec, seq_len=dec_len,
                                 want_final=False)
            new_ret.append(s_fin)
            x = _out_proj_c(x, g1, y_ctx, y_dec, w_out_c[i].astype(BF16), tok, bm=512)

        w_r = jnp.pad(w_router[l], ((0, 0), (0, LANES - N_EXPERTS)))
        b_r = jnp.pad(b_router[l], (0, LANES - N_EXPERTS), constant_values=-1e30)[None]
        last = l == depth - 1
        res = _moe(x, norm_ffn[l][None], sc2, sh2, g2, w_r, b_r, wg_all, wl_all, bg_all, bl_all, wd_all, bd_all, l,
                   norm_final[None], tok, final_norm=last)
        if last:
            y_final = res
        else:
            x = res[0]

    y_prompt = y_final[0].reshape(n_ctx, ctx_len, d)
    y_sample = y_final[1].reshape(n_dec, dec_len, d)
    return (y_prompt, y_sample, jnp.stack(new_ssd, axis=1), jnp.stack(new_ret, axis=1))
```

```python
import functools
import math

import jax
import jax.numpy as jnp
from jax import lax
from jax.experimental import pallas as pl
from jax.experimental.pallas import tpu as pltpu

D_MODEL = 1024
GRID_W = 64
CHUNK = 128
SSD_HEAD = 64
SSD_GROUPS = 2
D_STATE = 128
GROUP_W = D_MODEL // SSD_GROUPS
HEADS_PER_GROUP = GROUP_W // SSD_HEAD
SG_GROUPS = 8
H_RET = 4
DK_RET = D_MODEL // H_RET
DV_RET = 2 * DK_RET
ROPE_BASE = 10000.0
N_EXPERTS = 32
TOP_K = 4
SWIGLU_LIMIT = 7.0
SWIGLU_ALPHA = 1.702
EPS = 1e-6

LANES = 128
MOD_ROWS = 8
MOE_ROWS = 512
VMEM_LIMIT = 56 * 1024 * 1024

F32 = jnp.float32
BF16 = jnp.bfloat16
HI = lax.Precision.HIGHEST


def _cparams(sem):
    return pltpu.CompilerParams(dimension_semantics=sem, vmem_limit_bytes=VMEM_LIMIT)


def _silu(x):
    return x * (1.0 / (1.0 + jnp.exp(-x)))


def _gelu_tanh(x):
    return 0.5 * x * (1.0 + jnp.tanh(math.sqrt(2.0 / math.pi) * (x + 0.044715 * (x * x * x))))


def _softplus(x):
    return jnp.maximum(x, 0.0) + jnp.log(1.0 + jnp.exp(-jnp.abs(x)))


def _mod_kernel(c_ref, w_ref, b_ref, o_ref):
    a = _silu(c_ref[...])
    o_ref[...] = jnp.dot(a, w_ref[...], precision=HI, preferred_element_type=F32) + b_ref[...]


def _modulation(cvecs, w_mod, b_mod):
    depth, d, n = w_mod.shape
    tn = 1536
    return pl.pallas_call(
        _mod_kernel,
        grid=(depth, n // tn),
        in_specs=[
            pl.BlockSpec((MOD_ROWS, d), lambda l, j: (0, 0)),
            pl.BlockSpec((None, d, tn), lambda l, j: (l, 0, j)),
            pl.BlockSpec((None, 1, tn), lambda l, j: (l, 0, j)),
        ],
        out_specs=pl.BlockSpec((None, MOD_ROWS, tn), lambda l, j: (l, 0, j)),
        out_shape=jax.ShapeDtypeStruct((depth, MOD_ROWS, n), F32),
        compiler_params=_cparams(("arbitrary", "arbitrary")),
        name="modulation",
    )(cvecs, w_mod, b_mod.reshape(depth, 1, n))


class _Tokens:
    def __init__(self, n_ctx_seq, ctx_len, n_dec_seq, dec_len):
        self.n_ctx_seq, self.ctx_len = n_ctx_seq, ctx_len
        self.n_dec_seq, self.dec_len = n_dec_seq, dec_len
        self.t_ctx = n_ctx_seq * ctx_len
        self.t_dec = n_dec_seq * dec_len
        self.total = self.t_ctx + self.t_dec

    def group_of_tile(self, bm):
        assert self.t_ctx % bm == 0 and self.dec_len % bm == 0
        n_ctx_tiles = self.t_ctx // bm
        per_seq = self.dec_len // bm

        def group(i):
            return jnp.where(i < n_ctx_tiles, 0, 1 + (i - n_ctx_tiles) // per_seq)

        return group


def _modulated_norm(x, gam, sc, sh):
    ms = jnp.mean(x * x, axis=-1, keepdims=True)
    return (x * lax.rsqrt(ms + EPS) * gam) * (1.0 + sc) + sh


def _proj_kernel(*refs, epilogue, n_extra, tok, bm):
    x_ref, gam_ref, sc_ref, sh_ref, w_ref = refs[:5]
    extra = refs[5:5 + n_extra]
    o_ref = refs[5 + n_extra]
    h_scr = refs[6 + n_extra]
    i = pl.program_id(0)

    @pl.when(pl.program_id(1) == 0)
    def _():
        h_scr[...] = _modulated_norm(x_ref[...], gam_ref[...], sc_ref[...], sh_ref[...]).astype(BF16)

    acc = jnp.dot(h_scr[...], w_ref[...], preferred_element_type=F32)
    o_ref[...] = epilogue(acc, i, tok, bm, *extra).astype(o_ref.dtype)


def _ep_plain(acc, i, tok, bm):
    return acc


def _ep_silu(acc, i, tok, bm):
    return _silu(acc)


def _ep_gelu(acc, i, tok, bm):
    return _gelu_tanh(acc)


def _ep_softplus_bias(acc, i, tok, bm, bias_ref):
    return _softplus(acc + bias_ref[...])


def _ep_gelu_ln(acc, i, tok, bm):
    g = _gelu_tanh(acc)
    mu = jnp.mean(g, axis=-1, keepdims=True)
    gc = g - mu
    return gc * lax.rsqrt(jnp.mean(gc * gc, axis=-1, keepdims=True) + 1e-5)


def _ep_conv_silu(acc, i, tok, bm, cw_ref, cb_ref):
    n = acc.shape[0]
    seq = jnp.where(i * bm < tok.t_ctx, tok.ctx_len, tok.dec_len)
    t = lax.broadcasted_iota(jnp.int32, (n, 1), 0) & (seq - 1)
    cw = cw_ref[...]
    y = acc * cw[2:3, :] + cb_ref[...]
    y = y + jnp.where(t >= 2, pltpu.roll(acc, 2, axis=0), 0.0) * cw[0:1, :]
    y = y + jnp.where(t >= 1, pltpu.roll(acc, 1, axis=0), 0.0) * cw[1:2, :]
    y = y + jnp.where(t < seq - 1, pltpu.roll(acc, n - 1, axis=0), 0.0) * cw[3:4, :]
    return _silu(y)


def _ep_rope(acc, i, tok, bm, cos_ref, sin_ref, *, scale):
    cs, sn = cos_ref[...], sin_ref[...]
    half = DK_RET // 2
    outs = []
    for h in range(acc.shape[1] // DK_RET):
        x1 = acc[:, h * DK_RET:h * DK_RET + half] * scale
        x2 = acc[:, h * DK_RET + half:(h + 1) * DK_RET] * scale
        outs.append(x1 * cs - x2 * sn)
        outs.append(x2 * cs + x1 * sn)
    return jnp.concatenate(outs, axis=1)


def _norm_proj(x, gam, sc, sh, w, tok, *, bm, tn, out_dtype, epilogue, extra=(), extra_specs=(), name):
    t, d = x.shape
    n = w.shape[1]
    group = tok.group_of_tile(bm)
    kern = functools.partial(_proj_kernel, epilogue=epilogue, n_extra=len(extra), tok=tok, bm=bm)
    return pl.pallas_call(
        kern,
        grid=(t // bm, n // tn),
        in_specs=[
            pl.BlockSpec((bm, d), lambda i, j: (i, 0)),
            pl.BlockSpec((1, d), lambda i, j: (0, 0)),
            pl.BlockSpec((None, 1, d), lambda i, j: (group(i), 0, 0)),
            pl.BlockSpec((None, 1, d), lambda i, j: (group(i), 0, 0)),
            pl.BlockSpec((d, tn), lambda i, j: (0, j)),
            *extra_specs,
        ],
        out_specs=pl.BlockSpec((bm, tn), lambda i, j: (i, j)),
        out_shape=jax.ShapeDtypeStruct((t, n), out_dtype),
        scratch_shapes=[pltpu.VMEM((bm, d), BF16)],
        compiler_params=_cparams(("arbitrary", "arbitrary")),
        name=name,
    )(x, gam, sc, sh, w, *extra)


def _proj_multi_kernel(*refs, pieces, n_extra, tok, bm):
    x_ref, gam_ref, sc_ref, sh_ref, w_ref = refs[:5]
    extra = refs[5:5 + n_extra]
    outs = refs[5 + n_extra:5 + n_extra + len(pieces)]
    h_scr = refs[5 + n_extra + len(pieces)]
    i = pl.program_id(0)
    h_scr[...] = _modulated_norm(x_ref[...], gam_ref[...], sc_ref[...], sh_ref[...]).astype(BF16)
    c0 = 0
    for (width, chunk, _, epilogue, uses_extra), o_ref in zip(pieces, outs):
        for j in range(width // chunk):
            acc = jnp.dot(h_scr[...], w_ref[:, c0 + j * chunk:c0 + (j + 1) * chunk], preferred_element_type=F32)
            res = epilogue(acc, i, tok, bm, *(extra if uses_extra else ()))
            o_ref[:, j * chunk:(j + 1) * chunk] = res.astype(o_ref.dtype)
        c0 += width


def _norm_proj_multi(x, gam, sc, sh, w, tok, pieces, *, bm, extra=(), extra_specs=(), name):
    t, d = x.shape
    n = w.shape[1]
    assert n == sum(p[0] for p in pieces)
    group = tok.group_of_tile(bm)
    kern = functools.partial(_proj_multi_kernel, pieces=tuple(pieces), n_extra=len(extra), tok=tok, bm=bm)
    return pl.pallas_call(
        kern,
        grid=(t // bm,),
        in_specs=[
            pl.BlockSpec((bm, d), lambda i: (i, 0)),
            pl.BlockSpec((1, d), lambda i: (0, 0)),
            pl.BlockSpec((None, 1, d), lambda i: (group(i), 0, 0)),
            pl.BlockSpec((None, 1, d), lambda i: (group(i), 0, 0)),
            pl.BlockSpec((d, n), lambda i: (0, 0), pipeline_mode=pl.Buffered(1)),
            *extra_specs,
        ],
        out_specs=[pl.BlockSpec((bm, p[0]), lambda i: (i, 0)) for p in pieces],
        out_shape=[jax.ShapeDtypeStruct((t, p[0]), p[2]) for p in pieces],
        scratch_shapes=[pltpu.VMEM((bm, d), BF16)],
        compiler_params=_cparams(("arbitrary",)),
        name=name,
    )(x, gam, sc, sh, w, *extra)


def _ssd_kernel(*refs, n_chunks, has_init, has_final):
    xs_ref, b_ref, c_ref, dt_ref, sz_ref, a_ref, dsk_ref, nrm_ref = refs[:8]
    k = 8
    s0_ref = None
    if has_init:
        s0_ref = refs[k]
        k += 1
    y_ref = refs[k]
    k += 1
    sfin_ref = None
    if has_final:
        sfin_ref = refs[k]
        k += 1
    y_scrs = (refs[k], refs[k + 1])
    s_scrs = (refs[k + 2], refs[k + 3])

    q = CHUNK
    row = lax.broadcasted_iota(jnp.int32, (q, q), 0)
    col = lax.broadcasted_iota(jnp.int32, (q, q), 1)
    lane = lax.broadcasted_iota(jnp.int32, (1, q), 1)
    left = lane < SSD_HEAD
    a_neg = a_ref[...]
    n_pairs = HEADS_PER_GROUP // 2
    keeps = (col <= row, col >= row)
    tris = tuple(kp.astype(F32).astype(BF16) for kp in keeps)

    def masked_sums(tri, x):
        hi = x.astype(BF16)
        r1 = x - hi.astype(F32)
        mid = r1.astype(BF16)
        lo = (r1 - mid.astype(F32)).astype(BF16)
        return (jnp.dot(tri, hi, preferred_element_type=F32) + jnp.dot(tri, mid, preferred_element_type=F32)
                + jnp.dot(tri, lo, preferred_element_type=F32))

    def chunk(ci, direction):
        s_scr = s_scrs[direction]
        r0 = pl.multiple_of(ci * q, q)
        xs = xs_ref[pl.ds(r0, q), :]
        bm_ = b_ref[pl.ds(r0, q), :]
        cm = c_ref[pl.ds(r0, q), :]
        dt = dt_ref[pl.ds(r0, q), :]
        la = dt * a_neg
        keep, tri = keeps[direction], tris[direction]
        last = q - 1 if direction == 0 else 0
        cum = masked_sums(tri, la)
        cum_t = cum.T
        dt_t = dt.T
        tot_t = jnp.broadcast_to(cum_t[:, last:last + 1], (q, q))
        w_t = dt_t * jnp.exp(tot_t - cum_t)
        g = lax.dot_general(cm, bm_, (((1,), (1,)), ((), ())), preferred_element_type=F32)
        b_t = bm_.astype(F32).T
        cm_f = cm.astype(F32)
        outs = []
        for p in range(n_pairs):
            xs_p = xs[:, p * LANES:(p + 1) * LANES]
            s_p = s_scr[:, p * LANES:(p + 1) * LANES]
            s_b = s_p.astype(BF16)
            zero = jnp.zeros_like(xs_p)
            zero_s = jnp.zeros_like(s_b)
            lhs, rhs, lhs_s, rhs_s, decs = [], [], [], [], []
            for hh in range(2):
                cidx = direction * HEADS_PER_GROUP + 2 * p + hh
                cum_b = jnp.broadcast_to(cum[:, cidx:cidx + 1], (q, q))
                dec = jnp.exp(jnp.where(keep, cum_b - cum_t[cidx:cidx + 1, :], -jnp.inf))
                scores = g * dec * dt_t[cidx:cidx + 1, :]
                lhs += [scores.astype(BF16), (cm_f * jnp.exp(cum_b)).astype(BF16)]
                sel = left if hh == 0 else jnp.logical_not(left)
                rhs += [jnp.where(sel, xs_p, zero), jnp.where(sel, s_b, zero_s)]
                lhs_s.append((b_t * w_t[cidx:cidx + 1, :]).astype(BF16))
                rhs_s.append(jnp.where(sel, xs_p, zero))
                decs.append(jnp.exp(cum_t[cidx:cidx + 1, last:last + 1]))
            y_p = jnp.dot(jnp.concatenate(lhs, axis=1), jnp.concatenate(rhs, axis=0),
                          preferred_element_type=F32)
            upd = jnp.dot(jnp.concatenate(lhs_s, axis=1), jnp.concatenate(rhs_s, axis=0),
                          preferred_element_type=F32)
            s_scr[:, p * LANES:(p + 1) * LANES] = s_p * jnp.where(left, decs[0], decs[1]) + upd
            outs.append(y_p)
        return jnp.concatenate(outs, axis=1)

    for direction in range(2):
        if has_init:
            s_scrs[direction][...] = s0_ref[direction]
        else:
            s_scrs[direction][...] = jnp.zeros_like(s_scrs[direction])

    def scan_body(j, carry):
        for direction, ci in ((0, j), (1, n_chunks - 1 - j)):
            r0 = pl.multiple_of(ci * q, q)
            y_scrs[direction][pl.ds(r0, q), :] = chunk(ci, direction)
        return carry

    lax.fori_loop(0, n_chunks, scan_body, 0)
    if has_final:
        for direction in range(2):
            sfin_ref[direction] = s_scrs[direction][...]

    def finish_body(ci, carry):
        r0 = pl.multiple_of(ci * q, q)
        y = y_scrs[0][pl.ds(r0, q), :] + y_scrs[1][pl.ds(r0, q), :]
        y = y + dsk_ref[...] * xs_ref[pl.ds(r0, q), :].astype(F32)
        y = y * sz_ref[pl.ds(r0, q), :].astype(F32)
        y = y * lax.rsqrt(jnp.mean(y * y, axis=-1, keepdims=True) + EPS)
        y_ref[pl.ds(r0, q), :] = (y * nrm_ref[...]).astype(y_ref.dtype)
        return carry

    lax.fori_loop(0, n_chunks, finish_body, 0)


def _ssd_scan(xbc, dt, sz, a_neg, dsk, nrm, s0, *, row0, n_seq, seq_len, want_final):
    assert row0 % seq_len == 0
    b0 = row0 // seq_len
    has_init = s0 is not None
    gw = GROUP_W
    b_blk0 = D_MODEL // D_STATE
    c_blk0 = b_blk0 + SSD_GROUPS
    in_specs = [
        pl.BlockSpec((seq_len, gw), lambda b, g: (b0 + b, g)),
        pl.BlockSpec((seq_len, D_STATE), lambda b, g: (b0 + b, b_blk0 + g)),
        pl.BlockSpec((seq_len, D_STATE), lambda b, g: (b0 + b, c_blk0 + g)),
        pl.BlockSpec((seq_len, LANES), lambda b, g: (b0 + b, g)),
        pl.BlockSpec((seq_len, gw), lambda b, g: (b0 + b, g)),
        pl.BlockSpec((None, 1, LANES), lambda b, g: (g, 0, 0)),
        pl.BlockSpec((None, 1, gw), lambda b, g: (g, 0, 0)),
        pl.BlockSpec((None, 1, gw), lambda b, g: (g, 0, 0)),
    ]
    args = [xbc, xbc, xbc, dt, sz, a_neg, dsk, nrm]
    if has_init:
        in_specs.append(pl.BlockSpec((None, None, 2, D_STATE, gw), lambda b, g: (b, g, 0, 0, 0)))
        args.append(s0)
    out_shape = [jax.ShapeDtypeStruct((n_seq * seq_len, D_MODEL), BF16)]
    out_specs = [pl.BlockSpec((seq_len, gw), lambda b, g: (b, g))]
    if want_final:
        out_shape.append(jax.ShapeDtypeStruct((n_seq, SSD_GROUPS, 2, D_STATE, gw), F32))
        out_specs.append(pl.BlockSpec((None, None, 2, D_STATE, gw), lambda b, g: (b, g, 0, 0, 0)))
    kern = functools.partial(_ssd_kernel, n_chunks=seq_len // CHUNK, has_init=has_init, has_final=want_final)
    return pl.pallas_call(
        kern,
        grid=(n_seq, SSD_GROUPS),
        in_specs=in_specs,
        out_specs=out_specs,
        out_shape=out_shape,
        scratch_shapes=[pltpu.VMEM((seq_len, gw), F32), pltpu.VMEM((seq_len, gw), F32),
                        pltpu.VMEM((D_STATE, gw), F32), pltpu.VMEM((D_STATE, gw), F32)],
        compiler_params=_cparams(("arbitrary", "arbitrary")),
        name="ssd_scan_ctx" if want_final else "ssd_scan_dec",
    )(*args)


def _ret_kernel(*refs, n_chunks, has_init, has_final):
    lg_ref, q_ref, k_ref, v_ref, sg_ref = refs[:5]
    k = 5
    s0_ref = None
    if has_init:
        s0_ref = refs[k]
        k += 1
    y_ref = refs[k]
    k += 1
    sfin_ref = None
    if has_final:
        sfin_ref = refs[k]
        k += 1
    y_scrs = (refs[k], refs[k + 1])
    s_scrs = (refs[k + 2], refs[k + 3])

    qn = CHUNK
    h = pl.program_id(1)
    row = lax.broadcasted_iota(jnp.int32, (qn, qn), 0)
    col = lax.broadcasted_iota(jnp.int32, (qn, qn), 1)
    rowk = lax.broadcasted_iota(jnp.int32, (qn, DK_RET), 0).astype(F32)

    def tables(direction):
        lg = lg_ref[direction, h]
        if direction == 0:
            keep = col <= row
            dist = (row - col).astype(F32)
            e_q = jnp.exp(lg * (rowk + 1.0))
            w_k = jnp.exp(lg * (qn - 1.0 - rowk))
        else:
            keep = col >= row
            dist = (col - row).astype(F32)
            e_q = jnp.exp(lg * (qn - rowk))
            w_k = jnp.exp(lg * rowk)
        dmat = jnp.where(keep, jnp.exp(lg * dist), 0.0)
        return dmat, e_q, w_k, jnp.exp(jnp.full((1, 1), float(qn), F32) * lg)

    def chunk(ci, direction, tabs):
        dmat, e_q, w_k, dec = tabs
        s_scr = s_scrs[direction]
        r0 = pl.multiple_of(ci * qn, qn)
        qc = q_ref[pl.ds(r0, qn), :]
        kc = k_ref[pl.ds(r0, qn), :]
        vc = v_ref[pl.ds(r0, qn), :]
        scores = lax.dot_general(qc, kc, (((1,), (1,)), ((), ())), preferred_element_type=F32) * dmat
        s_old = s_scr[...]
        lhs = jnp.concatenate([scores.astype(BF16), (qc.astype(F32) * e_q).astype(BF16)], axis=1)
        rhs = jnp.concatenate([vc, s_old.astype(BF16)], axis=0)
        y = jnp.dot(lhs, rhs, preferred_element_type=F32)
        kw_t = (kc.astype(F32) * w_k).T.astype(BF16)
        s_scr[...] = s_old * dec + jnp.dot(kw_t, vc, preferred_element_type=F32)
        return y

    for direction in range(2):
        if has_init:
            s_scrs[direction][...] = s0_ref[direction]
        else:
            s_scrs[direction][...] = jnp.zeros_like(s_scrs[direction])
    tabs = (tables(0), tables(1))

    def scan_body(j, carry):
        for direction, ci in ((0, j), (1, n_chunks - 1 - j)):
            r0 = pl.multiple_of(ci * qn, qn)
            y_scrs[direction][pl.ds(r0, qn), :] = chunk(ci, direction, tabs[direction])
        return carry

    lax.fori_loop(0, n_chunks, scan_body, 0)
    if has_final:
        for direction in range(2):
            sfin_ref[direction] = s_scrs[direction][...]

    def finish_body(ci, carry):
        r0 = pl.multiple_of(ci * qn, qn)
        y = y_scrs[0][pl.ds(r0, qn), :] + y_scrs[1][pl.ds(r0, qn), :]
        y = y * lax.rsqrt(jnp.mean(y * y, axis=-1, keepdims=True) + EPS)
        y_ref[pl.ds(r0, qn), :] = (y * sg_ref[pl.ds(r0, qn), :].astype(F32)).astype(y_ref.dtype)
        return carry

    lax.fori_loop(0, n_chunks, finish_body, 0)


def _ret_scan(log_g, q, k, v, sg, s0, *, row0, n_seq, seq_len, want_final):
    assert row0 % seq_len == 0
    b0 = row0 // seq_len
    has_init = s0 is not None
    in_specs = [
        pl.BlockSpec(memory_space=pltpu.SMEM),
        pl.BlockSpec((seq_len, DK_RET), lambda b, h: (b0 + b, h)),
        pl.BlockSpec((seq_len, DK_RET), lambda b, h: (b0 + b, h)),
        pl.BlockSpec((seq_len, DV_RET), lambda b, h: (b0 + b, h)),
        pl.BlockSpec((seq_len, DV_RET), lambda b, h: (b0 + b, h)),
    ]
    args = [log_g, q, k, v, sg]
    if has_init:
        in_specs.append(pl.BlockSpec((None, 2, None, DK_RET, DV_RET), lambda b, h: (b, 0, h, 0, 0)))
        args.append(s0)
    out_shape = [jax.ShapeDtypeStruct((n_seq * seq_len, H_RET * DV_RET), BF16)]
    out_specs = [pl.BlockSpec((seq_len, DV_RET), lambda b, h: (b, h))]
    if want_final:
        out_shape.append(jax.ShapeDtypeStruct((n_seq, 2, H_RET, DK_RET, DV_RET), F32))
        out_specs.append(pl.BlockSpec((None, 2, None, DK_RET, DV_RET), lambda b, h: (b, 0, h, 0, 0)))
    kern = functools.partial(_ret_kernel, n_chunks=seq_len // CHUNK, has_init=has_init, has_final=want_final)
    return pl.pallas_call(
        kern,
        grid=(n_seq, H_RET),
        in_specs=in_specs,
        out_specs=out_specs,
        out_shape=out_shape,
        scratch_shapes=[pltpu.VMEM((seq_len, DV_RET), F32), pltpu.VMEM((seq_len, DV_RET), F32),
                        pltpu.VMEM((DK_RET, DV_RET), F32), pltpu.VMEM((DK_RET, DV_RET), F32)],
        compiler_params=_cparams(("arbitrary", "arbitrary")),
        name="ret_scan_ctx" if want_final else "ret_scan_dec",
    )(*args)


def _mixer_rows(tok, bm):
    n_ctx_tiles = tok.t_ctx // bm

    def specs(width):
        return [pl.BlockSpec((bm, width), lambda i: (jnp.minimum(i, n_ctx_tiles - 1), 0)),
                pl.BlockSpec((bm, width), lambda i: (jnp.maximum(i - n_ctx_tiles, 0), 0))]

    def select(yc_ref, yd_ref):
        return jnp.where(pl.program_id(0) < n_ctx_tiles, yc_ref[...], yd_ref[...])

    return specs, select


def _out_a_kernel(x_ref, gate_ref, yc_ref, yd_ref, u_ref, v_ref, wsp_ref, bsp_ref, w1_ref, w2_ref, o_ref, sgu_scr,
                  *, select):
    bm = x_ref.shape[0]
    for ci in range(bm // CHUNK):
        rows = slice(ci * CHUNK, (ci + 1) * CHUNK)
        for g in range(SG_GROUPS):
            cols = slice(g * LANES, (g + 1) * LANES)
            mix = jnp.dot(wsp_ref[g], v_ref[rows, cols], preferred_element_type=F32) + bsp_ref[:, cols]
            sgu_scr[rows, cols] = (u_ref[rows, cols].astype(F32) * mix).astype(BF16)
    out = jnp.dot(select(yc_ref, yd_ref), w1_ref[...], preferred_element_type=F32)
    out = out + jnp.dot(sgu_scr[...], w2_ref[...], preferred_element_type=F32)
    o_ref[...] = x_ref[...] + gate_ref[...] * out


def _out_proj_a(x, gate, y_ctx, y_dec, u, v, w_sp, b_full, w1, w2, tok, *, bm):
    t, d = x.shape
    group = tok.group_of_tile(bm)
    y_specs, select = _mixer_rows(tok, bm)
    return pl.pallas_call(
        functools.partial(_out_a_kernel, select=select),
        grid=(t // bm,),
        in_specs=[
            pl.BlockSpec((bm, d), lambda i: (i, 0)),
            pl.BlockSpec((None, 1, d), lambda i: (group(i), 0, 0)),
            *y_specs(d),
            pl.BlockSpec((bm, d), lambda i: (i, 0)),
            pl.BlockSpec((bm, d), lambda i: (i, 0)),
            pl.BlockSpec((SG_GROUPS, CHUNK, CHUNK), lambda i: (0, 0, 0)),
            pl.BlockSpec((CHUNK, d), lambda i: (0, 0)),
            pl.BlockSpec((d, d), lambda i: (0, 0)),
            pl.BlockSpec((d, d), lambda i: (0, 0)),
        ],
        out_specs=pl.BlockSpec((bm, d), lambda i: (i, 0)),
        out_shape=jax.ShapeDtypeStruct((t, d), F32),
        scratch_shapes=[pltpu.VMEM((bm, d), BF16)],
        compiler_params=_cparams(("arbitrary",)),
        name="out_proj_a",
    )(x, gate, y_ctx, y_dec, u, v, w_sp, b_full, w1, w2)


def _out_c_kernel(x_ref, gate_ref, yc_ref, yd_ref, w_ref, o_ref, *, select):
    out = jnp.dot(select(yc_ref, yd_ref), w_ref[...], preferred_element_type=F32)
    o_ref[...] = x_ref[...] + gate_ref[...] * out


def _out_proj_c(x, gate, y_ctx, y_dec, w, tok, *, bm):
    t, d = x.shape
    kdim = y_ctx.shape[1]
    group = tok.group_of_tile(bm)
    y_specs, select = _mixer_rows(tok, bm)
    return pl.pallas_call(
        functools.partial(_out_c_kernel, select=select),
        grid=(t // bm,),
        in_specs=[
            pl.BlockSpec((bm, d), lambda i: (i, 0)),
            pl.BlockSpec((None, 1, d), lambda i: (group(i), 0, 0)),
            *y_specs(kdim),
            pl.BlockSpec((kdim, d), lambda i: (0, 0)),
        ],
        out_specs=pl.BlockSpec((bm, d), lambda i: (i, 0)),
        out_shape=jax.ShapeDtypeStruct((t, d), F32),
        compiler_params=_cparams(("arbitrary",)),
        name="out_proj_c",
    )(x, gate, y_ctx, y_dec, w)


def _split_kernel(w_ref, g_ref, l_ref):
    w = w_ref[...]
    k, n2 = w.shape
    half = LANES // 2
    lane = lax.broadcasted_iota(jnp.int32, (k, LANES), 1)
    first = lane < half
    idx = jnp.where(first, 2 * lane, 2 * (lane - half) + 1)
    gs, ls = [], []
    for j in range(n2 // (2 * LANES)):
        a = jnp.take_along_axis(w[:, (2 * j) * LANES:(2 * j + 1) * LANES], idx, axis=1)
        b = jnp.take_along_axis(w[:, (2 * j + 1) * LANES:(2 * j + 2) * LANES], idx, axis=1)
        gs.append(jnp.where(first, a, pltpu.roll(b, half, axis=1)))
        ls.append(jnp.where(first, pltpu.roll(a, half, axis=1), b))
    g_ref[...] = jnp.concatenate(gs, axis=1).astype(BF16)
    l_ref[...] = jnp.concatenate(ls, axis=1).astype(BF16)


def _split_gate_lin(w_gu):
    dl, e, k, n2 = w_gu.shape
    tn = 512
    spec_out = pl.BlockSpec((None, None, k, tn), lambda a, b, j: (a, b, 0, j))
    return pl.pallas_call(
        _split_kernel,
        grid=(dl, e, n2 // (2 * tn)),
        in_specs=[pl.BlockSpec((None, None, k, 2 * tn), lambda a, b, j: (a, b, 0, j))],
        out_specs=[spec_out, spec_out],
        out_shape=[jax.ShapeDtypeStruct((dl, e, k, n2 // 2), BF16)] * 2,
        compiler_params=_cparams(("arbitrary",) * 3),
        name="split_gate_lin",
    )(w_gu)


def _router_kernel(x_ref, gam_ref, sc_ref, sh_ref, wr_ref, br_ref, h_ref, e_ref, r_ref, g_ref, cnt_ref, cnt_scr):
    i = pl.program_id(0)
    bm = x_ref.shape[0]

    @pl.when(i == 0)
    def _():
        cnt_scr[...] = jnp.zeros_like(cnt_scr)

    h = _modulated_norm(x_ref[...], gam_ref[...], sc_ref[...], sh_ref[...])
    h_ref[...] = h
    logits = jnp.dot(h, wr_ref[...], precision=HI, preferred_element_type=F32) + br_ref[...]
    lane = lax.broadcasted_iota(jnp.int32, logits.shape, 1).astype(F32)
    vals, idxs = [], []
    work = logits
    for _ in range(TOP_K):
        m = jnp.max(work, axis=-1, keepdims=True)
        idx = jnp.min(jnp.where(work == m, lane, float(LANES)), axis=-1, keepdims=True)
        vals.append(m)
        idxs.append(idx)
        work = jnp.where(lane == idx, -jnp.inf, work)
    exps = [jnp.exp(v - vals[0]) for v in vals]
    inv = 1.0 / functools.reduce(lambda a, b: a + b, exps)
    hot = functools.reduce(jnp.logical_or, [lane == idx for idx in idxs])
    hot_f = hot.astype(F32)
    ri = lax.broadcasted_iota(jnp.int32, (bm, bm), 0)
    ci = lax.broadcasted_iota(jnp.int32, (bm, bm), 1)
    before = (ci < ri).astype(BF16)
    rank_all = cnt_scr[...] + jnp.dot(before, hot_f.astype(BF16), preferred_element_type=F32)
    e_out = jnp.zeros(logits.shape, F32)
    r_out = jnp.zeros(logits.shape, F32)
    g_out = jnp.zeros(logits.shape, F32)
    for k in range(TOP_K):
        rk = jnp.sum(jnp.where(lane == idxs[k], rank_all, 0.0), axis=-1, keepdims=True)
        e_out = jnp.where(lane == float(k), idxs[k], e_out)
        r_out = jnp.where(lane == float(k), rk, r_out)
        g_out = jnp.where(lane == float(k), exps[k] * inv, g_out)
    e_ref[...] = e_out.astype(jnp.int32)
    r_ref[...] = r_out.astype(jnp.int32)
    g_ref[...] = g_out
    cnt_scr[...] = cnt_scr[...] + jnp.sum(hot_f, axis=0, keepdims=True)
    cnt_ref[...] = cnt_scr[...]


def _router(x, gam, sc, sh, w_r, b_r, tok, *, bm):
    t, d = x.shape
    group = tok.group_of_tile(bm)
    row_spec = pl.BlockSpec((bm, LANES), lambda i: (i, 0))
    return pl.pallas_call(
        _router_kernel,
        grid=(t // bm,),
        in_specs=[
            pl.BlockSpec((bm, d), lambda i: (i, 0)),
            pl.BlockSpec((1, d), lambda i: (0, 0)),
            pl.BlockSpec((None, 1, d), lambda i: (group(i), 0, 0)),
            pl.BlockSpec((None, 1, d), lambda i: (group(i), 0, 0)),
            pl.BlockSpec((d, LANES), lambda i: (0, 0)),
            pl.BlockSpec((1, LANES), lambda i: (0, 0)),
        ],
        out_specs=[pl.BlockSpec((bm, d), lambda i: (i, 0)), row_spec, row_spec, row_spec,
                   pl.BlockSpec((1, LANES), lambda i: (0, 0))],
        out_shape=[jax.ShapeDtypeStruct((t, d), F32), jax.ShapeDtypeStruct((t, LANES), jnp.int32),
                   jax.ShapeDtypeStruct((t, LANES), jnp.int32), jax.ShapeDtypeStruct((t, LANES), F32),
                   jax.ShapeDtypeStruct((1, LANES), F32)],
        scratch_shapes=[pltpu.VMEM((1, LANES), F32)],
        compiler_params=_cparams(("arbitrary",)),
        name="moe_router",
    )(x, gam, sc, sh, w_r, b_r)


def _dispatch_kernel(pe_ref, dest_ref, h_ref, xs_hbm, zbuf, sem, zsem):
    bm = h_ref.shape[0]

    @pl.when(pl.program_id(0) == 0)
    def _():
        zbuf[...] = jnp.zeros_like(zbuf)

        def tail_copy(e):
            start = pl.multiple_of(pe_ref[e + 1] - MOE_ROWS, MOE_ROWS)
            return pltpu.make_async_copy(zbuf, xs_hbm.at[pl.ds(start, MOE_ROWS)], zsem)

        for e in range(N_EXPERTS):
            @pl.when(pe_ref[e + 1] > pe_ref[e])
            def _():
                tail_copy(e).start()
        for e in range(N_EXPERTS):
            @pl.when(pe_ref[e + 1] > pe_ref[e])
            def _():
                tail_copy(e).wait()

        def spare_copy(b):
            return pltpu.make_async_copy(zbuf, xs_hbm.at[pl.ds(pl.multiple_of(b * MOE_ROWS, MOE_ROWS), MOE_ROWS)], zsem)

        first_spare = pe_ref[N_EXPERTS] // MOE_ROWS
        n_blk = xs_hbm.shape[0] // MOE_ROWS
        lax.fori_loop(first_spare, n_blk, lambda b, c: (spare_copy(b).start(), c)[1], 0)
        lax.fori_loop(first_spare, n_blk, lambda b, c: (spare_copy(b).wait(), c)[1], 0)

    def issue(r, carry):
        for k in range(TOP_K):
            pltpu.make_async_copy(h_ref.at[pl.ds(r, 1)], xs_hbm.at[pl.ds(dest_ref[r * TOP_K + k], 1)],
                                  sem).start(priority=k % 2)
        return carry

    lax.fori_loop(0, bm, issue, 0, unroll=2)
    for k in range(TOP_K):
        pltpu.make_async_copy(h_ref, xs_hbm.at[pl.ds(0, bm)], sem).wait()


def _dispatch(pad_bounds, dest, h, n_rows, *, bm):
    t, d = h.shape
    n_tiles = t // bm
    grid_spec = pltpu.PrefetchScalarGridSpec(
        num_scalar_prefetch=1,
        grid=(n_tiles,),
        in_specs=[
            pl.BlockSpec((None, None, bm * TOP_K), lambda i, pe: (i, 0, 0), memory_space=pltpu.SMEM),
            pl.BlockSpec((bm, d), lambda i, pe: (i, 0)),
        ],
        out_specs=pl.BlockSpec(memory_space=pl.ANY),
        scratch_shapes=[pltpu.VMEM((MOE_ROWS, d), F32), pltpu.SemaphoreType.DMA, pltpu.SemaphoreType.DMA],
    )
    return pl.pallas_call(
        _dispatch_kernel,
        grid_spec=grid_spec,
        out_shape=jax.ShapeDtypeStruct((n_rows, d), F32),
        compiler_params=_cparams(("arbitrary",)),
        name="moe_dispatch",
    )(pad_bounds, dest.reshape(n_tiles, 1, bm * TOP_K), h)


def _expert_kernel(be_ref, na_ref, x_ref, wg_ref, wl_ref, bg_ref, bl_ref, wd_ref, bd_ref, o_ref):
    @pl.when(pl.program_id(0) < na_ref[0])
    def _():
        x = x_ref[...].astype(BF16)
        hg = jnp.dot(x, wg_ref[...], preferred_element_type=F32) + bg_ref[...]
        hl = jnp.dot(x, wl_ref[...], preferred_element_type=F32) + bl_ref[...]
        glu = jnp.minimum(hg, SWIGLU_LIMIT)
        lin = jnp.clip(hl, -SWIGLU_LIMIT, SWIGLU_LIMIT)
        act = glu * (1.0 / (1.0 + jnp.exp(-SWIGLU_ALPHA * glu))) * (lin + 1.0)
        o_ref[...] = jnp.dot(act.astype(BF16), wd_ref[...], preferred_element_type=F32) + bd_ref[...]

    @pl.when(pl.program_id(0) >= na_ref[0])
    def _():
        o_ref[...] = jnp.zeros_like(o_ref)


def _experts(blk_expert, n_active, xs, wg, wl, bg, bl, wd, bd, layer):
    n_rows = xs.shape[0]
    d, dff = wg.shape[2], wg.shape[3]
    n_blk = n_rows // MOE_ROWS
    wmap = lambda i, be, na: (layer, be[i], 0, 0)
    rmap = lambda i, be, na: (jnp.minimum(i, na[0] - 1), 0)
    grid_spec = pltpu.PrefetchScalarGridSpec(
        num_scalar_prefetch=2,
        grid=(n_blk,),
        in_specs=[
            pl.BlockSpec((MOE_ROWS, d), rmap),
            pl.BlockSpec((None, None, d, dff), wmap),
            pl.BlockSpec((None, None, d, dff), wmap),
            pl.BlockSpec((None, None, 1, dff), wmap),
            pl.BlockSpec((None, None, 1, dff), wmap),
            pl.BlockSpec((None, None, dff, d), wmap),
            pl.BlockSpec((None, None, 1, d), wmap),
        ],
        out_specs=pl.BlockSpec((MOE_ROWS, d), lambda i, be, na: (i, 0)),
    )
    return pl.pallas_call(
        _expert_kernel,
        grid_spec=grid_spec,
        out_shape=jax.ShapeDtypeStruct((n_rows, d), F32),
        compiler_params=_cparams(("arbitrary",)),
        name="moe_experts",
    )(blk_expert, n_active, xs, wg, wl, bg, bl, wd, bd)


def _combine_kernel(dest_ref, next_ref, x_ref, gate_ref, g_ref, gam_ref, ys_hbm, o_ref, *rest, final_norm, n_ctx_tiles):
    if final_norm:
        of_ref, buf, sem = rest
    else:
        buf, sem = rest
    bm = x_ref.shape[0]
    i = pl.program_id(0)
    slot = i % 2

    def gather(idx_ref, s):
        def issue(r, carry):
            for k in range(TOP_K):
                pltpu.make_async_copy(ys_hbm.at[pl.ds(idx_ref[r * TOP_K + k], 1)], buf.at[s, k, pl.ds(r, 1)],
                                      sem.at[s]).start(priority=k % 2)
            return carry

        lax.fori_loop(0, bm, issue, 0, unroll=2)

    @pl.when(i == 0)
    def _():
        gather(dest_ref, 0)

    @pl.when(i + 1 < pl.num_programs(0))
    def _():
        gather(next_ref, 1 - slot)

    for k in range(TOP_K):
        pltpu.make_async_copy(ys_hbm.at[pl.ds(0, bm)], buf.at[slot, k], sem.at[slot]).wait()
    gates = g_ref[...]
    acc = gates[:, 0:1] * buf[slot, 0]
    for k in range(1, TOP_K):
        acc = acc + gates[:, k:k + 1] * buf[slot, k]
    xn = x_ref[...] + gate_ref[...] * acc
    if not final_norm:
        o_ref[...] = xn
    else:
        ms = jnp.mean(xn * xn, axis=-1, keepdims=True)
        yn = xn * lax.rsqrt(ms + EPS) * gam_ref[...]
        is_ctx = pl.program_id(0) < n_ctx_tiles

        @pl.when(is_ctx)
        def _():
            o_ref[...] = yn

        @pl.when(jnp.logical_not(is_ctx))
        def _():
            of_ref[...] = yn


def _combine(dest, x, gate, gates, gam_final, ys, tok, *, bm, final_norm):
    t, d = x.shape
    group = tok.group_of_tile(bm)
    n_tiles = t // bm
    n_ctx_tiles = tok.t_ctx // bm
    dest_tiles = dest.reshape(n_tiles, 1, bm * TOP_K)
    if final_norm:
        out_shape = [jax.ShapeDtypeStruct((tok.t_ctx, d), F32), jax.ShapeDtypeStruct((tok.t_dec, d), F32)]
        out_specs = [pl.BlockSpec((bm, d), lambda i: (jnp.minimum(i, n_ctx_tiles - 1), 0)),
                     pl.BlockSpec((bm, d), lambda i: (jnp.maximum(i - n_ctx_tiles, 0), 0))]
    else:
        out_shape = [jax.ShapeDtypeStruct((t, d), F32)]
        out_specs = [pl.BlockSpec((bm, d), lambda i: (i, 0))]
    return pl.pallas_call(
        functools.partial(_combine_kernel, final_norm=final_norm, n_ctx_tiles=n_ctx_tiles),
        grid=(n_tiles,),
        in_specs=[
            pl.BlockSpec((None, None, bm * TOP_K), lambda i: (i, 0, 0), memory_space=pltpu.SMEM),
            pl.BlockSpec((None, None, bm * TOP_K), lambda i: (jnp.minimum(i + 1, n_tiles - 1), 0, 0),
                         memory_space=pltpu.SMEM),
            pl.BlockSpec((bm, d), lambda i: (i, 0)),
            pl.BlockSpec((None, 1, d), lambda i: (group(i), 0, 0)),
            pl.BlockSpec((bm, LANES), lambda i: (i, 0)),
            pl.BlockSpec((1, d), lambda i: (0, 0)),
            pl.BlockSpec(memory_space=pl.ANY),
        ],
        out_specs=out_specs,
        out_shape=out_shape,
        scratch_shapes=[pltpu.VMEM((2, TOP_K, bm, d), F32), pltpu.SemaphoreType.DMA((2,))],
        compiler_params=_cparams(("arbitrary",)),
        name="moe_combine_final" if final_norm else "moe_combine",
    )(dest_tiles, dest_tiles, x, gate, gates, gam_final, ys)


def _moe(x, gam, sc, sh, gate, w_r, b_r, wg, wl, bg, bl, wd, bd, layer, gam_final, tok, *, final_norm):
    t, d = x.shape
    h, e_out, r_out, gates, counts = _router(x, gam, sc, sh, w_r, b_r, tok, bm=512)
    counts = counts[0, :N_EXPERTS].astype(jnp.int32)
    padded = (counts + MOE_ROWS - 1) // MOE_ROWS * MOE_ROWS
    pad_end = jnp.cumsum(padded)
    pad_start = pad_end - padded
    e_sel = e_out[:, :TOP_K]
    dest = pad_start[e_sel] + r_out[:, :TOP_K]
    n_rows = t * TOP_K + N_EXPERTS * MOE_ROWS
    n_blk = n_rows // MOE_ROWS
    blk_start = jnp.arange(n_blk, dtype=jnp.int32) * MOE_ROWS
    blk_expert = jnp.minimum(jnp.sum((pad_end[None, :] <= blk_start[:, None]).astype(jnp.int32), axis=1),
                             N_EXPERTS - 1)
    n_active = (pad_end[-1:] // MOE_ROWS).astype(jnp.int32)
    pad_bounds = jnp.concatenate([jnp.zeros((1,), jnp.int32), pad_end.astype(jnp.int32)])
    xs = _dispatch(pad_bounds, dest, h, n_rows, bm=256)
    ys = _experts(blk_expert, n_active, xs, wg, wl, bg, bl, wd, bd, layer)
    return _combine(dest, x, gate, gates, gam_final, ys, tok, bm=256, final_norm=final_norm)


def _rope_tables(tok):
    rows = tok.dec_len // GRID_W
    r = jnp.repeat(jnp.arange(rows), GRID_W).astype(F32)
    cidx = jnp.tile(jnp.arange(GRID_W), rows).astype(F32)
    n_freq = DK_RET // 4
    inv = ROPE_BASE ** (-jnp.arange(n_freq, dtype=F32) / n_freq)
    ang = jnp.concatenate([r[:, None] * inv, cidx[:, None] * inv], axis=-1)
    cos = jnp.concatenate([jnp.ones((tok.t_ctx, DK_RET // 2), F32), jnp.tile(jnp.cos(ang), (tok.n_dec_seq, 1))])
    sin = jnp.concatenate([jnp.zeros((tok.t_ctx, DK_RET // 2), F32), jnp.tile(jnp.sin(ang), (tok.n_dec_seq, 1))])
    return cos, sin


def _group_cols(p):
    h = p.shape[1]
    a = p.reshape(2, SSD_GROUPS, h // SSD_GROUPS).transpose(1, 0, 2).reshape(SSD_GROUPS, -1)
    return jnp.pad(a, ((0, 0), (0, LANES - a.shape[1])))[:, None, :]


def kernel(x_prompt, x_sample, state_ssd, state_ret, c, c_ctx, w_mod, b_mod, norm_mix, norm_ffn, w_in_a, conv_w, conv_b, dt_bias, a_log, d_skip, ssd_norm, w_sp, b_sp, w_out_a, w_in_c, decay_logit, w_out_c, w_router, b_router, w_gu, b_gu, w_down, b_down, norm_final):
    n_ctx, ctx_len, d = x_prompt.shape
    n_dec, dec_len, _ = x_sample.shape
    tok = _Tokens(n_ctx, ctx_len, n_dec, dec_len)
    depth = w_mod.shape[0]
    x = jnp.concatenate([x_prompt.reshape(tok.t_ctx, d), x_sample.reshape(tok.t_dec, d)])

    cvecs = jnp.concatenate([c_ctx[None], c, jnp.zeros((MOD_ROWS - 1 - n_dec, d), F32)])
    mod = _modulation(cvecs, w_mod, b_mod)
    mod = mod.reshape(depth, MOD_ROWS, 6, 1, d).transpose(0, 2, 1, 3, 4)

    h_ssd = a_log.shape[2]
    xbc_w = d + 2 * SSD_GROUPS * D_STATE
    o1, o2, o3 = d, d + xbc_w, d + xbc_w + 2 * h_ssd
    cos, sin = _rope_tables(tok)
    wg_all, wl_all = _split_gate_lin(w_gu)
    bg_all = b_gu[:, :, None, 0::2]
    bl_all = b_gu[:, :, None, 1::2]
    wd_all = w_down.astype(BF16)
    bd_all = b_down[:, :, None, :]
    new_ssd, new_ret = [], []
    y_final = None
    bm_proj = 512

    for l in range(depth):
        sh1, sc1, g1, sh2, sc2, g2 = (mod[l, j] for j in range(6))
        gam_mix = norm_mix[l][None]
        i = l // 2
        if l % 2 == 0:
            w_in = w_in_a[i]
            w_z = w_in[:, :o1].astype(BF16)
            w_xbc = w_in[:, o1:o2].astype(BF16)
            w_dt = w_in[:, o2:o3].reshape(d, 2, SSD_GROUPS, h_ssd // SSD_GROUPS).transpose(0, 2, 1, 3)
            w_dt = jnp.pad(w_dt.reshape(d, SSD_GROUPS, -1), ((0, 0), (0, 0), (0, LANES - 2 * h_ssd // SSD_GROUPS)))
            w_dt = w_dt.reshape(d, SSD_GROUPS * LANES).astype(BF16)
            w_uv = w_in[:, o3:].astype(BF16)
            xbc = _norm_proj(x, gam_mix, sc1, sh1, w_xbc, tok=tok, bm=dec_len, tn=256, out_dtype=BF16,
                             epilogue=_ep_conv_silu, extra=(conv_w[i], conv_b[i][None]),
                             extra_specs=(pl.BlockSpec((4, 256), lambda r, j: (0, j)),
                                          pl.BlockSpec((1, 256), lambda r, j: (0, j))), name="proj_xbc")
            dtb = _group_cols(dt_bias[i]).reshape(1, SSD_GROUPS * LANES)
            sz, dt, u, v = _norm_proj_multi(
                x, gam_mix, sc1, sh1, jnp.concatenate([w_z, w_dt, w_uv], axis=1), tok,
                [(d, 512, BF16, _ep_silu, False),
                 (SSD_GROUPS * LANES, SSD_GROUPS * LANES, F32, _ep_softplus_bias, True),
                 (d, 512, BF16, _ep_gelu, False), (d, d, BF16, _ep_gelu_ln, False)],
                bm=bm_proj, extra=(dtb,), extra_specs=(pl.BlockSpec((1, SSD_GROUPS * LANES), lambda r: (0, 0)),),
                name="proj_zdtuv")
            a_neg = _group_cols(-jnp.exp(a_log[i]))
            dsk = jnp.repeat(d_skip[i], SSD_HEAD).reshape(SSD_GROUPS, 1, GROUP_W)
            nrm = ssd_norm[i].reshape(SSD_GROUPS, 1, GROUP_W)
            s0 = state_ssd[:, i].reshape(n_dec, 2, SSD_GROUPS, HEADS_PER_GROUP, D_STATE, SSD_HEAD)
            s0 = s0.transpose(0, 2, 1, 4, 3, 5).reshape(n_dec, SSD_GROUPS, 2, D_STATE, GROUP_W)
            y_ctx, s_fin = _ssd_scan(xbc, dt, sz, a_neg, dsk, nrm, None, row0=0, n_seq=n_ctx,
                                     seq_len=ctx_len, want_final=True)
            (y_dec,) = _ssd_scan(xbc, dt, sz, a_neg, dsk, nrm, s0, row0=tok.t_ctx, n_seq=n_dec,
                                 seq_len=dec_len, want_final=False)
            s_fin = s_fin.reshape(n_ctx, SSD_GROUPS, 2, D_STATE, HEADS_PER_GROUP, SSD_HEAD)
            new_ssd.append(s_fin.transpose(0, 2, 1, 4, 3, 5).reshape(n_ctx, 2, h_ssd, D_STATE, SSD_HEAD))
            b_full = jnp.repeat(b_sp[i].T, LANES, axis=1)
            w_o = w_out_a[i].astype(BF16)
            x = _out_proj_a(x, g1, y_ctx, y_dec, u, v, w_sp[i].astype(BF16), b_full, w_o[:d], w_o[d:], tok, bm=512)
        else:
            hk = H_RET * DK_RET
            hv = H_RET * DV_RET
            w_in = w_in_c[i].astype(BF16)
            rope_specs = (pl.BlockSpec((bm_proj, DK_RET // 2), lambda r: (r, 0)),) * 2
            q, kk, vv, sg = _norm_proj_multi(
                x, gam_mix, sc1, sh1, w_in, tok,
                [(hk, 512, BF16, functools.partial(_ep_rope, scale=1.0), True),
                 (hk, 512, BF16, functools.partial(_ep_rope, scale=DK_RET ** -0.5), True),
                 (hv, 512, BF16, _ep_plain, False), (hv, 512, BF16, _ep_silu, False)],
                bm=bm_proj, extra=(cos, sin), extra_specs=rope_specs, name="proj_qkvg")
            log_g = jax.nn.log_sigmoid(decay_logit[i].astype(F32))
            y_ctx, s_fin = _ret_scan(log_g, q, kk, vv, sg, None, row0=0, n_seq=n_ctx, seq_len=ctx_len, want_final=True)
            (y_dec,) = _ret_scan(log_g, q, kk, vv, sg, state_ret[:, i], row0=tok.t_ctx, n_seq=n_dec, seq_len=dec_len,
                                 want_final=False)
            new_ret.append(s_fin)
            x = _out_proj_c(x, g1, y_ctx, y_dec, w_out_c[i].astype(BF16), tok, bm=512)

        w_r = jnp.pad(w_router[l], ((0, 0), (0, LANES - N_EXPERTS)))
        b_r = jnp.pad(b_router[l], (0, LANES - N_EXPERTS), constant_values=-1e30)[None]
        last = l == depth - 1
        res = _moe(x, norm_ffn[l][None], sc2, sh2, g2, w_r, b_r, wg_all, wl_all, bg_all, bl_all, wd_all, bd_all, l,
                   norm_final[None], tok, final_norm=last)
        if last:
            y_final = res
        else:
            x = res[0]

    y_prompt = y_final[0].reshape(n_ctx, ctx_len, d)
    y_sample = y_final[1].reshape(n_dec, dec_len, d)
    return (y_prompt, y_sample, jnp.stack(new_ssd, axis=1), jnp.stack(new_ret, axis=1))
```

```python
import functools
import math

import jax
import jax.numpy as jnp
from jax import lax
from jax.experimental import pallas as pl
from jax.experimental.pallas import tpu as pltpu

D_MODEL = 1024
GRID_W = 64
CHUNK = 128
SSD_HEAD = 64
SSD_GROUPS = 2
D_STATE = 128
GROUP_W = D_MODEL // SSD_GROUPS
HEADS_PER_GROUP = GROUP_W // SSD_HEAD
SG_GROUPS = 8
H_RET = 4
DK_RET = D_MODEL // H_RET
DV_RET = 2 * DK_RET
ROPE_BASE = 10000.0
N_EXPERTS = 32
TOP_K = 4
SWIGLU_LIMIT = 7.0
SWIGLU_ALPHA = 1.702
EPS = 1e-6

LANES = 128
MOD_ROWS = 8
MOE_ROWS = 512
VMEM_LIMIT = 56 * 1024 * 1024

F32 = jnp.float32
BF16 = jnp.bfloat16
HI = lax.Precision.HIGHEST


def _cparams(sem):
    return pltpu.CompilerParams(dimension_semantics=sem, vmem_limit_bytes=VMEM_LIMIT)


def _silu(x):
    return x * (1.0 / (1.0 + jnp.exp(-x)))


def _gelu_tanh(x):
    return 0.5 * x * (1.0 + jnp.tanh(math.sqrt(2.0 / math.pi) * (x + 0.044715 * (x * x * x))))


def _softplus(x):
    return jnp.maximum(x, 0.0) + jnp.log(1.0 + jnp.exp(-jnp.abs(x)))


def _mod_kernel(c_ref, w_ref, b_ref, o_ref):
    a = _silu(c_ref[...])
    o_ref[...] = jnp.dot(a, w_ref[...], precision=HI, preferred_element_type=F32) + b_ref[...]


def _modulation(cvecs, w_mod, b_mod):
    depth, d, n = w_mod.shape
    tn = 1536
    return pl.pallas_call(
        _mod_kernel,
        grid=(depth, n // tn),
        in_specs=[
            pl.BlockSpec((MOD_ROWS, d), lambda l, j: (0, 0)),
            pl.BlockSpec((None, d, tn), lambda l, j: (l, 0, j)),
            pl.BlockSpec((None, 1, tn), lambda l, j: (l, 0, j)),
        ],
        out_specs=pl.BlockSpec((None, MOD_ROWS, tn), lambda l, j: (l, 0, j)),
        out_shape=jax.ShapeDtypeStruct((depth, MOD_ROWS, n), F32),
        compiler_params=_cparams(("arbitrary", "arbitrary")),
        name="modulation",
    )(cvecs, w_mod, b_mod.reshape(depth, 1, n))


class _Tokens:
    def __init__(self, n_ctx_seq, ctx_len, n_dec_seq, dec_len):
        self.n_ctx_seq, self.ctx_len = n_ctx_seq, ctx_len
        self.n_dec_seq, self.dec_len = n_dec_seq, dec_len
        self.t_ctx = n_ctx_seq * ctx_len
        self.t_dec = n_dec_seq * dec_len
        self.total = self.t_ctx + self.t_dec

    def group_of_tile(self, bm):
        assert self.t_ctx % bm == 0 and self.dec_len % bm == 0
        n_ctx_tiles = self.t_ctx // bm
        per_seq = self.dec_len // bm

        def group(i):
            return jnp.where(i < n_ctx_tiles, 0, 1 + (i - n_ctx_tiles) // per_seq)

        return group


def _modulated_norm(x, gam, sc, sh):
    ms = jnp.mean(x * x, axis=-1, keepdims=True)
    return (x * lax.rsqrt(ms + EPS) * gam) * (1.0 + sc) + sh


def _proj_kernel(*refs, epilogue, n_extra, tok, bm):
    x_ref, gam_ref, sc_ref, sh_ref, w_ref = refs[:5]
    extra = refs[5:5 + n_extra]
    o_ref = refs[5 + n_extra]
    h_scr = refs[6 + n_extra]
    i = pl.program_id(0)

    @pl.when(pl.program_id(1) == 0)
    def _():
        h_scr[...] = _modulated_norm(x_ref[...], gam_ref[...], sc_ref[...], sh_ref[...]).astype(BF16)

    acc = jnp.dot(h_scr[...], w_ref[...], preferred_element_type=F32)
    o_ref[...] = epilogue(acc, i, tok, bm, *extra).astype(o_ref.dtype)


def _ep_plain(acc, i, tok, bm):
    return acc


def _ep_silu(acc, i, tok, bm):
    return _silu(acc)


def _ep_gelu(acc, i, tok, bm):
    return _gelu_tanh(acc)


def _ep_softplus_bias(acc, i, tok, bm, bias_ref):
    return _softplus(acc + bias_ref[...])


def _ep_gelu_ln(acc, i, tok, bm):
    g = _gelu_tanh(acc)
    mu = jnp.mean(g, axis=-1, keepdims=True)
    gc = g - mu
    return gc * lax.rsqrt(jnp.mean(gc * gc, axis=-1, keepdims=True) + 1e-5)


def _ep_conv_silu(acc, i, tok, bm, cw_ref, cb_ref):
    n = acc.shape[0]
    seq = jnp.where(i * bm < tok.t_ctx, tok.ctx_len, tok.dec_len)
    t = lax.broadcasted_iota(jnp.int32, (n, 1), 0) & (seq - 1)
    cw = cw_ref[...]
    y = acc * cw[2:3, :] + cb_ref[...]
    y = y + jnp.where(t >= 2, pltpu.roll(acc, 2, axis=0), 0.0) * cw[0:1, :]
    y = y + jnp.where(t >= 1, pltpu.roll(acc, 1, axis=0), 0.0) * cw[1:2, :]
    y = y + jnp.where(t < seq - 1, pltpu.roll(acc, n - 1, axis=0), 0.0) * cw[3:4, :]
    return _silu(y)


def _ep_rope(acc, i, tok, bm, cos_ref, sin_ref, *, scale):
    cs, sn = cos_ref[...], sin_ref[...]
    half = DK_RET // 2
    outs = []
    for h in range(acc.shape[1] // DK_RET):
        x1 = acc[:, h * DK_RET:h * DK_RET + half] * scale
        x2 = acc[:, h * DK_RET + half:(h + 1) * DK_RET] * scale
        outs.append(x1 * cs - x2 * sn)
        outs.append(x2 * cs + x1 * sn)
    return jnp.concatenate(outs, axis=1)


def _norm_proj(x, gam, sc, sh, w, tok, *, bm, tn, out_dtype, epilogue, extra=(), extra_specs=(), name):
    t, d = x.shape
    n = w.shape[1]
    group = tok.group_of_tile(bm)
    kern = functools.partial(_proj_kernel, epilogue=epilogue, n_extra=len(extra), tok=tok, bm=bm)
    return pl.pallas_call(
        kern,
        grid=(t // bm, n // tn),
        in_specs=[
            pl.BlockSpec((bm, d), lambda i, j: (i, 0)),
            pl.BlockSpec((1, d), lambda i, j: (0, 0)),
            pl.BlockSpec((None, 1, d), lambda i, j: (group(i), 0, 0)),
            pl.BlockSpec((None, 1, d), lambda i, j: (group(i), 0, 0)),
            pl.BlockSpec((d, tn), lambda i, j: (0, j)),
            *extra_specs,
        ],
        out_specs=pl.BlockSpec((bm, tn), lambda i, j: (i, j)),
        out_shape=jax.ShapeDtypeStruct((t, n), out_dtype),
        scratch_shapes=[pltpu.VMEM((bm, d), BF16)],
        compiler_params=_cparams(("arbitrary", "arbitrary")),
        name=name,
    )(x, gam, sc, sh, w, *extra)


def _proj_multi_kernel(*refs, pieces, n_extra, tok, bm):
    x_ref, gam_ref, sc_ref, sh_ref, w_ref = refs[:5]
    extra = refs[5:5 + n_extra]
    outs = refs[5 + n_extra:5 + n_extra + len(pieces)]
    h_scr = refs[5 + n_extra + len(pieces)]
    i = pl.program_id(0)
    h_scr[...] = _modulated_norm(x_ref[...], gam_ref[...], sc_ref[...], sh_ref[...]).astype(BF16)
    c0 = 0
    for (width, chunk, _, epilogue, uses_extra), o_ref in zip(pieces, outs):
        for j in range(width // chunk):
            acc = jnp.dot(h_scr[...], w_ref[:, c0 + j * chunk:c0 + (j + 1) * chunk], preferred_element_type=F32)
            res = epilogue(acc, i, tok, bm, *(extra if uses_extra else ()))
            o_ref[:, j * chunk:(j + 1) * chunk] = res.astype(o_ref.dtype)
        c0 += width


def _norm_proj_multi(x, gam, sc, sh, w, tok, pieces, *, bm, extra=(), extra_specs=(), name):
    t, d = x.shape
    n = w.shape[1]
    assert n == sum(p[0] for p in pieces)
    group = tok.group_of_tile(bm)
    kern = functools.partial(_proj_multi_kernel, pieces=tuple(pieces), n_extra=len(extra), tok=tok, bm=bm)
    return pl.pallas_call(
        kern,
        grid=(t // bm,),
        in_specs=[
            pl.BlockSpec((bm, d), lambda i: (i, 0)),
            pl.BlockSpec((1, d), lambda i: (0, 0)),
            pl.BlockSpec((None, 1, d), lambda i: (group(i), 0, 0)),
            pl.BlockSpec((None, 1, d), lambda i: (group(i), 0, 0)),
            pl.BlockSpec((d, n), lambda i: (0, 0), pipeline_mode=pl.Buffered(1)),
            *extra_specs,
        ],
        out_specs=[pl.BlockSpec((bm, p[0]), lambda i: (i, 0)) for p in pieces],
        out_shape=[jax.ShapeDtypeStruct((t, p[0]), p[2]) for p in pieces],
        scratch_shapes=[pltpu.VMEM((bm, d), BF16)],
        compiler_params=_cparams(("arbitrary",)),
        name=name,
    )(x, gam, sc, sh, w, *extra)


def _ssd_kernel(*refs, n_chunks, has_init, has_final):
    xs_ref, b_ref, c_ref, dt_ref, sz_ref, a_ref, dsk_ref, nrm_ref = refs[:8]
    k = 8
    s0_ref = None
    if has_init:
        s0_ref = refs[k]
        k += 1
    y_ref = refs[k]
    k += 1
    sfin_ref = None
    if has_final:
        sfin_ref = refs[k]
        k += 1
    y_scrs = (refs[k], refs[k + 1])
    s_scrs = (refs[k + 2], refs[k + 3])

    q = CHUNK
    row = lax.broadcasted_iota(jnp.int32, (q, q), 0)
    col = lax.broadcasted_iota(jnp.int32, (q, q), 1)
    lane = lax.broadcasted_iota(jnp.int32, (1, q), 1)
    left = lane < SSD_HEAD
    a_neg = a_ref[...]
    n_pairs = HEADS_PER_GROUP // 2
    keeps = (col <= row, col >= row)
    tris = tuple(kp.astype(F32).astype(BF16) for kp in keeps)

    def masked_sums(tri, x):
        hi = x.astype(BF16)
        r1 = x - hi.astype(F32)
        mid = r1.astype(BF16)
        lo = (r1 - mid.astype(F32)).astype(BF16)
        return (jnp.dot(tri, hi, preferred_element_type=F32) + jnp.dot(tri, mid, preferred_element_type=F32)
                + jnp.dot(tri, lo, preferred_element_type=F32))

    def chunk(ci, direction):
        s_scr = s_scrs[direction]
        r0 = pl.multiple_of(ci * q, q)
        xs = xs_ref[pl.ds(r0, q), :]
        bm_ = b_ref[pl.ds(r0, q), :]
        cm = c_ref[pl.ds(r0, q), :]
        dt = dt_ref[pl.ds(r0, q), :]
        la = dt * a_neg
        keep, tri = keeps[direction], tris[direction]
        last = q - 1 if direction == 0 else 0
        cum = masked_sums(tri, la)
        cum_t = cum.T
        dt_t = dt.T
        tot_t = jnp.broadcast_to(cum_t[:, last:last + 1], (q, q))
        w_t = dt_t * jnp.exp(tot_t - cum_t)
        g = lax.dot_general(cm, bm_, (((1,), (1,)), ((), ())), preferred_element_type=F32)
        b_t = bm_.astype(F32).T
        cm_f = cm.astype(F32)
        outs = []
        for p in range(n_pairs):
            xs_p = xs[:, p * LANES:(p + 1) * LANES]
            s_p = s_scr[:, p * LANES:(p + 1) * LANES]
            s_b = s_p.astype(BF16)
            zero = jnp.zeros_like(xs_p)
            zero_s = jnp.zeros_like(s_b)
            lhs, rhs, lhs_s, rhs_s, decs = [], [], [], [], []
            for hh in range(2):
                cidx = direction * HEADS_PER_GROUP + 2 * p + hh
                cum_b = jnp.broadcast_to(cum[:, cidx:cidx + 1], (q, q))
                dec = jnp.exp(jnp.where(keep, cum_b - cum_t[cidx:cidx + 1, :], -jnp.inf))
                scores = g * dec * dt_t[cidx:cidx + 1, :]
                lhs += [scores.astype(BF16), (cm_f * jnp.exp(cum_b)).astype(BF16)]
                sel = left if hh == 0 else jnp.logical_not(left)
                rhs += [jnp.where(sel, xs_p, zero), jnp.where(sel, s_b, zero_s)]
                lhs_s.append((b_t * w_t[cidx:cidx + 1, :]).astype(BF16))
                rhs_s.append(jnp.where(sel, xs_p, zero))
                decs.append(jnp.exp(cum_t[cidx:cidx + 1, last:last + 1]))
            y_p = jnp.dot(jnp.concatenate(lhs, axis=1), jnp.concatenate(rhs, axis=0),
                          preferred_element_type=F32)
            upd = jnp.dot(jnp.concatenate(lhs_s, axis=1), jnp.concatenate(rhs_s, axis=0),
                          preferred_element_type=F32)
            s_scr[:, p * LANES:(p + 1) * LANES] = s_p * jnp.where(left, decs[0], decs[1]) + upd
            outs.append(y_p)
        return jnp.concatenate(outs, axis=1)

    for direction in range(2):
        if has_init:
            s_scrs[direction][...] = s0_ref[direction]
        else:
            s_scrs[direction][...] = jnp.zeros_like(s_scrs[direction])

    def scan_body(j, carry):
        for direction, ci in ((0, j), (1, n_chunks - 1 - j)):
            r0 = pl.multiple_of(ci * q, q)
            y_scrs[direction][pl.ds(r0, q), :] = chunk(ci, direction)
        return carry

    lax.fori_loop(0, n_chunks, scan_body, 0)
    if has_final:
        for direction in range(2):
            sfin_ref[direction] = s_scrs[direction][...]

    def finish_body(ci, carry):
        r0 = pl.multiple_of(ci * q, q)
        y = y_scrs[0][pl.ds(r0, q), :] + y_scrs[1][pl.ds(r0, q), :]
        y = y + dsk_ref[...] * xs_ref[pl.ds(r0, q), :].astype(F32)
        y = y * sz_ref[pl.ds(r0, q), :].astype(F32)
        y = y * lax.rsqrt(jnp.mean(y * y, axis=-1, keepdims=True) + EPS)
        y_ref[pl.ds(r0, q), :] = (y * nrm_ref[...]).astype(y_ref.dtype)
        return carry

    lax.fori_loop(0, n_chunks, finish_body, 0)


def _ssd_scan(xbc, dt, sz, a_neg, dsk, nrm, s0, *, row0, n_seq, seq_len, want_final):
    assert row0 % seq_len == 0
    b0 = row0 // seq_len
    has_init = s0 is not None
    gw = GROUP_W
    b_blk0 = D_MODEL // D_STATE
    c_blk0 = b_blk0 + SSD_GROUPS
    in_specs = [
        pl.BlockSpec((seq_len, gw), lambda b, g: (b0 + b, g)),
        pl.BlockSpec((seq_len, D_STATE), lambda b, g: (b0 + b, b_blk0 + g)),
        pl.BlockSpec((seq_len, D_STATE), lambda b, g: (b0 + b, c_blk0 + g)),
        pl.BlockSpec((seq_len, LANES), lambda b, g: (b0 + b, g)),
        pl.BlockSpec((seq_len, gw), lambda b, g: (b0 + b, g)),
        pl.BlockSpec((None, 1, LANES), lambda b, g: (g, 0, 0)),
        pl.BlockSpec((None, 1, gw), lambda b, g: (g, 0, 0)),
        pl.BlockSpec((None, 1, gw), lambda b, g: (g, 0, 0)),
    ]
    args = [xbc, xbc, xbc, dt, sz, a_neg, dsk, nrm]
    if has_init:
        in_specs.append(pl.BlockSpec((None, None, 2, D_STATE, gw), lambda b, g: (b, g, 0, 0, 0)))
        args.append(s0)
    out_shape = [jax.ShapeDtypeStruct((n_seq * seq_len, D_MODEL), BF16)]
    out_specs = [pl.BlockSpec((seq_len, gw), lambda b, g: (b, g))]
    if want_final:
        out_shape.append(jax.ShapeDtypeStruct((n_seq, SSD_GROUPS, 2, D_STATE, gw), F32))
        out_specs.append(pl.BlockSpec((None, None, 2, D_STATE, gw), lambda b, g: (b, g, 0, 0, 0)))
    kern = functools.partial(_ssd_kernel, n_chunks=seq_len // CHUNK, has_init=has_init, has_final=want_final)
    return pl.pallas_call(
        kern,
        grid=(n_seq, SSD_GROUPS),
        in_specs=in_specs,
        out_specs=out_specs,
        out_shape=out_shape,
        scratch_shapes=[pltpu.VMEM((seq_len, gw), F32), pltpu.VMEM((seq_len, gw), F32),
                        pltpu.VMEM((D_STATE, gw), F32), pltpu.VMEM((D_STATE, gw), F32)],
        compiler_params=_cparams(("arbitrary", "arbitrary")),
        name="ssd_scan_ctx" if want_final else "ssd_scan_dec",
    )(*args)


def _ret_kernel(*refs, n_chunks, has_init, has_final):
    lg_ref, q_ref, k_ref, v_ref, sg_ref = refs[:5]
    k = 5
    s0_ref = None
    if has_init:
        s0_ref = refs[k]
        k += 1
    y_ref = refs[k]
    k += 1
    sfin_ref = None
    if has_final:
        sfin_ref = refs[k]
        k += 1
    y_scrs = (refs[k], refs[k + 1])
    s_scrs = (refs[k + 2], refs[k + 3])

    qn = CHUNK
    h = pl.program_id(1)
    row = lax.broadcasted_iota(jnp.int32, (qn, qn), 0)
    col = lax.broadcasted_iota(jnp.int32, (qn, qn), 1)
    rowk = lax.broadcasted_iota(jnp.int32, (qn, DK_RET), 0).astype(F32)

    def tables(direction):
        lg = lg_ref[direction, h]
        if direction == 0:
            keep = col <= row
            dist = (row - col).astype(F32)
            e_q = jnp.exp(lg * (rowk + 1.0))
            w_k = jnp.exp(lg * (qn - 1.0 - rowk))
        else:
            keep = col >= row
            dist = (col - row).astype(F32)
            e_q = jnp.exp(lg * (qn - rowk))
            w_k = jnp.exp(lg * rowk)
        dmat = jnp.where(keep, jnp.exp(lg * dist), 0.0)
        return dmat, e_q, w_k, jnp.exp(jnp.full((1, 1), float(qn), F32) * lg)

    def chunk(ci, direction, tabs):
        dmat, e_q, w_k, dec = tabs
        s_scr = s_scrs[direction]
        r0 = pl.multiple_of(ci * qn, qn)
        qc = q_ref[pl.ds(r0, qn), :]
        kc = k_ref[pl.ds(r0, qn), :]
        vc = v_ref[pl.ds(r0, qn), :]
        scores = lax.dot_general(qc, kc, (((1,), (1,)), ((), ())), preferred_element_type=F32) * dmat
        s_old = s_scr[...]
        lhs = jnp.concatenate([scores.astype(BF16), (qc.astype(F32) * e_q).astype(BF16)], axis=1)
        rhs = jnp.concatenate([vc, s_old.astype(BF16)], axis=0)
        y = jnp.dot(lhs, rhs, preferred_element_type=F32)
        kw_t = (kc.astype(F32) * w_k).T.astype(BF16)
        s_scr[...] = s_old * dec + jnp.dot(kw_t, vc, preferred_element_type=F32)
        return y

    for direction in range(2):
        if has_init:
            s_scrs[direction][...] = s0_ref[direction]
        else:
            s_scrs[direction][...] = jnp.zeros_like(s_scrs[direction])
    tabs = (tables(0), tables(1))

    def scan_body(j, carry):
        for direction, ci in ((0, j), (1, n_chunks - 1 - j)):
            r0 = pl.multiple_of(ci * qn, qn)
            y_scrs[direction][pl.ds(r0, qn), :] = chunk(ci, direction, tabs[direction])
        return carry

    lax.fori_loop(0, n_chunks, scan_body, 0)
    if has_final:
        for direction in range(2):
            sfin_ref[direction] = s_scrs[direction][...]

    def finish_body(ci, carry):
        r0 = pl.multiple_of(ci * qn, qn)
        y = y_scrs[0][pl.ds(r0, qn), :] + y_scrs[1][pl.ds(r0, qn), :]
        y = y * lax.rsqrt(jnp.mean(y * y, axis=-1, keepdims=True) + EPS)
        y_ref[pl.ds(r0, qn), :] = (y * sg_ref[pl.ds(r0, qn), :].astype(F32)).astype(y_ref.dtype)
        return carry

    lax.fori_loop(0, n_chunks, finish_body, 0)


def _ret_scan(log_g, q, k, v, sg, s0, *, row0, n_seq, seq_len, want_final):
    assert row0 % seq_len == 0
    b0 = row0 // seq_len
    has_init = s0 is not None
    in_specs = [
        pl.BlockSpec(memory_space=pltpu.SMEM),
        pl.BlockSpec((seq_len, DK_RET), lambda b, h: (b0 + b, h)),
        pl.BlockSpec((seq_len, DK_RET), lambda b, h: (b0 + b, h)),
        pl.BlockSpec((seq_len, DV_RET), lambda b, h: (b0 + b, h)),
        pl.BlockSpec((seq_len, DV_RET), lambda b, h: (b0 + b, h)),
    ]
    args = [log_g, q, k, v, sg]
    if has_init:
        in_specs.append(pl.BlockSpec((None, 2, None, DK_RET, DV_RET), lambda b, h: (b, 0, h, 0, 0)))
        args.append(s0)
    out_shape = [jax.ShapeDtypeStruct((n_seq * seq_len, H_RET * DV_RET), BF16)]
    out_specs = [pl.BlockSpec((seq_len, DV_RET), lambda b, h: (b, h))]
    if want_final:
        out_shape.append(jax.ShapeDtypeStruct((n_seq, 2, H_RET, DK_RET, DV_RET), F32))
        out_specs.append(pl.BlockSpec((None, 2, None, DK_RET, DV_RET), lambda b, h: (b, 0, h, 0, 0)))
    kern = functools.partial(_ret_kernel, n_chunks=seq_len // CHUNK, has_init=has_init, has_final=want_final)
    return pl.pallas_call(
        kern,
        grid=(n_seq, H_RET),
        in_specs=in_specs,
        out_specs=out_specs,
        out_shape=out_shape,
        scratch_shapes=[pltpu.VMEM((seq_len, DV_RET), F32), pltpu.VMEM((seq_len, DV_RET), F32),
                        pltpu.VMEM((DK_RET, DV_RET), F32), pltpu.VMEM((DK_RET, DV_RET), F32)],
        compiler_params=_cparams(("arbitrary", "arbitrary")),
        name="ret_scan_ctx" if want_final else "ret_scan_dec",
    )(*args)


def _mixer_rows(tok, bm):
    n_ctx_tiles = tok.t_ctx // bm

    def specs(width):
        return [pl.BlockSpec((bm, width), lambda i: (jnp.minimum(i, n_ctx_tiles - 1), 0)),
                pl.BlockSpec((bm, width), lambda i: (jnp.maximum(i - n_ctx_tiles, 0), 0))]

    def select(yc_ref, yd_ref):
        return jnp.where(pl.program_id(0) < n_ctx_tiles, yc_ref[...], yd_ref[...])

    return specs, select


def _out_a_kernel(x_ref, gate_ref, yc_ref, yd_ref, u_ref, v_ref, wsp_ref, bsp_ref, w1_ref, w2_ref, o_ref, sgu_scr,
                  *, select):
    bm = x_ref.shape[0]
    for ci in range(bm // CHUNK):
        rows = slice(ci * CHUNK, (ci + 1) * CHUNK)
        for g in range(SG_GROUPS):
            cols = slice(g * LANES, (g + 1) * LANES)
            mix = jnp.dot(wsp_ref[g], v_ref[rows, cols], preferred_element_type=F32) + bsp_ref[:, cols]
            sgu_scr[rows, cols] = (u_ref[rows, cols].astype(F32) * mix).astype(BF16)
    out = jnp.dot(select(yc_ref, yd_ref), w1_ref[...], preferred_element_type=F32)
    out = out + jnp.dot(sgu_scr[...], w2_ref[...], preferred_element_type=F32)
    o_ref[...] = x_ref[...] + gate_ref[...] * out


def _out_proj_a(x, gate, y_ctx, y_dec, u, v, w_sp, b_full, w1, w2, tok, *, bm):
    t, d = x.shape
    group = tok.group_of_tile(bm)
    y_specs, select = _mixer_rows(tok, bm)
    return pl.pallas_call(
        functools.partial(_out_a_kernel, select=select),
        grid=(t // bm,),
        in_specs=[
            pl.BlockSpec((bm, d), lambda i: (i, 0)),
            pl.BlockSpec((None, 1, d), lambda i: (group(i), 0, 0)),
            *y_specs(d),
            pl.BlockSpec((bm, d), lambda i: (i, 0)),
            pl.BlockSpec((bm, d), lambda i: (i, 0)),
            pl.BlockSpec((SG_GROUPS, CHUNK, CHUNK), lambda i: (0, 0, 0)),
            pl.BlockSpec((CHUNK, d), lambda i: (0, 0)),
            pl.BlockSpec((d, d), lambda i: (0, 0)),
            pl.BlockSpec((d, d), lambda i: (0, 0)),
        ],
        out_specs=pl.BlockSpec((bm, d), lambda i: (i, 0)),
        out_shape=jax.ShapeDtypeStruct((t, d), F32),
        scratch_shapes=[pltpu.VMEM((bm, d), BF16)],
        compiler_params=_cparams(("arbitrary",)),
        name="out_proj_a",
    )(x, gate, y_ctx, y_dec, u, v, w_sp, b_full, w1, w2)


def _out_c_kernel(x_ref, gate_ref, yc_ref, yd_ref, w_ref, o_ref, *, select):
    out = jnp.dot(select(yc_ref, yd_ref), w_ref[...], preferred_element_type=F32)
    o_ref[...] = x_ref[...] + gate_ref[...] * out


def _out_proj_c(x, gate, y_ctx, y_dec, w, tok, *, bm):
    t, d = x.shape
    kdim = y_ctx.shape[1]
    group = tok.group_of_tile(bm)
    y_specs, select = _mixer_rows(tok, bm)
    return pl.pallas_call(
        functools.partial(_out_c_kernel, select=select),
        grid=(t // bm,),
        in_specs=[
            pl.BlockSpec((bm, d), lambda i: (i, 0)),
            pl.BlockSpec((None, 1, d), lambda i: (group(i), 0, 0)),
            *y_specs(kdim),
            pl.BlockSpec((kdim, d), lambda i: (0, 0)),
        ],
        out_specs=pl.BlockSpec((bm, d), lambda i: (i, 0)),
        out_shape=jax.ShapeDtypeStruct((t, d), F32),
        compiler_params=_cparams(("arbitrary",)),
        name="out_proj_c",
    )(x, gate, y_ctx, y_dec, w)


def _split_kernel(w_ref, g_ref, l_ref):
    w = w_ref[...]
    k, n2 = w.shape
    half = LANES // 2
    lane = lax.broadcasted_iota(jnp.int32, (k, LANES), 1)
    first = lane < half
    idx = jnp.where(first, 2 * lane, 2 * (lane - half) + 1)
    gs, ls = [], []
    for j in range(n2 // (2 * LANES)):
        a = jnp.take_along_axis(w[:, (2 * j) * LANES:(2 * j + 1) * LANES], idx, axis=1)
        b = jnp.take_along_axis(w[:, (2 * j + 1) * LANES:(2 * j + 2) * LANES], idx, axis=1)
        gs.append(jnp.where(first, a, pltpu.roll(b, half, axis=1)))
        ls.append(jnp.where(first, pltpu.roll(a, half, axis=1), b))
    g_ref[...] = jnp.concatenate(gs, axis=1).astype(BF16)
    l_ref[...] = jnp.concatenate(ls, axis=1).astype(BF16)


def _split_gate_lin(w_gu):
    dl, e, k, n2 = w_gu.shape
    tn = 512
    spec_out = pl.BlockSpec((None, None, k, tn), lambda a, b, j: (a, b, 0, j))
    return pl.pallas_call(
        _split_kernel,
        grid=(dl, e, n2 // (2 * tn)),
        in_specs=[pl.BlockSpec((None, None, k, 2 * tn), lambda a, b, j: (a, b, 0, j))],
        out_specs=[spec_out, spec_out],
        out_shape=[jax.ShapeDtypeStruct((dl, e, k, n2 // 2), BF16)] * 2,
        compiler_params=_cparams(("arbitrary",) * 3),
        name="split_gate_lin",
    )(w_gu)


def _router_kernel(x_ref, gam_ref, sc_ref, sh_ref, wr_ref, br_ref, h_ref, e_ref, r_ref, g_ref, cnt_ref, cnt_scr):
    i = pl.program_id(0)
    bm = x_ref.shape[0]

    @pl.when(i == 0)
    def _():
        cnt_scr[...] = jnp.zeros_like(cnt_scr)

    h = _modulated_norm(x_ref[...], gam_ref[...], sc_ref[...], sh_ref[...])
    h_ref[...] = h
    logits = jnp.dot(h, wr_ref[...], precision=HI, preferred_element_type=F32) + br_ref[...]
    lane = lax.broadcasted_iota(jnp.int32, logits.shape, 1).astype(F32)
    vals, idxs = [], []
    work = logits
    for _ in range(TOP_K):
        m = jnp.max(work, axis=-1, keepdims=True)
        idx = jnp.min(jnp.where(work == m, lane, float(LANES)), axis=-1, keepdims=True)
        vals.append(m)
        idxs.append(idx)
        work = jnp.where(lane == idx, -jnp.inf, work)
    exps = [jnp.exp(v - vals[0]) for v in vals]
    inv = 1.0 / functools.reduce(lambda a, b: a + b, exps)
    hot = functools.reduce(jnp.logical_or, [lane == idx for idx in idxs])
    hot_f = hot.astype(F32)
    ri = lax.broadcasted_iota(jnp.int32, (bm, bm), 0)
    ci = lax.broadcasted_iota(jnp.int32, (bm, bm), 1)
    before = (ci < ri).astype(BF16)
    rank_all = cnt_scr[...] + jnp.dot(before, hot_f.astype(BF16), preferred_element_type=F32)
    e_out = jnp.zeros(logits.shape, F32)
    r_out = jnp.zeros(logits.shape, F32)
    g_out = jnp.zeros(logits.shape, F32)
    for k in range(TOP_K):
        rk = jnp.sum(jnp.where(lane == idxs[k], rank_all, 0.0), axis=-1, keepdims=True)
        e_out = jnp.where(lane == float(k), idxs[k], e_out)
        r_out = jnp.where(lane == float(k), rk, r_out)
        g_out = jnp.where(lane == float(k), exps[k] * inv, g_out)
    e_ref[...] = e_out.astype(jnp.int32)
    r_ref[...] = r_out.astype(jnp.int32)
    g_ref[...] = g_out
    cnt_scr[...] = cnt_scr[...] + jnp.sum(hot_f, axis=0, keepdims=True)
    cnt_ref[...] = cnt_scr[...]


def _router(x, gam, sc, sh, w_r, b_r, tok, *, bm):
    t, d = x.shape
    group = tok.group_of_tile(bm)
    row_spec = pl.BlockSpec((bm, LANES), lambda i: (i, 0))
    return pl.pallas_call(
        _router_kernel,
        grid=(t // bm,),
        in_specs=[
            pl.BlockSpec((bm, d), lambda i: (i, 0)),
            pl.BlockSpec((1, d), lambda i: (0, 0)),
            pl.BlockSpec((None, 1, d), lambda i: (group(i), 0, 0)),
            pl.BlockSpec((None, 1, d), lambda i: (group(i), 0, 0)),
            pl.BlockSpec((d, LANES), lambda i: (0, 0)),
            pl.BlockSpec((1, LANES), lambda i: (0, 0)),
        ],
        out_specs=[pl.BlockSpec((bm, d), lambda i: (i, 0)), row_spec, row_spec, row_spec,
                   pl.BlockSpec((1, LANES), lambda i: (0, 0))],
        out_shape=[jax.ShapeDtypeStruct((t, d), F32), jax.ShapeDtypeStruct((t, LANES), jnp.int32),
                   jax.ShapeDtypeStruct((t, LANES), jnp.int32), jax.ShapeDtypeStruct((t, LANES), F32),
                   jax.ShapeDtypeStruct((1, LANES), F32)],
        scratch_shapes=[pltpu.VMEM((1, LANES), F32)],
        compiler_params=_cparams(("arbitrary",)),
        name="moe_router",
    )(x, gam, sc, sh, w_r, b_r)


SLABS = D_MODEL // LANES


def _to_row_tiles(ref, x):
    for s in range(SLABS):
        ref[:, s, :] = x[:, s * LANES:(s + 1) * LANES]


def _from_row_tiles(ref):
    return jnp.concatenate([ref[:, s, :] for s in range(SLABS)], axis=1)


def _dispatch_kernel(pe_ref, dest_ref, h_ref, xs_hbm, stage, zbuf, sem, zsem):
    bm = h_ref.shape[0]

    @pl.when(pl.program_id(0) == 0)
    def _():
        zbuf[...] = jnp.zeros_like(zbuf)

        def tail_copy(e):
            start = pl.multiple_of(pe_ref[e + 1] - MOE_ROWS, MOE_ROWS)
            return pltpu.make_async_copy(zbuf, xs_hbm.at[pl.ds(start, MOE_ROWS)], zsem)

        for e in range(N_EXPERTS):
            @pl.when(pe_ref[e + 1] > pe_ref[e])
            def _():
                tail_copy(e).start()
        for e in range(N_EXPERTS):
            @pl.when(pe_ref[e + 1] > pe_ref[e])
            def _():
                tail_copy(e).wait()

        def spare_copy(b):
            return pltpu.make_async_copy(zbuf, xs_hbm.at[pl.ds(pl.multiple_of(b * MOE_ROWS, MOE_ROWS), MOE_ROWS)], zsem)

        first_spare = pe_ref[N_EXPERTS] // MOE_ROWS
        n_blk = xs_hbm.shape[0] // MOE_ROWS
        lax.fori_loop(first_spare, n_blk, lambda b, c: (spare_copy(b).start(), c)[1], 0)
        lax.fori_loop(first_spare, n_blk, lambda b, c: (spare_copy(b).wait(), c)[1], 0)

    _to_row_tiles(stage, h_ref[...])

    def issue(r, carry):
        for k in range(TOP_K):
            pltpu.make_async_copy(stage.at[r], xs_hbm.at[dest_ref[r * TOP_K + k]], sem).start(priority=k % 2)
        return carry

    lax.fori_loop(0, bm, issue, 0, unroll=2)
    for k in range(TOP_K):
        pltpu.make_async_copy(stage, xs_hbm.at[pl.ds(0, bm)], sem).wait()


def _dispatch(pad_bounds, dest, h, n_rows, *, bm):
    t, d = h.shape
    n_tiles = t // bm
    grid_spec = pltpu.PrefetchScalarGridSpec(
        num_scalar_prefetch=1,
        grid=(n_tiles,),
        in_specs=[
            pl.BlockSpec((None, None, bm * TOP_K), lambda i, pe: (i, 0, 0), memory_space=pltpu.SMEM),
            pl.BlockSpec((bm, d), lambda i, pe: (i, 0)),
        ],
        out_specs=pl.BlockSpec(memory_space=pl.ANY),
        scratch_shapes=[pltpu.VMEM((bm, SLABS, LANES), F32), pltpu.VMEM((MOE_ROWS, SLABS, LANES), F32),
                        pltpu.SemaphoreType.DMA, pltpu.SemaphoreType.DMA],
    )
    return pl.pallas_call(
        _dispatch_kernel,
        grid_spec=grid_spec,
        out_shape=jax.ShapeDtypeStruct((n_rows, SLABS, LANES), F32),
        compiler_params=_cparams(("arbitrary",)),
        name="moe_dispatch",
    )(pad_bounds, dest.reshape(n_tiles, 1, bm * TOP_K), h)


def _expert_kernel(be_ref, na_ref, xs_hbm, wg_ref, wl_ref, bg_ref, bl_ref, wd_ref, bd_ref, ys_hbm,
                   xbuf, ybuf, zbuf, sem_in, sem_out, zsem):
    i = pl.program_id(0)
    n_act = na_ref[0]
    slot = i % 2

    def rows(b):
        return pl.ds(pl.multiple_of(b * MOE_ROWS, MOE_ROWS), MOE_ROWS)

    def in_copy(b, s, j):
        return pltpu.make_async_copy(xs_hbm.at[rows(b), j, :], xbuf.at[s, :, pl.ds(j * LANES, LANES)], sem_in.at[s])

    def out_copy(b, s, j):
        return pltpu.make_async_copy(ybuf.at[s, :, pl.ds(j * LANES, LANES)], ys_hbm.at[rows(b), j, :], sem_out.at[s])

    @pl.when(i < n_act)
    def _():
        @pl.when(i == 0)
        def _():
            for j in range(SLABS):
                in_copy(0, 0, j).start()

        @pl.when(i + 1 < n_act)
        def _():
            for j in range(SLABS):
                in_copy(i + 1, 1 - slot, j).start()

        for j in range(SLABS):
            in_copy(i, slot, j).wait()

        @pl.when(i >= 2)
        def _():
            for j in range(SLABS):
                out_copy(i - 2, slot, j).wait()

        x = xbuf[slot].astype(BF16)
        hg = jnp.dot(x, wg_ref[...], preferred_element_type=F32) + bg_ref[...]
        hl = jnp.dot(x, wl_ref[...], preferred_element_type=F32) + bl_ref[...]
        glu = jnp.minimum(hg, SWIGLU_LIMIT)
        lin = jnp.clip(hl, -SWIGLU_LIMIT, SWIGLU_LIMIT)
        act = glu * (1.0 / (1.0 + jnp.exp(-SWIGLU_ALPHA * glu))) * (lin + 1.0)
        ybuf[slot] = jnp.dot(act.astype(BF16), wd_ref[...], preferred_element_type=F32) + bd_ref[...]
        for j in range(SLABS):
            out_copy(i, slot, j).start()

        @pl.when(i == n_act - 1)
        def _():
            for j in range(SLABS):
                out_copy(i, slot, j).wait()

            @pl.when(i >= 1)
            def _():
                for j in range(SLABS):
                    out_copy(i - 1, 1 - slot, j).wait()

    @pl.when(i >= n_act)
    def _():
        @pl.when(i == n_act)
        def _():
            zbuf[...] = jnp.zeros_like(zbuf)

        fill = pltpu.make_async_copy(zbuf, ys_hbm.at[rows(i)], zsem)
        fill.start()
        fill.wait()


def _experts(blk_expert, n_active, xs, wg, wl, bg, bl, wd, bd, layer):
    n_rows = xs.shape[0]
    d, dff = wg.shape[2], wg.shape[3]
    n_blk = n_rows // MOE_ROWS
    wmap = lambda i, be, na: (layer, be[i], 0, 0)
    grid_spec = pltpu.PrefetchScalarGridSpec(
        num_scalar_prefetch=2,
        grid=(n_blk,),
        in_specs=[
            pl.BlockSpec(memory_space=pl.ANY),
            pl.BlockSpec((None, None, d, dff), wmap),
            pl.BlockSpec((None, None, d, dff), wmap),
            pl.BlockSpec((None, None, 1, dff), wmap),
            pl.BlockSpec((None, None, 1, dff), wmap),
            pl.BlockSpec((None, None, dff, d), wmap),
            pl.BlockSpec((None, None, 1, d), wmap),
        ],
        out_specs=pl.BlockSpec(memory_space=pl.ANY),
        scratch_shapes=[pltpu.VMEM((2, MOE_ROWS, d), F32), pltpu.VMEM((2, MOE_ROWS, d), F32),
                        pltpu.VMEM((MOE_ROWS, SLABS, LANES), F32),
                        pltpu.SemaphoreType.DMA((2,)), pltpu.SemaphoreType.DMA((2,)), pltpu.SemaphoreType.DMA],
    )
    return pl.pallas_call(
        _expert_kernel,
        grid_spec=grid_spec,
        out_shape=jax.ShapeDtypeStruct((n_rows, SLABS, LANES), F32),
        compiler_params=_cparams(("arbitrary",)),
        name="moe_experts",
    )(blk_expert, n_active, xs, wg, wl, bg, bl, wd, bd)


def _combine_kernel(dest_ref, next_ref, g_ref, x_ref, gate_ref, gam_ref, ys_hbm, o_ref, *rest, final_norm, n_ctx_tiles):
    if final_norm:
        of_ref, buf, acc_scr, sem = rest
    else:
        buf, acc_scr, sem = rest
    bm = x_ref.shape[0]
    i = pl.program_id(0)
    slot = i % 2

    def gather(idx_ref, s):
        def issue(r, carry):
            for k in range(TOP_K):
                pltpu.make_async_copy(ys_hbm.at[idx_ref[r * TOP_K + k]], buf.at[s, k, r],
                                      sem.at[s]).start(priority=k % 2)
            return carry

        lax.fori_loop(0, bm, issue, 0, unroll=2)

    @pl.when(i == 0)
    def _():
        gather(dest_ref, 0)

    @pl.when(i + 1 < pl.num_programs(0))
    def _():
        gather(next_ref, 1 - slot)

    for k in range(TOP_K):
        pltpu.make_async_copy(ys_hbm.at[pl.ds(0, bm)], buf.at[slot, k], sem.at[slot]).wait()

    def mix(r, carry):
        acc = g_ref[r * TOP_K] * buf[slot, 0, r]
        for k in range(1, TOP_K):
            acc = acc + g_ref[r * TOP_K + k] * buf[slot, k, r]
        acc_scr[r] = acc
        return carry

    lax.fori_loop(0, bm, mix, 0, unroll=8)
    xn = x_ref[...] + gate_ref[...] * _from_row_tiles(acc_scr)
    if not final_norm:
        o_ref[...] = xn
    else:
        ms = jnp.mean(xn * xn, axis=-1, keepdims=True)
        yn = xn * lax.rsqrt(ms + EPS) * gam_ref[...]
        is_ctx = pl.program_id(0) < n_ctx_tiles

        @pl.when(is_ctx)
        def _():
            o_ref[...] = yn

        @pl.when(jnp.logical_not(is_ctx))
        def _():
            of_ref[...] = yn


def _combine(dest, x, gate, gates, gam_final, ys, tok, *, bm, final_norm):
    t, d = x.shape
    group = tok.group_of_tile(bm)
    n_tiles = t // bm
    n_ctx_tiles = tok.t_ctx // bm
    dest_tiles = dest.reshape(n_tiles, 1, bm * TOP_K)
    if final_norm:
        out_shape = [jax.ShapeDtypeStruct((tok.t_ctx, d), F32), jax.ShapeDtypeStruct((tok.t_dec, d), F32)]
        out_specs = [pl.BlockSpec((bm, d), lambda i: (jnp.minimum(i, n_ctx_tiles - 1), 0)),
                     pl.BlockSpec((bm, d), lambda i: (jnp.maximum(i - n_ctx_tiles, 0), 0))]
    else:
        out_shape = [jax.ShapeDtypeStruct((t, d), F32)]
        out_specs = [pl.BlockSpec((bm, d), lambda i: (i, 0))]
    return pl.pallas_call(
        functools.partial(_combine_kernel, final_norm=final_norm, n_ctx_tiles=n_ctx_tiles),
        grid=(n_tiles,),
        in_specs=[
            pl.BlockSpec((None, None, bm * TOP_K), lambda i: (i, 0, 0), memory_space=pltpu.SMEM),
            pl.BlockSpec((None, None, bm * TOP_K), lambda i: (jnp.minimum(i + 1, n_tiles - 1), 0, 0),
                         memory_space=pltpu.SMEM),
            pl.BlockSpec((None, None, bm * TOP_K), lambda i: (i, 0, 0), memory_space=pltpu.SMEM),
            pl.BlockSpec((bm, d), lambda i: (i, 0)),
            pl.BlockSpec((None, 1, d), lambda i: (group(i), 0, 0)),
            pl.BlockSpec((1, d), lambda i: (0, 0)),
            pl.BlockSpec(memory_space=pl.ANY),
        ],
        out_specs=out_specs,
        out_shape=out_shape,
        scratch_shapes=[pltpu.VMEM((2, TOP_K, bm, SLABS, LANES), F32), pltpu.VMEM((bm, SLABS, LANES), F32),
                        pltpu.SemaphoreType.DMA((2,))],
        compiler_params=_cparams(("arbitrary",)),
        name="moe_combine_final" if final_norm else "moe_combine",
    )(dest_tiles, dest_tiles, gates[:, :TOP_K].reshape(n_tiles, 1, bm * TOP_K), x, gate, gam_final, ys)


def _moe(x, gam, sc, sh, gate, w_r, b_r, wg, wl, bg, bl, wd, bd, layer, gam_final, tok, *, final_norm):
    t, d = x.shape
    h, e_out, r_out, gates, counts = _router(x, gam, sc, sh, w_r, b_r, tok, bm=512)
    counts = counts[0, :N_EXPERTS].astype(jnp.int32)
    padded = (counts + MOE_ROWS - 1) // MOE_ROWS * MOE_ROWS
    pad_end = jnp.cumsum(padded)
    pad_start = pad_end - padded
    e_sel = e_out[:, :TOP_K]
    dest = pad_start[e_sel] + r_out[:, :TOP_K]
    n_rows = t * TOP_K + N_EXPERTS * MOE_ROWS
    n_blk = n_rows // MOE_ROWS
    blk_start = jnp.arange(n_blk, dtype=jnp.int32) * MOE_ROWS
    blk_expert = jnp.minimum(jnp.sum((pad_end[None, :] <= blk_start[:, None]).astype(jnp.int32), axis=1),
                             N_EXPERTS - 1)
    n_active = (pad_end[-1:] // MOE_ROWS).astype(jnp.int32)
    pad_bounds = jnp.concatenate([jnp.zeros((1,), jnp.int32), pad_end.astype(jnp.int32)])
    xs = _dispatch(pad_bounds, dest, h, n_rows, bm=256)
    ys = _experts(blk_expert, n_active, xs, wg, wl, bg, bl, wd, bd, layer)
    return _combine(dest, x, gate, gates, gam_final, ys, tok, bm=256, final_norm=final_norm)


def _rope_tables(tok):
    rows = tok.dec_len // GRID_W
    r = jnp.repeat(jnp.arange(rows), GRID_W).astype(F32)
    cidx = jnp.tile(jnp.arange(GRID_W), rows).astype(F32)
    n_freq = DK_RET // 4
    inv = ROPE_BASE ** (-jnp.arange(n_freq, dtype=F32) / n_freq)
    ang = jnp.concatenate([r[:, None] * inv, cidx[:, None] * inv], axis=-1)
    cos = jnp.concatenate([jnp.ones((tok.t_ctx, DK_RET // 2), F32), jnp.tile(jnp.cos(ang), (tok.n_dec_seq, 1))])
    sin = jnp.concatenate([jnp.zeros((tok.t_ctx, DK_RET // 2), F32), jnp.tile(jnp.sin(ang), (tok.n_dec_seq, 1))])
    return cos, sin


def _group_cols(p):
    h = p.shape[1]
    a = p.reshape(2, SSD_GROUPS, h // SSD_GROUPS).transpose(1, 0, 2).reshape(SSD_GROUPS, -1)
    return jnp.pad(a, ((0, 0), (0, LANES - a.shape[1])))[:, None, :]


def kernel(x_prompt, x_sample, state_ssd, state_ret, c, c_ctx, w_mod, b_mod, norm_mix, norm_ffn, w_in_a, conv_w, conv_b, dt_bias, a_log, d_skip, ssd_norm, w_sp, b_sp, w_out_a, w_in_c, decay_logit, w_out_c, w_router, b_router, w_gu, b_gu, w_down, b_down, norm_final):
    n_ctx, ctx_len, d = x_prompt.shape
    n_dec, dec_len, _ = x_sample.shape
    tok = _Tokens(n_ctx, ctx_len, n_dec, dec_len)
    depth = w_mod.shape[0]
    x = jnp.concatenate([x_prompt.reshape(tok.t_ctx, d), x_sample.reshape(tok.t_dec, d)])

    cvecs = jnp.concatenate([c_ctx[None], c, jnp.zeros((MOD_ROWS - 1 - n_dec, d), F32)])
    mod = _modulation(cvecs, w_mod, b_mod)
    mod = mod.reshape(depth, MOD_ROWS, 6, 1, d).transpose(0, 2, 1, 3, 4)

    h_ssd = a_log.shape[2]
    xbc_w = d + 2 * SSD_GROUPS * D_STATE
    o1, o2, o3 = d, d + xbc_w, d + xbc_w + 2 * h_ssd
    cos, sin = _rope_tables(tok)
    wg_all, wl_all = _split_gate_lin(w_gu)
    bg_all = b_gu[:, :, None, 0::2]
    bl_all = b_gu[:, :, None, 1::2]
    wd_all = w_down.astype(BF16)
    bd_all = b_down[:, :, None, :]
    new_ssd, new_ret = [], []
    y_final = None
    bm_proj = 512

    for l in range(depth):
        sh1, sc1, g1, sh2, sc2, g2 = (mod[l, j] for j in range(6))
        gam_mix = norm_mix[l][None]
        i = l // 2
        if l % 2 == 0:
            w_in = w_in_a[i]
            w_z = w_in[:, :o1].astype(BF16)
            w_xbc = w_in[:, o1:o2].astype(BF16)
            w_dt = w_in[:, o2:o3].reshape(d, 2, SSD_GROUPS, h_ssd // SSD_GROUPS).transpose(0, 2, 1, 3)
            w_dt = jnp.pad(w_dt.reshape(d, SSD_GROUPS, -1), ((0, 0), (0, 0), (0, LANES - 2 * h_ssd // SSD_GROUPS)))
            w_dt = w_dt.reshape(d, SSD_GROUPS * LANES).astype(BF16)
            w_uv = w_in[:, o3:].astype(BF16)
            xbc = _norm_proj(x, gam_mix, sc1, sh1, w_xbc, tok=tok, bm=dec_len, tn=256, out_dtype=BF16,
                             epilogue=_ep_conv_silu, extra=(conv_w[i], conv_b[i][None]),
                             extra_specs=(pl.BlockSpec((4, 256), lambda r, j: (0, j)),
                                          pl.BlockSpec((1, 256), lambda r, j: (0, j))), name="proj_xbc")
            dtb = _group_cols(dt_bias[i]).reshape(1, SSD_GROUPS * LANES)
            sz, dt, u, v = _norm_proj_multi(
                x, gam_mix, sc1, sh1, jnp.concatenate([w_z, w_dt, w_uv], axis=1), tok,
                [(d, 512, BF16, _ep_silu, False),
                 (SSD_GROUPS * LANES, SSD_GROUPS * LANES, F32, _ep_softplus_bias, True),
                 (d, 512, BF16, _ep_gelu, False), (d, d, BF16, _ep_gelu_ln, False)],
                bm=bm_proj, extra=(dtb,), extra_specs=(pl.BlockSpec((1, SSD_GROUPS * LANES), lambda r: (0, 0)),),
                name="proj_zdtuv")
            a_neg = _group_cols(-jnp.exp(a_log[i]))
            dsk = jnp.repeat(d_skip[i], SSD_HEAD).reshape(SSD_GROUPS, 1, GROUP_W)
            nrm = ssd_norm[i].reshape(SSD_GROUPS, 1, GROUP_W)
            s0 = state_ssd[:, i].reshape(n_dec, 2, SSD_GROUPS, HEADS_PER_GROUP, D_STATE, SSD_HEAD)
            s0 = s0.transpose(0, 2, 1, 4, 3, 5).reshape(n_dec, SSD_GROUPS, 2, D_STATE, GROUP_W)
            y_ctx, s_fin = _ssd_scan(xbc, dt, sz, a_neg, dsk, nrm, None, row0=0, n_seq=n_ctx,
                                     seq_len=ctx_len, want_final=True)
            (y_dec,) = _ssd_scan(xbc, dt, sz, a_neg, dsk, nrm, s0, row0=tok.t_ctx, n_seq=n_dec,
                                 seq_len=dec_len, want_final=False)
            s_fin = s_fin.reshape(n_ctx, SSD_GROUPS, 2, D_STATE, HEADS_PER_GROUP, SSD_HEAD)
            new_ssd.append(s_fin.transpose(0, 2, 1, 4, 3, 5).reshape(n_ctx, 2, h_ssd, D_STATE, SSD_HEAD))
            b_full = jnp.repeat(b_sp[i].T, LANES, axis=1)
            w_o = w_out_a[i].astype(BF16)
            x = _out_proj_a(x, g1, y_ctx, y_dec, u, v, w_sp[i].astype(BF16), b_full, w_o[:d], w_o[d:], tok, bm=512)
        else:
            hk = H_RET * DK_RET
            hv = H_RET * DV_RET
            w_in = w_in_c[i].astype(BF16)
            rope_specs = (pl.BlockSpec((bm_proj, DK_RET // 2), lambda r: (r, 0)),) * 2
            q, kk, vv, sg = _norm_proj_multi(
                x, gam_mix, sc1, sh1, w_in, tok,
                [(hk, 512, BF16, functools.partial(_ep_rope, scale=1.0), True),
                 (hk, 512, BF16, functools.partial(_ep_rope, scale=DK_RET ** -0.5), True),
                 (hv, 512, BF16, _ep_plain, False), (hv, 512, BF16, _ep_silu, False)],
                bm=bm_proj, extra=(cos, sin), extra_specs=rope_specs, name="proj_qkvg")
            log_g = jax.nn.log_sigmoid(decay_logit[i].astype(F32))
            y_ctx, s_fin = _ret_scan(log_g, q, kk, vv, sg, None, row0=0, n_seq=n_ctx, seq_len=ctx_len, want_final=True)
            (y_dec,) = _ret_scan(log_g, q, kk, vv, sg, state_ret[:, i], row0=tok.t_ctx, n_seq=n_dec, seq_len=dec_len,
                                 want_final=False)
            new_ret.append(s_fin)
            x = _out_proj_c(x, g1, y_ctx, y_dec, w_out_c[i].astype(BF16), tok, bm=512)

        w_r = jnp.pad(w_router[l], ((0, 0), (0, LANES - N_EXPERTS)))
        b_r = jnp.pad(b_router[l], (0, LANES - N_EXPERTS), constant_values=-1e30)[None]
        last = l == depth - 1
        res = _moe(x, norm_ffn[l][None], sc2, sh2, g2, w_r, b_r, wg_all, wl_all, bg_all, bl_all, wd_all, bd_all, l,
                   norm_final[None], tok, final_norm=last)
        if last:
            y_final = res
        else:
            x = res[0]

    y_prompt = y_final[0].reshape(n_ctx, ctx_len, d)
    y_sample = y_final[1].reshape(n_dec, dec_len, d)
    return (y_prompt, y_sample, jnp.stack(new_ssd, axis=1), jnp.stack(new_ret, axis=1))
```

```python
import functools
import math

import jax
import jax.numpy as jnp
from jax import lax
from jax.experimental import pallas as pl
from jax.experimental.pallas import tpu as pltpu

D_MODEL = 1024
GRID_W = 64
CHUNK = 128
SSD_HEAD = 64
SSD_GROUPS = 2
D_STATE = 128
GROUP_W = D_MODEL // SSD_GROUPS
HEADS_PER_GROUP = GROUP_W // SSD_HEAD
SG_GROUPS = 8
H_RET = 4
DK_RET = D_MODEL // H_RET
DV_RET = 2 * DK_RET
ROPE_BASE = 10000.0
N_EXPERTS = 32
TOP_K = 4
SWIGLU_LIMIT = 7.0
SWIGLU_ALPHA = 1.702
EPS = 1e-6

LANES = 128
MOD_ROWS = 8
MOE_ROWS = 512
VMEM_LIMIT = 56 * 1024 * 1024

F32 = jnp.float32
BF16 = jnp.bfloat16
HI = lax.Precision.HIGHEST


def _cparams(sem):
    return pltpu.CompilerParams(dimension_semantics=sem, vmem_limit_bytes=VMEM_LIMIT)


def _silu(x):
    return x * (1.0 / (1.0 + jnp.exp(-x)))


def _gelu_tanh(x):
    return 0.5 * x * (1.0 + jnp.tanh(math.sqrt(2.0 / math.pi) * (x + 0.044715 * (x * x * x))))


def _softplus(x):
    return jnp.maximum(x, 0.0) + jnp.log(1.0 + jnp.exp(-jnp.abs(x)))


def _mod_kernel(c_ref, w_ref, b_ref, o_ref):
    a = _silu(c_ref[...])
    o_ref[...] = jnp.dot(a, w_ref[...], precision=HI, preferred_element_type=F32) + b_ref[...]


def _modulation(cvecs, w_mod, b_mod):
    depth, d, n = w_mod.shape
    tn = 1536
    return pl.pallas_call(
        _mod_kernel,
        grid=(depth, n // tn),
        in_specs=[
            pl.BlockSpec((MOD_ROWS, d), lambda l, j: (0, 0)),
            pl.BlockSpec((None, d, tn), lambda l, j: (l, 0, j)),
            pl.BlockSpec((None, 1, tn), lambda l, j: (l, 0, j)),
        ],
        out_specs=pl.BlockSpec((None, MOD_ROWS, tn), lambda l, j: (l, 0, j)),
        out_shape=jax.ShapeDtypeStruct((depth, MOD_ROWS, n), F32),
        compiler_params=_cparams(("arbitrary", "arbitrary")),
        name="modulation",
    )(cvecs, w_mod, b_mod.reshape(depth, 1, n))


class _Tokens:
    def __init__(self, n_ctx_seq, ctx_len, n_dec_seq, dec_len):
        self.n_ctx_seq, self.ctx_len = n_ctx_seq, ctx_len
        self.n_dec_seq, self.dec_len = n_dec_seq, dec_len
        self.t_ctx = n_ctx_seq * ctx_len
        self.t_dec = n_dec_seq * dec_len
        self.total = self.t_ctx + self.t_dec

    def group_of_tile(self, bm):
        assert self.t_ctx % bm == 0 and self.dec_len % bm == 0
        n_ctx_tiles = self.t_ctx // bm
        per_seq = self.dec_len // bm

        def group(i):
            return jnp.where(i < n_ctx_tiles, 0, 1 + (i - n_ctx_tiles) // per_seq)

        return group


def _modulated_norm(x, gam, sc, sh):
    ms = jnp.mean(x * x, axis=-1, keepdims=True)
    return (x * lax.rsqrt(ms + EPS) * gam) * (1.0 + sc) + sh


def _proj_kernel(*refs, epilogue, n_extra, tok, bm):
    x_ref, gam_ref, sc_ref, sh_ref, w_ref = refs[:5]
    extra = refs[5:5 + n_extra]
    o_ref = refs[5 + n_extra]
    h_scr = refs[6 + n_extra]
    i = pl.program_id(0)

    @pl.when(pl.program_id(1) == 0)
    def _():
        h_scr[...] = _modulated_norm(x_ref[...], gam_ref[...], sc_ref[...], sh_ref[...]).astype(BF16)

    acc = jnp.dot(h_scr[...], w_ref[...], preferred_element_type=F32)
    o_ref[...] = epilogue(acc, i, tok, bm, *extra).astype(o_ref.dtype)


def _ep_plain(acc, i, tok, bm):
    return acc


def _ep_silu(acc, i, tok, bm):
    return _silu(acc)


def _ep_gelu(acc, i, tok, bm):
    return _gelu_tanh(acc)


def _ep_softplus_bias(acc, i, tok, bm, bias_ref):
    return _softplus(acc + bias_ref[...])


def _ep_gelu_ln(acc, i, tok, bm):
    g = _gelu_tanh(acc)
    mu = jnp.mean(g, axis=-1, keepdims=True)
    gc = g - mu
    return gc * lax.rsqrt(jnp.mean(gc * gc, axis=-1, keepdims=True) + 1e-5)


def _ep_conv_silu(acc, i, tok, bm, cw_ref, cb_ref):
    n = acc.shape[0]
    seq = jnp.where(i * bm < tok.t_ctx, tok.ctx_len, tok.dec_len)
    t = lax.broadcasted_iota(jnp.int32, (n, 1), 0) & (seq - 1)
    cw = cw_ref[...]
    y = acc * cw[2:3, :] + cb_ref[...]
    y = y + jnp.where(t >= 2, pltpu.roll(acc, 2, axis=0), 0.0) * cw[0:1, :]
    y = y + jnp.where(t >= 1, pltpu.roll(acc, 1, axis=0), 0.0) * cw[1:2, :]
    y = y + jnp.where(t < seq - 1, pltpu.roll(acc, n - 1, axis=0), 0.0) * cw[3:4, :]
    return _silu(y)


def _ep_rope(acc, i, tok, bm, cos_ref, sin_ref, *, scale):
    cs, sn = cos_ref[...], sin_ref[...]
    half = DK_RET // 2
    outs = []
    for h in range(acc.shape[1] // DK_RET):
        x1 = acc[:, h * DK_RET:h * DK_RET + half] * scale
        x2 = acc[:, h * DK_RET + half:(h + 1) * DK_RET] * scale
        outs.append(x1 * cs - x2 * sn)
        outs.append(x2 * cs + x1 * sn)
    return jnp.concatenate(outs, axis=1)


def _norm_proj(x, gam, sc, sh, w, tok, *, bm, tn, out_dtype, epilogue, extra=(), extra_specs=(), name):
    t, d = x.shape
    n = w.shape[1]
    group = tok.group_of_tile(bm)
    kern = functools.partial(_proj_kernel, epilogue=epilogue, n_extra=len(extra), tok=tok, bm=bm)
    return pl.pallas_call(
        kern,
        grid=(t // bm, n // tn),
        in_specs=[
            pl.BlockSpec((bm, d), lambda i, j: (i, 0)),
            pl.BlockSpec((1, d), lambda i, j: (0, 0)),
            pl.BlockSpec((None, 1, d), lambda i, j: (group(i), 0, 0)),
            pl.BlockSpec((None, 1, d), lambda i, j: (group(i), 0, 0)),
            pl.BlockSpec((d, tn), lambda i, j: (0, j)),
            *extra_specs,
        ],
        out_specs=pl.BlockSpec((bm, tn), lambda i, j: (i, j)),
        out_shape=jax.ShapeDtypeStruct((t, n), out_dtype),
        scratch_shapes=[pltpu.VMEM((bm, d), BF16)],
        compiler_params=_cparams(("arbitrary", "arbitrary")),
        name=name,
    )(x, gam, sc, sh, w, *extra)


def _proj_multi_kernel(*refs, pieces, n_extra, tok, bm):
    x_ref, gam_ref, sc_ref, sh_ref, w_ref = refs[:5]
    extra = refs[5:5 + n_extra]
    outs = refs[5 + n_extra:5 + n_extra + len(pieces)]
    h_scr = refs[5 + n_extra + len(pieces)]
    i = pl.program_id(0)
    h_scr[...] = _modulated_norm(x_ref[...], gam_ref[...], sc_ref[...], sh_ref[...]).astype(BF16)
    c0 = 0
    for (width, chunk, _, epilogue, uses_extra), o_ref in zip(pieces, outs):
        for j in range(width // chunk):
            acc = jnp.dot(h_scr[...], w_ref[:, c0 + j * chunk:c0 + (j + 1) * chunk], preferred_element_type=F32)
            res = epilogue(acc, i, tok, bm, *(extra if uses_extra else ()))
            o_ref[:, j * chunk:(j + 1) * chunk] = res.astype(o_ref.dtype)
        c0 += width


def _norm_proj_multi(x, gam, sc, sh, w, tok, pieces, *, bm, extra=(), extra_specs=(), name):
    t, d = x.shape
    n = w.shape[1]
    assert n == sum(p[0] for p in pieces)
    group = tok.group_of_tile(bm)
    kern = functools.partial(_proj_multi_kernel, pieces=tuple(pieces), n_extra=len(extra), tok=tok, bm=bm)
    return pl.pallas_call(
        kern,
        grid=(t // bm,),
        in_specs=[
            pl.BlockSpec((bm, d), lambda i: (i, 0)),
            pl.BlockSpec((1, d), lambda i: (0, 0)),
            pl.BlockSpec((None, 1, d), lambda i: (group(i), 0, 0)),
            pl.BlockSpec((None, 1, d), lambda i: (group(i), 0, 0)),
            pl.BlockSpec((d, n), lambda i: (0, 0), pipeline_mode=pl.Buffered(1)),
            *extra_specs,
        ],
        out_specs=[pl.BlockSpec((bm, p[0]), lambda i: (i, 0)) for p in pieces],
        out_shape=[jax.ShapeDtypeStruct((t, p[0]), p[2]) for p in pieces],
        scratch_shapes=[pltpu.VMEM((bm, d), BF16)],
        compiler_params=_cparams(("arbitrary",)),
        name=name,
    )(x, gam, sc, sh, w, *extra)


def _ssd_kernel(*refs, n_chunks, has_init, has_final):
    xs_ref, b_ref, c_ref, dt_ref, sz_ref, a_ref, dsk_ref, nrm_ref = refs[:8]
    k = 8
    s0_ref = None
    if has_init:
        s0_ref = refs[k]
        k += 1
    y_ref = refs[k]
    k += 1
    sfin_ref = None
    if has_final:
        sfin_ref = refs[k]
        k += 1
    y_scrs = (refs[k], refs[k + 1])
    s_scrs = (refs[k + 2], refs[k + 3])

    q = CHUNK
    row = lax.broadcasted_iota(jnp.int32, (q, q), 0)
    col = lax.broadcasted_iota(jnp.int32, (q, q), 1)
    lane = lax.broadcasted_iota(jnp.int32, (1, q), 1)
    left = lane < SSD_HEAD
    a_neg = a_ref[...]
    n_pairs = HEADS_PER_GROUP // 2
    keeps = (col <= row, col >= row)
    tris = tuple(kp.astype(F32).astype(BF16) for kp in keeps)

    def masked_sums(tri, x):
        hi = x.astype(BF16)
        r1 = x - hi.astype(F32)
        mid = r1.astype(BF16)
        lo = (r1 - mid.astype(F32)).astype(BF16)
        return (jnp.dot(tri, hi, preferred_element_type=F32) + jnp.dot(tri, mid, preferred_element_type=F32)
                + jnp.dot(tri, lo, preferred_element_type=F32))

    def chunk(ci, direction):
        s_scr = s_scrs[direction]
        r0 = pl.multiple_of(ci * q, q)
        xs = xs_ref[pl.ds(r0, q), :]
        bm_ = b_ref[pl.ds(r0, q), :]
        cm = c_ref[pl.ds(r0, q), :]
        dt = dt_ref[pl.ds(r0, q), :]
        la = dt * a_neg
        keep, tri = keeps[direction], tris[direction]
        last = q - 1 if direction == 0 else 0
        cum = masked_sums(tri, la)
        cum_t = cum.T
        dt_t = dt.T
        tot_t = jnp.broadcast_to(cum_t[:, last:last + 1], (q, q))
        w_t = dt_t * jnp.exp(tot_t - cum_t)
        g = lax.dot_general(cm, bm_, (((1,), (1,)), ((), ())), preferred_element_type=F32)
        b_t = bm_.astype(F32).T
        cm_f = cm.astype(F32)
        outs = []
        for p in range(n_pairs):
            xs_p = xs[:, p * LANES:(p + 1) * LANES]
            s_p = s_scr[:, p * LANES:(p + 1) * LANES]
            s_b = s_p.astype(BF16)
            zero = jnp.zeros_like(xs_p)
            zero_s = jnp.zeros_like(s_b)
            lhs, rhs, lhs_s, rhs_s, decs = [], [], [], [], []
            for hh in range(2):
                cidx = direction * HEADS_PER_GROUP + 2 * p + hh
                cum_b = jnp.broadcast_to(cum[:, cidx:cidx + 1], (q, q))
                dec = jnp.exp(jnp.where(keep, cum_b - cum_t[cidx:cidx + 1, :], -jnp.inf))
                scores = g * dec * dt_t[cidx:cidx + 1, :]
                lhs += [scores.astype(BF16), (cm_f * jnp.exp(cum_b)).astype(BF16)]
                sel = left if hh == 0 else jnp.logical_not(left)
                rhs += [jnp.where(sel, xs_p, zero), jnp.where(sel, s_b, zero_s)]
                lhs_s.append((b_t * w_t[cidx:cidx + 1, :]).astype(BF16))
                rhs_s.append(jnp.where(sel, xs_p, zero))
                decs.append(jnp.exp(cum_t[cidx:cidx + 1, last:last + 1]))
            y_p = jnp.dot(jnp.concatenate(lhs, axis=1), jnp.concatenate(rhs, axis=0),
                          preferred_element_type=F32)
            upd = jnp.dot(jnp.concatenate(lhs_s, axis=1), jnp.concatenate(rhs_s, axis=0),
                          preferred_element_type=F32)
            s_scr[:, p * LANES:(p + 1) * LANES] = s_p * jnp.where(left, decs[0], decs[1]) + upd
            outs.append(y_p)
        return jnp.concatenate(outs, axis=1)

    for direction in range(2):
        if has_init:
            s_scrs[direction][...] = s0_ref[direction]
        else:
            s_scrs[direction][...] = jnp.zeros_like(s_scrs[direction])

    def scan_body(j, carry):
        for direction, ci in ((0, j), (1, n_chunks - 1 - j)):
            r0 = pl.multiple_of(ci * q, q)
            y_scrs[direction][pl.ds(r0, q), :] = chunk(ci, direction)
        return carry

    lax.fori_loop(0, n_chunks, scan_body, 0)
    if has_final:
        for direction in range(2):
            sfin_ref[direction] = s_scrs[direction][...]

    def finish_body(ci, carry):
        r0 = pl.multiple_of(ci * q, q)
        y = y_scrs[0][pl.ds(r0, q), :] + y_scrs[1][pl.ds(r0, q), :]
        y = y + dsk_ref[...] * xs_ref[pl.ds(r0, q), :].astype(F32)
        y = y * sz_ref[pl.ds(r0, q), :].astype(F32)
        y = y * lax.rsqrt(jnp.mean(y * y, axis=-1, keepdims=True) + EPS)
        y_ref[pl.ds(r0, q), :] = (y * nrm_ref[...]).astype(y_ref.dtype)
        return carry

    lax.fori_loop(0, n_chunks, finish_body, 0)


def _ssd_scan(xbc, dt, sz, a_neg, dsk, nrm, s0, *, row0, n_seq, seq_len, want_final):
    assert row0 % seq_len == 0
    b0 = row0 // seq_len
    has_init = s0 is not None
    gw = GROUP_W
    b_blk0 = D_MODEL // D_STATE
    c_blk0 = b_blk0 + SSD_GROUPS
    in_specs = [
        pl.BlockSpec((seq_len, gw), lambda b, g: (b0 + b, g)),
        pl.BlockSpec((seq_len, D_STATE), lambda b, g: (b0 + b, b_blk0 + g)),
        pl.BlockSpec((seq_len, D_STATE), lambda b, g: (b0 + b, c_blk0 + g)),
        pl.BlockSpec((seq_len, LANES), lambda b, g: (b0 + b, g)),
        pl.BlockSpec((seq_len, gw), lambda b, g: (b0 + b, g)),
        pl.BlockSpec((None, 1, LANES), lambda b, g: (g, 0, 0)),
        pl.BlockSpec((None, 1, gw), lambda b, g: (g, 0, 0)),
        pl.BlockSpec((None, 1, gw), lambda b, g: (g, 0, 0)),
    ]
    args = [xbc, xbc, xbc, dt, sz, a_neg, dsk, nrm]
    if has_init:
        in_specs.append(pl.BlockSpec((None, None, 2, D_STATE, gw), lambda b, g: (b, g, 0, 0, 0)))
        args.append(s0)
    out_shape = [jax.ShapeDtypeStruct((n_seq * seq_len, D_MODEL), BF16)]
    out_specs = [pl.BlockSpec((seq_len, gw), lambda b, g: (b, g))]
    if want_final:
        out_shape.append(jax.ShapeDtypeStruct((n_seq, SSD_GROUPS, 2, D_STATE, gw), F32))
        out_specs.append(pl.BlockSpec((None, None, 2, D_STATE, gw), lambda b, g: (b, g, 0, 0, 0)))
    kern = functools.partial(_ssd_kernel, n_chunks=seq_len // CHUNK, has_init=has_init, has_final=want_final)
    return pl.pallas_call(
        kern,
        grid=(n_seq, SSD_GROUPS),
        in_specs=in_specs,
        out_specs=out_specs,
        out_shape=out_shape,
        scratch_shapes=[pltpu.VMEM((seq_len, gw), F32), pltpu.VMEM((seq_len, gw), F32),
                        pltpu.VMEM((D_STATE, gw), F32), pltpu.VMEM((D_STATE, gw), F32)],
        compiler_params=_cparams(("arbitrary", "arbitrary")),
        name="ssd_scan_ctx" if want_final else "ssd_scan_dec",
    )(*args)


def _ret_kernel(*refs, n_chunks, has_init, has_final):
    lg_ref, q_ref, k_ref, v_ref, sg_ref = refs[:5]
    k = 5
    s0_ref = None
    if has_init:
        s0_ref = refs[k]
        k += 1
    y_ref = refs[k]
    k += 1
    sfin_ref = None
    if has_final:
        sfin_ref = refs[k]
        k += 1
    y_scrs = (refs[k], refs[k + 1])
    s_scrs = (refs[k + 2], refs[k + 3])

    qn = CHUNK
    h = pl.program_id(1)
    row = lax.broadcasted_iota(jnp.int32, (qn, qn), 0)
    col = lax.broadcasted_iota(jnp.int32, (qn, qn), 1)
    rowk = lax.broadcasted_iota(jnp.int32, (qn, DK_RET), 0).astype(F32)

    def tables(direction):
        lg = lg_ref[direction, h]
        if direction == 0:
            keep = col <= row
            dist = (row - col).astype(F32)
            e_q = jnp.exp(lg * (rowk + 1.0))
            w_k = jnp.exp(lg * (qn - 1.0 - rowk))
        else:
            keep = col >= row
            dist = (col - row).astype(F32)
            e_q = jnp.exp(lg * (qn - rowk))
            w_k = jnp.exp(lg * rowk)
        dmat = jnp.where(keep, jnp.exp(lg * dist), 0.0)
        return dmat, e_q, w_k, jnp.exp(jnp.full((1, 1), float(qn), F32) * lg)

    def chunk(ci, direction, tabs):
        dmat, e_q, w_k, dec = tabs
        s_scr = s_scrs[direction]
        r0 = pl.multiple_of(ci * qn, qn)
        qc = q_ref[pl.ds(r0, qn), :]
        kc = k_ref[pl.ds(r0, qn), :]
        vc = v_ref[pl.ds(r0, qn), :]
        scores = lax.dot_general(qc, kc, (((1,), (1,)), ((), ())), preferred_element_type=F32) * dmat
        s_old = s_scr[...]
        lhs = jnp.concatenate([scores.astype(BF16), (qc.astype(F32) * e_q).astype(BF16)], axis=1)
        rhs = jnp.concatenate([vc, s_old.astype(BF16)], axis=0)
        y = jnp.dot(lhs, rhs, preferred_element_type=F32)
        kw_t = (kc.astype(F32) * w_k).T.astype(BF16)
        s_scr[...] = s_old * dec + jnp.dot(kw_t, vc, preferred_element_type=F32)
        return y

    for direction in range(2):
        if has_init:
            s_scrs[direction][...] = s0_ref[direction]
        else:
            s_scrs[direction][...] = jnp.zeros_like(s_scrs[direction])
    tabs = (tables(0), tables(1))

    def scan_body(j, carry):
        for direction, ci in ((0, j), (1, n_chunks - 1 - j)):
            r0 = pl.multiple_of(ci * qn, qn)
            y_scrs[direction][pl.ds(r0, qn), :] = chunk(ci, direction, tabs[direction])
        return carry

    lax.fori_loop(0, n_chunks, scan_body, 0)
    if has_final:
        for direction in range(2):
            sfin_ref[direction] = s_scrs[direction][...]

    def finish_body(ci, carry):
        r0 = pl.multiple_of(ci * qn, qn)
        y = y_scrs[0][pl.ds(r0, qn), :] + y_scrs[1][pl.ds(r0, qn), :]
        y = y * lax.rsqrt(jnp.mean(y * y, axis=-1, keepdims=True) + EPS)
        y_ref[pl.ds(r0, qn), :] = (y * sg_ref[pl.ds(r0, qn), :].astype(F32)).astype(y_ref.dtype)
        return carry

    lax.fori_loop(0, n_chunks, finish_body, 0)


def _ret_scan(log_g, q, k, v, sg, s0, *, row0, n_seq, seq_len, want_final):
    assert row0 % seq_len == 0
    b0 = row0 // seq_len
    has_init = s0 is not None
    in_specs = [
        pl.BlockSpec(memory_space=pltpu.SMEM),
        pl.BlockSpec((seq_len, DK_RET), lambda b, h: (b0 + b, h)),
        pl.BlockSpec((seq_len, DK_RET), lambda b, h: (b0 + b, h)),
        pl.BlockSpec((seq_len, DV_RET), lambda b, h: (b0 + b, h)),
        pl.BlockSpec((seq_len, DV_RET), lambda b, h: (b0 + b, h)),
    ]
    args = [log_g, q, k, v, sg]
    if has_init:
        in_specs.append(pl.BlockSpec((None, 2, None, DK_RET, DV_RET), lambda b, h: (b, 0, h, 0, 0)))
        args.append(s0)
    out_shape = [jax.ShapeDtypeStruct((n_seq * seq_len, H_RET * DV_RET), BF16)]
    out_specs = [pl.BlockSpec((seq_len, DV_RET), lambda b, h: (b, h))]
    if want_final:
        out_shape.append(jax.ShapeDtypeStruct((n_seq, 2, H_RET, DK_RET, DV_RET), F32))
        out_specs.append(pl.BlockSpec((None, 2, None, DK_RET, DV_RET), lambda b, h: (b, 0, h, 0, 0)))
    kern = functools.partial(_ret_kernel, n_chunks=seq_len // CHUNK, has_init=has_init, has_final=want_final)
    return pl.pallas_call(
        kern,
        grid=(n_seq, H_RET),
        in_specs=in_specs,
        out_specs=out_specs,
        out_shape=out_shape,
        scratch_shapes=[pltpu.VMEM((seq_len, DV_RET), F32), pltpu.VMEM((seq_len, DV_RET), F32),
                        pltpu.VMEM((DK_RET, DV_RET), F32), pltpu.VMEM((DK_RET, DV_RET), F32)],
        compiler_params=_cparams(("arbitrary", "arbitrary")),
        name="ret_scan_ctx" if want_final else "ret_scan_dec",
    )(*args)


def _mixer_rows(tok, bm):
    n_ctx_tiles = tok.t_ctx // bm

    def specs(width):
        return [pl.BlockSpec((bm, width), lambda i: (jnp.minimum(i, n_ctx_tiles - 1), 0)),
                pl.BlockSpec((bm, width), lambda i: (jnp.maximum(i - n_ctx_tiles, 0), 0))]

    def select(yc_ref, yd_ref):
        return jnp.where(pl.program_id(0) < n_ctx_tiles, yc_ref[...], yd_ref[...])

    return specs, select


def _out_a_kernel(x_ref, gate_ref, yc_ref, yd_ref, u_ref, v_ref, wsp_ref, bsp_ref, w1_ref, w2_ref, o_ref, sgu_scr,
                  *, select):
    bm = x_ref.shape[0]
    for ci in range(bm // CHUNK):
        rows = slice(ci * CHUNK, (ci + 1) * CHUNK)
        for g in range(SG_GROUPS):
            cols = slice(g * LANES, (g + 1) * LANES)
            mix = jnp.dot(wsp_ref[g], v_ref[rows, cols], preferred_element_type=F32) + bsp_ref[:, cols]
            sgu_scr[rows, cols] = (u_ref[rows, cols].astype(F32) * mix).astype(BF16)
    out = jnp.dot(select(yc_ref, yd_ref), w1_ref[...], preferred_element_type=F32)
    out = out + jnp.dot(sgu_scr[...], w2_ref[...], preferred_element_type=F32)
    o_ref[...] = x_ref[...] + gate_ref[...] * out


def _out_proj_a(x, gate, y_ctx, y_dec, u, v, w_sp, b_full, w1, w2, tok, *, bm):
    t, d = x.shape
    group = tok.group_of_tile(bm)
    y_specs, select = _mixer_rows(tok, bm)
    return pl.pallas_call(
        functools.partial(_out_a_kernel, select=select),
        grid=(t // bm,),
        in_specs=[
            pl.BlockSpec((bm, d), lambda i: (i, 0)),
            pl.BlockSpec((None, 1, d), lambda i: (group(i), 0, 0)),
            *y_specs(d),
            pl.BlockSpec((bm, d), lambda i: (i, 0)),
            pl.BlockSpec((bm, d), lambda i: (i, 0)),
            pl.BlockSpec((SG_GROUPS, CHUNK, CHUNK), lambda i: (0, 0, 0)),
            pl.BlockSpec((CHUNK, d), lambda i: (0, 0)),
            pl.BlockSpec((d, d), lambda i: (0, 0)),
            pl.BlockSpec((d, d), lambda i: (0, 0)),
        ],
        out_specs=pl.BlockSpec((bm, d), lambda i: (i, 0)),
        out_shape=jax.ShapeDtypeStruct((t, d), F32),
        scratch_shapes=[pltpu.VMEM((bm, d), BF16)],
        compiler_params=_cparams(("arbitrary",)),
        name="out_proj_a",
    )(x, gate, y_ctx, y_dec, u, v, w_sp, b_full, w1, w2)


def _out_c_kernel(x_ref, gate_ref, yc_ref, yd_ref, w_ref, o_ref, *, select):
    out = jnp.dot(select(yc_ref, yd_ref), w_ref[...], preferred_element_type=F32)
    o_ref[...] = x_ref[...] + gate_ref[...] * out


def _out_proj_c(x, gate, y_ctx, y_dec, w, tok, *, bm):
    t, d = x.shape
    kdim = y_ctx.shape[1]
    group = tok.group_of_tile(bm)
    y_specs, select = _mixer_rows(tok, bm)
    return pl.pallas_call(
        functools.partial(_out_c_kernel, select=select),
        grid=(t // bm,),
        in_specs=[
            pl.BlockSpec((bm, d), lambda i: (i, 0)),
            pl.BlockSpec((None, 1, d), lambda i: (group(i), 0, 0)),
            *y_specs(kdim),
            pl.BlockSpec((kdim, d), lambda i: (0, 0)),
        ],
        out_specs=pl.BlockSpec((bm, d), lambda i: (i, 0)),
        out_shape=jax.ShapeDtypeStruct((t, d), F32),
        compiler_params=_cparams(("arbitrary",)),
        name="out_proj_c",
    )(x, gate, y_ctx, y_dec, w)


def _split_kernel(w_ref, g_ref, l_ref):
    w = w_ref[...]
    k, n2 = w.shape
    half = LANES // 2
    lane = lax.broadcasted_iota(jnp.int32, (k, LANES), 1)
    first = lane < half
    idx = jnp.where(first, 2 * lane, 2 * (lane - half) + 1)
    gs, ls = [], []
    for j in range(n2 // (2 * LANES)):
        a = jnp.take_along_axis(w[:, (2 * j) * LANES:(2 * j + 1) * LANES], idx, axis=1)
        b = jnp.take_along_axis(w[:, (2 * j + 1) * LANES:(2 * j + 2) * LANES], idx, axis=1)
        gs.append(jnp.where(first, a, pltpu.roll(b, half, axis=1)))
        ls.append(jnp.where(first, pltpu.roll(a, half, axis=1), b))
    g_ref[...] = jnp.concatenate(gs, axis=1).astype(BF16)
    l_ref[...] = jnp.concatenate(ls, axis=1).astype(BF16)


def _split_gate_lin(w_gu):
    dl, e, k, n2 = w_gu.shape
    tn = 512
    spec_out = pl.BlockSpec((None, None, k, tn), lambda a, b, j: (a, b, 0, j))
    return pl.pallas_call(
        _split_kernel,
        grid=(dl, e, n2 // (2 * tn)),
        in_specs=[pl.BlockSpec((None, None, k, 2 * tn), lambda a, b, j: (a, b, 0, j))],
        out_specs=[spec_out, spec_out],
        out_shape=[jax.ShapeDtypeStruct((dl, e, k, n2 // 2), BF16)] * 2,
        compiler_params=_cparams(("arbitrary",) * 3),
        name="split_gate_lin",
    )(w_gu)


def _router_kernel(x_ref, gam_ref, sc_ref, sh_ref, wr_ref, br_ref, h_ref, e_ref, r_ref, g_ref, cnt_ref, cnt_scr):
    i = pl.program_id(0)
    bm = x_ref.shape[0]

    @pl.when(i == 0)
    def _():
        cnt_scr[...] = jnp.zeros_like(cnt_scr)

    h = _modulated_norm(x_ref[...], gam_ref[...], sc_ref[...], sh_ref[...])
    h_ref[...] = h
    logits = jnp.dot(h, wr_ref[...], precision=HI, preferred_element_type=F32) + br_ref[...]
    lane = lax.broadcasted_iota(jnp.int32, logits.shape, 1).astype(F32)
    vals, idxs = [], []
    work = logits
    for _ in range(TOP_K):
        m = jnp.max(work, axis=-1, keepdims=True)
        idx = jnp.min(jnp.where(work == m, lane, float(LANES)), axis=-1, keepdims=True)
        vals.append(m)
        idxs.append(idx)
        work = jnp.where(lane == idx, -jnp.inf, work)
    exps = [jnp.exp(v - vals[0]) for v in vals]
    inv = 1.0 / functools.reduce(lambda a, b: a + b, exps)
    hot = functools.reduce(jnp.logical_or, [lane == idx for idx in idxs])
    hot_f = hot.astype(F32)
    ri = lax.broadcasted_iota(jnp.int32, (bm, bm), 0)
    ci = lax.broadcasted_iota(jnp.int32, (bm, bm), 1)
    before = (ci < ri).astype(BF16)
    rank_all = cnt_scr[...] + jnp.dot(before, hot_f.astype(BF16), preferred_element_type=F32)
    e_out = jnp.zeros(logits.shape, F32)
    r_out = jnp.zeros(logits.shape, F32)
    g_out = jnp.zeros(logits.shape, F32)
    for k in range(TOP_K):
        rk = jnp.sum(jnp.where(lane == idxs[k], rank_all, 0.0), axis=-1, keepdims=True)
        e_out = jnp.where(lane == float(k), idxs[k], e_out)
        r_out = jnp.where(lane == float(k), rk, r_out)
        g_out = jnp.where(lane == float(k), exps[k] * inv, g_out)
    e_ref[...] = e_out.astype(jnp.int32)
    r_ref[...] = r_out.astype(jnp.int32)
    g_ref[...] = g_out
    cnt_scr[...] = cnt_scr[...] + jnp.sum(hot_f, axis=0, keepdims=True)
    cnt_ref[...] = cnt_scr[...]


def _router(x, gam, sc, sh, w_r, b_r, tok, *, bm):
    t, d = x.shape
    group = tok.group_of_tile(bm)
    row_spec = pl.BlockSpec((bm, LANES), lambda i: (i, 0))
    return pl.pallas_call(
        _router_kernel,
        grid=(t // bm,),
        in_specs=[
            pl.BlockSpec((bm, d), lambda i: (i, 0)),
            pl.BlockSpec((1, d), lambda i: (0, 0)),
            pl.BlockSpec((None, 1, d), lambda i: (group(i), 0, 0)),
            pl.BlockSpec((None, 1, d), lambda i: (group(i), 0, 0)),
            pl.BlockSpec((d, LANES), lambda i: (0, 0)),
            pl.BlockSpec((1, LANES), lambda i: (0, 0)),
        ],
        out_specs=[pl.BlockSpec((bm, d), lambda i: (i, 0)), row_spec, row_spec, row_spec,
                   pl.BlockSpec((1, LANES), lambda i: (0, 0))],
        out_shape=[jax.ShapeDtypeStruct((t, d), F32), jax.ShapeDtypeStruct((t, LANES), jnp.int32),
                   jax.ShapeDtypeStruct((t, LANES), jnp.int32), jax.ShapeDtypeStruct((t, LANES), F32),
                   jax.ShapeDtypeStruct((1, LANES), F32)],
        scratch_shapes=[pltpu.VMEM((1, LANES), F32)],
        compiler_params=_cparams(("arbitrary",)),
        name="moe_router",
    )(x, gam, sc, sh, w_r, b_r)


SLABS = D_MODEL // LANES


def _to_row_tiles(ref, x):
    for s in range(SLABS):
        ref[:, s, :] = x[:, s * LANES:(s + 1) * LANES]


def _from_row_tiles(ref):
    return jnp.concatenate([ref[:, s, :] for s in range(SLABS)], axis=1)


def _dispatch_kernel(pe_ref, dest_ref, h_ref, xs_hbm, stage, zbuf, sem, zsem):
    bm = h_ref.shape[0]

    @pl.when(pl.program_id(0) == 0)
    def _():
        zbuf[...] = jnp.zeros_like(zbuf)

        def tail_copy(e):
            start = pl.multiple_of(pe_ref[e + 1] - MOE_ROWS, MOE_ROWS)
            return pltpu.make_async_copy(zbuf, xs_hbm.at[pl.ds(start, MOE_ROWS)], zsem)

        for e in range(N_EXPERTS):
            @pl.when(pe_ref[e + 1] > pe_ref[e])
            def _():
                tail_copy(e).start()
        for e in range(N_EXPERTS):
            @pl.when(pe_ref[e + 1] > pe_ref[e])
            def _():
                tail_copy(e).wait()

        def spare_copy(b):
            return pltpu.make_async_copy(zbuf, xs_hbm.at[pl.ds(pl.multiple_of(b * MOE_ROWS, MOE_ROWS), MOE_ROWS)], zsem)

        first_spare = pe_ref[N_EXPERTS] // MOE_ROWS
        n_blk = xs_hbm.shape[0] // MOE_ROWS
        lax.fori_loop(first_spare, n_blk, lambda b, c: (spare_copy(b).start(), c)[1], 0)
        lax.fori_loop(first_spare, n_blk, lambda b, c: (spare_copy(b).wait(), c)[1], 0)

    i = pl.program_id(0)
    slot = i % 2
    _to_row_tiles(stage.at[slot], h_ref[...])

    def issue(r, carry):
        for k in range(TOP_K):
            pltpu.make_async_copy(stage.at[slot, r], xs_hbm.at[dest_ref[r * TOP_K + k]],
                                  sem.at[slot]).start(priority=k % 2)
        return carry

    lax.fori_loop(0, bm, issue, 0, unroll=2)

    def drain(s):
        for k in range(TOP_K):
            pltpu.make_async_copy(stage.at[s], xs_hbm.at[pl.ds(0, bm)], sem.at[s]).wait()

    @pl.when(i >= 1)
    def _():
        drain(1 - slot)

    @pl.when(i == pl.num_programs(0) - 1)
    def _():
        drain(slot)


def _dispatch(pad_bounds, dest, h, n_rows, *, bm):
    t, d = h.shape
    n_tiles = t // bm
    grid_spec = pltpu.PrefetchScalarGridSpec(
        num_scalar_prefetch=1,
        grid=(n_tiles,),
        in_specs=[
            pl.BlockSpec((None, None, bm * TOP_K), lambda i, pe: (i, 0, 0), memory_space=pltpu.SMEM),
            pl.BlockSpec((bm, d), lambda i, pe: (i, 0)),
        ],
        out_specs=pl.BlockSpec(memory_space=pl.ANY),
        scratch_shapes=[pltpu.VMEM((2, bm, SLABS, LANES), F32), pltpu.VMEM((MOE_ROWS, SLABS, LANES), F32),
                        pltpu.SemaphoreType.DMA((2,)), pltpu.SemaphoreType.DMA],
    )
    return pl.pallas_call(
        _dispatch_kernel,
        grid_spec=grid_spec,
        out_shape=jax.ShapeDtypeStruct((n_rows, SLABS, LANES), F32),
        compiler_params=_cparams(("arbitrary",)),
        name="moe_dispatch",
    )(pad_bounds, dest.reshape(n_tiles, 1, bm * TOP_K), h)


def _expert_kernel(be_ref, na_ref, xs_hbm, wg_ref, wl_ref, bg_ref, bl_ref, wd_ref, bd_ref, ys_hbm,
                   xbuf, ybuf, zbuf, wd_s, sem_in, sem_out, zsem):
    i = pl.program_id(0)
    n_act = na_ref[0]
    slot = i % 2

    def rows(b):
        return pl.ds(pl.multiple_of(b * MOE_ROWS, MOE_ROWS), MOE_ROWS)

    def in_copy(b, s, j):
        return pltpu.make_async_copy(xs_hbm.at[rows(b), j, :], xbuf.at[s, :, pl.ds(j * LANES, LANES)], sem_in.at[s])

    def out_copy(b, s, j):
        return pltpu.make_async_copy(ybuf.at[s, :, pl.ds(j * LANES, LANES)], ys_hbm.at[rows(b), j, :], sem_out.at[s])

    @pl.when(i < n_act)
    def _():
        @pl.when(i == 0)
        def _():
            for j in range(SLABS):
                in_copy(0, 0, j).start()

        @pl.when(i + 1 < n_act)
        def _():
            for j in range(SLABS):
                in_copy(i + 1, 1 - slot, j).start()

        for j in range(SLABS):
            in_copy(i, slot, j).wait()

        @pl.when(i >= 2)
        def _():
            for j in range(SLABS):
                out_copy(i - 2, slot, j).wait()

        @pl.when(jnp.logical_or(i == 0, be_ref[i] != be_ref[jnp.maximum(i - 1, 0)]))
        def _():
            wd_s[...] = wd_ref[...].astype(BF16)

        x = xbuf[slot].astype(BF16)
        hg = jnp.dot(x, wg_ref[...], preferred_element_type=F32) + bg_ref[...]
        hl = jnp.dot(x, wl_ref[...], preferred_element_type=F32) + bl_ref[...]
        glu = jnp.minimum(hg, SWIGLU_LIMIT)
        lin = jnp.clip(hl, -SWIGLU_LIMIT, SWIGLU_LIMIT)
        act = glu * (1.0 / (1.0 + jnp.exp(-SWIGLU_ALPHA * glu))) * (lin + 1.0)
        ybuf[slot] = jnp.dot(act.astype(BF16), wd_s[...], preferred_element_type=F32) + bd_ref[...]
        for j in range(SLABS):
            out_copy(i, slot, j).start()

        @pl.when(i == n_act - 1)
        def _():
            for j in range(SLABS):
                out_copy(i, slot, j).wait()

            @pl.when(i >= 1)
            def _():
                for j in range(SLABS):
                    out_copy(i - 1, 1 - slot, j).wait()

    @pl.when(i >= n_act)
    def _():
        @pl.when(i == n_act)
        def _():
            zbuf[...] = jnp.zeros_like(zbuf)

        fill = pltpu.make_async_copy(zbuf, ys_hbm.at[rows(i)], zsem)
        fill.start()
        fill.wait()


def _experts(blk_expert, n_active, xs, wg, wl, bg, bl, wd, bd, layer):
    n_rows = xs.shape[0]
    d, dff = wg.shape[2], wg.shape[3]
    n_blk = n_rows // MOE_ROWS
    wmap = lambda i, be, na: (layer, be[i], 0, 0)
    grid_spec = pltpu.PrefetchScalarGridSpec(
        num_scalar_prefetch=2,
        grid=(n_blk,),
        in_specs=[
            pl.BlockSpec(memory_space=pl.ANY),
            pl.BlockSpec((None, None, d, dff), wmap),
            pl.BlockSpec((None, None, d, dff), wmap),
            pl.BlockSpec((None, None, 1, dff), wmap),
            pl.BlockSpec((None, None, 1, dff), wmap),
            pl.BlockSpec((None, None, dff, d), wmap),
            pl.BlockSpec((None, None, 1, d), wmap),
        ],
        out_specs=pl.BlockSpec(memory_space=pl.ANY),
        scratch_shapes=[pltpu.VMEM((2, MOE_ROWS, d), F32), pltpu.VMEM((2, MOE_ROWS, d), F32),
                        pltpu.VMEM((MOE_ROWS, SLABS, LANES), F32), pltpu.VMEM((dff, d), BF16),
                        pltpu.SemaphoreType.DMA((2,)), pltpu.SemaphoreType.DMA((2,)), pltpu.SemaphoreType.DMA],
    )
    return pl.pallas_call(
        _expert_kernel,
        grid_spec=grid_spec,
        out_shape=jax.ShapeDtypeStruct((n_rows, SLABS, LANES), F32),
        compiler_params=_cparams(("arbitrary",)),
        name="moe_experts",
    )(blk_expert, n_active, xs, wg, wl, bg, bl, wd, bd)


def _combine_kernel(dest_ref, next_ref, g_ref, x_ref, gate_ref, gam_ref, ys_hbm, o_ref, *rest, final_norm, n_ctx_tiles):
    if final_norm:
        of_ref, buf, acc_scr, sem = rest
    else:
        buf, acc_scr, sem = rest
    bm = x_ref.shape[0]
    i = pl.program_id(0)
    slot = i % 2

    def gather(idx_ref, s):
        def issue(r, carry):
            for k in range(TOP_K):
                pltpu.make_async_copy(ys_hbm.at[idx_ref[r * TOP_K + k]], buf.at[s, k, r],
                                      sem.at[s]).start(priority=k % 2)
            return carry

        lax.fori_loop(0, bm, issue, 0, unroll=2)

    @pl.when(i == 0)
    def _():
        gather(dest_ref, 0)

    @pl.when(i + 1 < pl.num_programs(0))
    def _():
        gather(next_ref, 1 - slot)

    for k in range(TOP_K):
        pltpu.make_async_copy(ys_hbm.at[pl.ds(0, bm)], buf.at[slot, k], sem.at[slot]).wait()

    def mix(r, carry):
        acc = g_ref[r * TOP_K] * buf[slot, 0, r]
        for k in range(1, TOP_K):
            acc = acc + g_ref[r * TOP_K + k] * buf[slot, k, r]
        acc_scr[r] = acc
        return carry

    lax.fori_loop(0, bm, mix, 0, unroll=8)
    xn = x_ref[...] + gate_ref[...] * _from_row_tiles(acc_scr)
    if not final_norm:
        o_ref[...] = xn
    else:
        ms = jnp.mean(xn * xn, axis=-1, keepdims=True)
        yn = xn * lax.rsqrt(ms + EPS) * gam_ref[...]
        is_ctx = pl.program_id(0) < n_ctx_tiles

        @pl.when(is_ctx)
        def _():
            o_ref[...] = yn

        @pl.when(jnp.logical_not(is_ctx))
        def _():
            of_ref[...] = yn


def _combine(dest, x, gate, gates, gam_final, ys, tok, *, bm, final_norm):
    t, d = x.shape
    group = tok.group_of_tile(bm)
    n_tiles = t // bm
    n_ctx_tiles = tok.t_ctx // bm
    dest_tiles = dest.reshape(n_tiles, 1, bm * TOP_K)
    if final_norm:
        out_shape = [jax.ShapeDtypeStruct((tok.t_ctx, d), F32), jax.ShapeDtypeStruct((tok.t_dec, d), F32)]
        out_specs = [pl.BlockSpec((bm, d), lambda i: (jnp.minimum(i, n_ctx_tiles - 1), 0)),
                     pl.BlockSpec((bm, d), lambda i: (jnp.maximum(i - n_ctx_tiles, 0), 0))]
    else:
        out_shape = [jax.ShapeDtypeStruct((t, d), F32)]
        out_specs = [pl.BlockSpec((bm, d), lambda i: (i, 0))]
    return pl.pallas_call(
        functools.partial(_combine_kernel, final_norm=final_norm, n_ctx_tiles=n_ctx_tiles),
        grid=(n_tiles,),
        in_specs=[
            pl.BlockSpec((None, None, bm * TOP_K), lambda i: (i, 0, 0), memory_space=pltpu.SMEM),
            pl.BlockSpec((None, None, bm * TOP_K), lambda i: (jnp.minimum(i + 1, n_tiles - 1), 0, 0),
                         memory_space=pltpu.SMEM),
            pl.BlockSpec((None, None, bm * TOP_K), lambda i: (i, 0, 0), memory_space=pltpu.SMEM),
            pl.BlockSpec((bm, d), lambda i: (i, 0)),
            pl.BlockSpec((None, 1, d), lambda i: (group(i), 0, 0)),
            pl.BlockSpec((1, d), lambda i: (0, 0)),
            pl.BlockSpec(memory_space=pl.ANY),
        ],
        out_specs=out_specs,
        out_shape=out_shape,
        scratch_shapes=[pltpu.VMEM((2, TOP_K, bm, SLABS, LANES), F32), pltpu.VMEM((bm, SLABS, LANES), F32),
                        pltpu.SemaphoreType.DMA((2,))],
        compiler_params=_cparams(("arbitrary",)),
        name="moe_combine_final" if final_norm else "moe_combine",
    )(dest_tiles, dest_tiles, gates[:, :TOP_K].reshape(n_tiles, 1, bm * TOP_K), x, gate, gam_final, ys)


def _moe(x, gam, sc, sh, gate, w_r, b_r, wg, wl, bg, bl, wd, bd, layer, gam_final, tok, *, final_norm):
    t, d = x.shape
    h, e_out, r_out, gates, counts = _router(x, gam, sc, sh, w_r, b_r, tok, bm=512)
    counts = counts[0, :N_EXPERTS].astype(jnp.int32)
    padded = (counts + MOE_ROWS - 1) // MOE_ROWS * MOE_ROWS
    pad_end = jnp.cumsum(padded)
    pad_start = pad_end - padded
    e_sel = e_out[:, :TOP_K]
    dest = pad_start[e_sel] + r_out[:, :TOP_K]
    n_rows = t * TOP_K + N_EXPERTS * MOE_ROWS
    n_blk = n_rows // MOE_ROWS
    blk_start = jnp.arange(n_blk, dtype=jnp.int32) * MOE_ROWS
    blk_expert = jnp.minimum(jnp.sum((pad_end[None, :] <= blk_start[:, None]).astype(jnp.int32), axis=1),
                             N_EXPERTS - 1)
    n_active = (pad_end[-1:] // MOE_ROWS).astype(jnp.int32)
    pad_bounds = jnp.concatenate([jnp.zeros((1,), jnp.int32), pad_end.astype(jnp.int32)])
    xs = _dispatch(pad_bounds, dest, h, n_rows, bm=256)
    ys = _experts(blk_expert, n_active, xs, wg, wl, bg, bl, wd, bd, layer)
    return _combine(dest, x, gate, gates, gam_final, ys, tok, bm=256, final_norm=final_norm)


def _rope_tables(tok):
    rows = tok.dec_len // GRID_W
    r = jnp.repeat(jnp.arange(rows), GRID_W).astype(F32)
    cidx = jnp.tile(jnp.arange(GRID_W), rows).astype(F32)
    n_freq = DK_RET // 4
    inv = ROPE_BASE ** (-jnp.arange(n_freq, dtype=F32) / n_freq)
    ang = jnp.concatenate([r[:, None] * inv, cidx[:, None] * inv], axis=-1)
    cos = jnp.concatenate([jnp.ones((tok.t_ctx, DK_RET // 2), F32), jnp.tile(jnp.cos(ang), (tok.n_dec_seq, 1))])
    sin = jnp.concatenate([jnp.zeros((tok.t_ctx, DK_RET // 2), F32), jnp.tile(jnp.sin(ang), (tok.n_dec_seq, 1))])
    return cos, sin


def _group_cols(p):
    h = p.shape[1]
    a = p.reshape(2, SSD_GROUPS, h // SSD_GROUPS).transpose(1, 0, 2).reshape(SSD_GROUPS, -1)
    return jnp.pad(a, ((0, 0), (0, LANES - a.shape[1])))[:, None, :]


def kernel(x_prompt, x_sample, state_ssd, state_ret, c, c_ctx, w_mod, b_mod, norm_mix, norm_ffn, w_in_a, conv_w, conv_b, dt_bias, a_log, d_skip, ssd_norm, w_sp, b_sp, w_out_a, w_in_c, decay_logit, w_out_c, w_router, b_router, w_gu, b_gu, w_down, b_down, norm_final):
    n_ctx, ctx_len, d = x_prompt.shape
    n_dec, dec_len, _ = x_sample.shape
    tok = _Tokens(n_ctx, ctx_len, n_dec, dec_len)
    depth = w_mod.shape[0]
    x = jnp.concatenate([x_prompt.reshape(tok.t_ctx, d), x_sample.reshape(tok.t_dec, d)])

    cvecs = jnp.concatenate([c_ctx[None], c, jnp.zeros((MOD_ROWS - 1 - n_dec, d), F32)])
    mod = _modulation(cvecs, w_mod, b_mod)
    mod = mod.reshape(depth, MOD_ROWS, 6, 1, d).transpose(0, 2, 1, 3, 4)

    h_ssd = a_log.shape[2]
    xbc_w = d + 2 * SSD_GROUPS * D_STATE
    o1, o2, o3 = d, d + xbc_w, d + xbc_w + 2 * h_ssd
    cos, sin = _rope_tables(tok)
    wg_all, wl_all = _split_gate_lin(w_gu)
    bg_all = b_gu[:, :, None, 0::2]
    bl_all = b_gu[:, :, None, 1::2]
    wd_all = w_down
    bd_all = b_down[:, :, None, :]
    new_ssd, new_ret = [], []
    y_final = None
    bm_proj = 512

    for l in range(depth):
        sh1, sc1, g1, sh2, sc2, g2 = (mod[l, j] for j in range(6))
        gam_mix = norm_mix[l][None]
        i = l // 2
        if l % 2 == 0:
            w_in = w_in_a[i]
            w_z = w_in[:, :o1].astype(BF16)
            w_xbc = w_in[:, o1:o2].astype(BF16)
            w_dt = w_in[:, o2:o3].reshape(d, 2, SSD_GROUPS, h_ssd // SSD_GROUPS).transpose(0, 2, 1, 3)
            w_dt = jnp.pad(w_dt.reshape(d, SSD_GROUPS, -1), ((0, 0), (0, 0), (0, LANES - 2 * h_ssd // SSD_GROUPS)))
            w_dt = w_dt.reshape(d, SSD_GROUPS * LANES).astype(BF16)
            w_uv = w_in[:, o3:].astype(BF16)
            xbc = _norm_proj(x, gam_mix, sc1, sh1, w_xbc, tok=tok, bm=dec_len, tn=256, out_dtype=BF16,
                             epilogue=_ep_conv_silu, extra=(conv_w[i], conv_b[i][None]),
                             extra_specs=(pl.BlockSpec((4, 256), lambda r, j: (0, j)),
                                          pl.BlockSpec((1, 256), lambda r, j: (0, j))), name="proj_xbc")
            dtb = _group_cols(dt_bias[i]).reshape(1, SSD_GROUPS * LANES)
            sz, dt, u, v = _norm_proj_multi(
                x, gam_mix, sc1, sh1, jnp.concatenate([w_z, w_dt, w_uv], axis=1), tok,
                [(d, 512, BF16, _ep_silu, False),
                 (SSD_GROUPS * LANES, SSD_GROUPS * LANES, F32, _ep_softplus_bias, True),
                 (d, 512, BF16, _ep_gelu, False), (d, d, BF16, _ep_gelu_ln, False)],
                bm=bm_proj, extra=(dtb,), extra_specs=(pl.BlockSpec((1, SSD_GROUPS * LANES), lambda r: (0, 0)),),
                name="proj_zdtuv")
            a_neg = _group_cols(-jnp.exp(a_log[i]))
            dsk = jnp.repeat(d_skip[i], SSD_HEAD).reshape(SSD_GROUPS, 1, GROUP_W)
            nrm = ssd_norm[i].reshape(SSD_GROUPS, 1, GROUP_W)
            s0 = state_ssd[:, i].reshape(n_dec, 2, SSD_GROUPS, HEADS_PER_GROUP, D_STATE, SSD_HEAD)
            s0 = s0.transpose(0, 2, 1, 4, 3, 5).reshape(n_dec, SSD_GROUPS, 2, D_STATE, GROUP_W)
            y_ctx, s_fin = _ssd_scan(xbc, dt, sz, a_neg, dsk, nrm, None, row0=0, n_seq=n_ctx,
                                     seq_len=ctx_len, want_final=True)
            (y_dec,) = _ssd_scan(xbc, dt, sz, a_neg, dsk, nrm, s0, row0=tok.t_ctx, n_seq=n_dec,
                                 seq_len=dec_len, want_final=False)
            s_fin = s_fin.reshape(n_ctx, SSD_GROUPS, 2, D_STATE, HEADS_PER_GROUP, SSD_HEAD)
            new_ssd.append(s_fin.transpose(0, 2, 1, 4, 3, 5).reshape(n_ctx, 2, h_ssd, D_STATE, SSD_HEAD))
            b_full = jnp.repeat(b_sp[i].T, LANES, axis=1)
            w_o = w_out_a[i].astype(BF16)
            x = _out_proj_a(x, g1, y_ctx, y_dec, u, v, w_sp[i].astype(BF16), b_full, w_o[:d], w_o[d:], tok, bm=512)
        else:
            hk = H_RET * DK_RET
            hv = H_RET * DV_RET
            w_in = w_in_c[i].astype(BF16)
            rope_specs = (pl.BlockSpec((bm_proj, DK_RET // 2), lambda r: (r, 0)),) * 2
            q, kk, vv, sg = _norm_proj_multi(
                x, gam_mix, sc1, sh1, w_in, tok,
                [(hk, 512, BF16, functools.partial(_ep_rope, scale=1.0), True),
                 (hk, 512, BF16, functools.partial(_ep_rope, scale=DK_RET ** -0.5), True),
                 (hv, 512, BF16, _ep_plain, False), (hv, 512, BF16, _ep_silu, False)],
                bm=bm_proj, extra=(cos, sin), extra_specs=rope_specs, name="proj_qkvg")
            log_g = jax.nn.log_sigmoid(decay_logit[i].astype(F32))
            y_ctx, s_fin = _ret_scan(log_g, q, kk, vv, sg, None, row0=0, n_seq=n_ctx, seq_len=ctx_len, want_final=True)
            (y_dec,) = _ret_scan(log_g, q, kk, vv, sg, state_ret[:, i], row0=tok.t_ctx, n_seq=n_dec, seq_len=dec_len,
                                 want_final=False)
            new_ret.append(s_fin)
            x = _out_proj_c(x, g1, y_ctx, y_dec, w_out_c[i].astype(BF16), tok, bm=512)

        w_r = jnp.pad(w_router[l], ((0, 0), (0, LANES - N_EXPERTS)))
        b_r = jnp.pad(b_router[l], (0, LANES - N_EXPERTS), constant_values=-1e30)[None]
        last = l == depth - 1
        res = _moe(x, norm_ffn[l][None], sc2, sh2, g2, w_r, b_r, wg_all, wl_all, bg_all, bl_all, wd_all, bd_all, l,
                   norm_final[None], tok, final_norm=last)
        if last:
            y_final = res
        else:
            x = res[0]

    y_prompt = y_final[0].reshape(n_ctx, ctx_len, d)
    y_sample = y_final[1].reshape(n_dec, dec_len, d)
    return (y_prompt, y_sample, jnp.stack(new_ssd, axis=1), jnp.stack(new_ret, axis=1))
```

```python
import functools
import math

import jax
import jax.numpy as jnp
from jax import lax
from jax.experimental import pallas as pl
from jax.experimental.pallas import tpu as pltpu

D_MODEL = 1024
GRID_W = 64
CHUNK = 128
RET_CHUNK = 256
SSD_HEAD = 64
SSD_GROUPS = 2
D_STATE = 128
GROUP_W = D_MODEL // SSD_GROUPS
HEADS_PER_GROUP = GROUP_W // SSD_HEAD
SG_GROUPS = 8
H_RET = 4
DK_RET = D_MODEL // H_RET
DV_RET = 2 * DK_RET
ROPE_BASE = 10000.0
N_EXPERTS = 32
TOP_K = 4
SWIGLU_LIMIT = 7.0
SWIGLU_ALPHA = 1.702
EPS = 1e-6

LANES = 128
MOD_ROWS = 8
MOE_ROWS = 512
VMEM_LIMIT = 56 * 1024 * 1024

F32 = jnp.float32
BF16 = jnp.bfloat16
HI = lax.Precision.HIGHEST


def _cparams(sem):
    return pltpu.CompilerParams(dimension_semantics=sem, vmem_limit_bytes=VMEM_LIMIT)


def _silu(x):
    return x * (1.0 / (1.0 + jnp.exp(-x)))


def _gelu_tanh(x):
    return 0.5 * x * (1.0 + jnp.tanh(math.sqrt(2.0 / math.pi) * (x + 0.044715 * (x * x * x))))


def _softplus(x):
    return jnp.maximum(x, 0.0) + jnp.log(1.0 + jnp.exp(-jnp.abs(x)))


def _mod_kernel(c_ref, w_ref, b_ref, o_ref):
    a = _silu(c_ref[...])
    o_ref[...] = jnp.dot(a, w_ref[...], precision=HI, preferred_element_type=F32) + b_ref[...]


def _modulation(cvecs, w_mod, b_mod):
    depth, d, n = w_mod.shape
    tn = 1536
    return pl.pallas_call(
        _mod_kernel,
        grid=(depth, n // tn),
        in_specs=[
            pl.BlockSpec((MOD_ROWS, d), lambda l, j: (0, 0)),
            pl.BlockSpec((None, d, tn), lambda l, j: (l, 0, j)),
            pl.BlockSpec((None, 1, tn), lambda l, j: (l, 0, j)),
        ],
        out_specs=pl.BlockSpec((None, MOD_ROWS, tn), lambda l, j: (l, 0, j)),
        out_shape=jax.ShapeDtypeStruct((depth, MOD_ROWS, n), F32),
        compiler_params=_cparams(("arbitrary", "arbitrary")),
        name="modulation",
    )(cvecs, w_mod, b_mod.reshape(depth, 1, n))


class _Tokens:
    def __init__(self, n_ctx_seq, ctx_len, n_dec_seq, dec_len):
        self.n_ctx_seq, self.ctx_len = n_ctx_seq, ctx_len
        self.n_dec_seq, self.dec_len = n_dec_seq, dec_len
        self.t_ctx = n_ctx_seq * ctx_len
        self.t_dec = n_dec_seq * dec_len
        self.total = self.t_ctx + self.t_dec

    def group_of_tile(self, bm):
        assert self.t_ctx % bm == 0 and self.dec_len % bm == 0
        n_ctx_tiles = self.t_ctx // bm
        per_seq = self.dec_len // bm

        def group(i):
            return jnp.where(i < n_ctx_tiles, 0, 1 + (i - n_ctx_tiles) // per_seq)

        return group


def _modulated_norm(x, gam, sc, sh):
    ms = jnp.mean(x * x, axis=-1, keepdims=True)
    return (x * lax.rsqrt(ms + EPS) * gam) * (1.0 + sc) + sh


def _proj_kernel(*refs, epilogue, n_extra, tok, bm):
    x_ref, gam_ref, sc_ref, sh_ref, w_ref = refs[:5]
    extra = refs[5:5 + n_extra]
    o_ref = refs[5 + n_extra]
    h_scr = refs[6 + n_extra]
    i = pl.program_id(0)

    @pl.when(pl.program_id(1) == 0)
    def _():
        h_scr[...] = _modulated_norm(x_ref[...], gam_ref[...], sc_ref[...], sh_ref[...]).astype(BF16)

    acc = jnp.dot(h_scr[...], w_ref[...], preferred_element_type=F32)
    o_ref[...] = epilogue(acc, i, tok, bm, *extra).astype(o_ref.dtype)


def _ep_plain(acc, i, tok, bm):
    return acc


def _ep_silu(acc, i, tok, bm):
    return _silu(acc)


def _ep_gelu(acc, i, tok, bm):
    return _gelu_tanh(acc)


def _ep_softplus_bias(acc, i, tok, bm, bias_ref):
    return _softplus(acc + bias_ref[...])


def _ep_gelu_ln(acc, i, tok, bm):
    g = _gelu_tanh(acc)
    mu = jnp.mean(g, axis=-1, keepdims=True)
    gc = g - mu
    return gc * lax.rsqrt(jnp.mean(gc * gc, axis=-1, keepdims=True) + 1e-5)


def _ep_conv_silu(acc, i, tok, bm, cw_ref, cb_ref):
    n = acc.shape[0]
    seq = jnp.where(i * bm < tok.t_ctx, tok.ctx_len, tok.dec_len)
    t = lax.broadcasted_iota(jnp.int32, (n, 1), 0) & (seq - 1)
    cw = cw_ref[...]
    y = acc * cw[2:3, :] + cb_ref[...]
    y = y + jnp.where(t >= 2, pltpu.roll(acc, 2, axis=0), 0.0) * cw[0:1, :]
    y = y + jnp.where(t >= 1, pltpu.roll(acc, 1, axis=0), 0.0) * cw[1:2, :]
    y = y + jnp.where(t < seq - 1, pltpu.roll(acc, n - 1, axis=0), 0.0) * cw[3:4, :]
    return _silu(y)


def _ep_rope(acc, i, tok, bm, cos_ref, sin_ref, *, scale):
    cs, sn = cos_ref[...], sin_ref[...]
    half = DK_RET // 2
    outs = []
    for h in range(acc.shape[1] // DK_RET):
        x1 = acc[:, h * DK_RET:h * DK_RET + half] * scale
        x2 = acc[:, h * DK_RET + half:(h + 1) * DK_RET] * scale
        outs.append(x1 * cs - x2 * sn)
        outs.append(x2 * cs + x1 * sn)
    return jnp.concatenate(outs, axis=1)


def _norm_proj(x, gam, sc, sh, w, tok, *, bm, tn, out_dtype, epilogue, extra=(), extra_specs=(), name):
    t, d = x.shape
    n = w.shape[1]
    group = tok.group_of_tile(bm)
    kern = functools.partial(_proj_kernel, epilogue=epilogue, n_extra=len(extra), tok=tok, bm=bm)
    return pl.pallas_call(
        kern,
        grid=(t // bm, n // tn),
        in_specs=[
            pl.BlockSpec((bm, d), lambda i, j: (i, 0)),
            pl.BlockSpec((1, d), lambda i, j: (0, 0)),
            pl.BlockSpec((None, 1, d), lambda i, j: (group(i), 0, 0)),
            pl.BlockSpec((None, 1, d), lambda i, j: (group(i), 0, 0)),
            pl.BlockSpec((d, tn), lambda i, j: (0, j)),
            *extra_specs,
        ],
        out_specs=pl.BlockSpec((bm, tn), lambda i, j: (i, j)),
        out_shape=jax.ShapeDtypeStruct((t, n), out_dtype),
        scratch_shapes=[pltpu.VMEM((bm, d), BF16)],
        compiler_params=_cparams(("arbitrary", "arbitrary")),
        name=name,
    )(x, gam, sc, sh, w, *extra)


def _proj_multi_kernel(*refs, pieces, n_extra, tok, bm):
    x_ref, gam_ref, sc_ref, sh_ref, w_ref = refs[:5]
    extra = refs[5:5 + n_extra]
    outs = refs[5 + n_extra:5 + n_extra + len(pieces)]
    h_scr = refs[5 + n_extra + len(pieces)]
    i = pl.program_id(0)
    h_scr[...] = _modulated_norm(x_ref[...], gam_ref[...], sc_ref[...], sh_ref[...]).astype(BF16)
    c0 = 0
    for (width, chunk, _, epilogue, uses_extra), o_ref in zip(pieces, outs):
        for j in range(width // chunk):
            acc = jnp.dot(h_scr[...], w_ref[:, c0 + j * chunk:c0 + (j + 1) * chunk], preferred_element_type=F32)
            res = epilogue(acc, i, tok, bm, *(extra if uses_extra else ()))
            o_ref[:, j * chunk:(j + 1) * chunk] = res.astype(o_ref.dtype)
        c0 += width


def _norm_proj_multi(x, gam, sc, sh, w, tok, pieces, *, bm, extra=(), extra_specs=(), name):
    t, d = x.shape
    n = w.shape[1]
    assert n == sum(p[0] for p in pieces)
    group = tok.group_of_tile(bm)
    kern = functools.partial(_proj_multi_kernel, pieces=tuple(pieces), n_extra=len(extra), tok=tok, bm=bm)
    return pl.pallas_call(
        kern,
        grid=(t // bm,),
        in_specs=[
            pl.BlockSpec((bm, d), lambda i: (i, 0)),
            pl.BlockSpec((1, d), lambda i: (0, 0)),
            pl.BlockSpec((None, 1, d), lambda i: (group(i), 0, 0)),
            pl.BlockSpec((None, 1, d), lambda i: (group(i), 0, 0)),
            pl.BlockSpec((d, n), lambda i: (0, 0), pipeline_mode=pl.Buffered(1)),
            *extra_specs,
        ],
        out_specs=[pl.BlockSpec((bm, p[0]), lambda i: (i, 0)) for p in pieces],
        out_shape=[jax.ShapeDtypeStruct((t, p[0]), p[2]) for p in pieces],
        scratch_shapes=[pltpu.VMEM((bm, d), BF16)],
        compiler_params=_cparams(("arbitrary",)),
        name=name,
    )(x, gam, sc, sh, w, *extra)


def _ssd_kernel(*refs, n_chunks, has_init, has_final):
    xs_ref, b_ref, c_ref, dt_ref, sz_ref, a_ref, dsk_ref, nrm_ref = refs[:8]
    k = 8
    s0_ref = None
    if has_init:
        s0_ref = refs[k]
        k += 1
    y_ref = refs[k]
    k += 1
    sfin_ref = None
    if has_final:
        sfin_ref = refs[k]
        k += 1
    y_scrs = (refs[k], refs[k + 1])
    s_scrs = (refs[k + 2], refs[k + 3])

    q = CHUNK
    row = lax.broadcasted_iota(jnp.int32, (q, q), 0)
    col = lax.broadcasted_iota(jnp.int32, (q, q), 1)
    lane = lax.broadcasted_iota(jnp.int32, (1, q), 1)
    left = lane < SSD_HEAD
    a_neg = a_ref[...]
    n_pairs = HEADS_PER_GROUP // 2
    keeps = (col <= row, col >= row)
    tris = tuple(kp.astype(F32).astype(BF16) for kp in keeps)

    def masked_sums(tri, x):
        hi = x.astype(BF16)
        r1 = x - hi.astype(F32)
        mid = r1.astype(BF16)
        lo = (r1 - mid.astype(F32)).astype(BF16)
        return (jnp.dot(tri, hi, preferred_element_type=F32) + jnp.dot(tri, mid, preferred_element_type=F32)
                + jnp.dot(tri, lo, preferred_element_type=F32))

    def chunk(ci, direction):
        s_scr = s_scrs[direction]
        r0 = pl.multiple_of(ci * q, q)
        xs = xs_ref[pl.ds(r0, q), :]
        bm_ = b_ref[pl.ds(r0, q), :]
        cm = c_ref[pl.ds(r0, q), :]
        dt = dt_ref[pl.ds(r0, q), :]
        la = dt * a_neg
        keep, tri = keeps[direction], tris[direction]
        last = q - 1 if direction == 0 else 0
        cum = masked_sums(tri, la)
        cum_t = cum.T
        dt_t = dt.T
        tot_t = jnp.broadcast_to(cum_t[:, last:last + 1], (q, q))
        w_t = dt_t * jnp.exp(tot_t - cum_t)
        g = lax.dot_general(cm, bm_, (((1,), (1,)), ((), ())), preferred_element_type=F32)
        b_t = bm_.astype(F32).T
        cm_f = cm.astype(F32)
        outs = []
        for p in range(n_pairs):
            xs_p = xs[:, p * LANES:(p + 1) * LANES]
            s_p = s_scr[:, p * LANES:(p + 1) * LANES]
            s_b = s_p.astype(BF16)
            zero = jnp.zeros_like(xs_p)
            zero_s = jnp.zeros_like(s_b)
            lhs, rhs, lhs_s, rhs_s, decs = [], [], [], [], []
            for hh in range(2):
                cidx = direction * HEADS_PER_GROUP + 2 * p + hh
                cum_b = jnp.broadcast_to(cum[:, cidx:cidx + 1], (q, q))
                dec = jnp.exp(jnp.where(keep, cum_b - cum_t[cidx:cidx + 1, :], -jnp.inf))
                scores = g * dec * dt_t[cidx:cidx + 1, :]
                lhs += [scores.astype(BF16), (cm_f * jnp.exp(cum_b)).astype(BF16)]
                sel = left if hh == 0 else jnp.logical_not(left)
                rhs += [jnp.where(sel, xs_p, zero), jnp.where(sel, s_b, zero_s)]
                lhs_s.append((b_t * w_t[cidx:cidx + 1, :]).astype(BF16))
                rhs_s.append(jnp.where(sel, xs_p, zero))
                decs.append(jnp.exp(cum_t[cidx:cidx + 1, last:last + 1]))
            y_p = jnp.dot(jnp.concatenate(lhs, axis=1), jnp.concatenate(rhs, axis=0),
                          preferred_element_type=F32)
            upd = jnp.dot(jnp.concatenate(lhs_s, axis=1), jnp.concatenate(rhs_s, axis=0),
                          preferred_element_type=F32)
            s_scr[:, p * LANES:(p + 1) * LANES] = s_p * jnp.where(left, decs[0], decs[1]) + upd
            outs.append(y_p)
        return jnp.concatenate(outs, axis=1)

    for direction in range(2):
        if has_init:
            s_scrs[direction][...] = s0_ref[direction]
        else:
            s_scrs[direction][...] = jnp.zeros_like(s_scrs[direction])

    def scan_body(j, carry):
        for direction, ci in ((0, j), (1, n_chunks - 1 - j)):
            r0 = pl.multiple_of(ci * q, q)
            y_scrs[direction][pl.ds(r0, q), :] = chunk(ci, direction)
        return carry

    lax.fori_loop(0, n_chunks, scan_body, 0)
    if has_final:
        for direction in range(2):
            sfin_ref[direction] = s_scrs[direction][...]

    def finish_body(ci, carry):
        r0 = pl.multiple_of(ci * q, q)
        y = y_scrs[0][pl.ds(r0, q), :] + y_scrs[1][pl.ds(r0, q), :]
        y = y + dsk_ref[...] * xs_ref[pl.ds(r0, q), :].astype(F32)
        y = y * sz_ref[pl.ds(r0, q), :].astype(F32)
        y = y * lax.rsqrt(jnp.mean(y * y, axis=-1, keepdims=True) + EPS)
        y_ref[pl.ds(r0, q), :] = (y * nrm_ref[...]).astype(y_ref.dtype)
        return carry

    lax.fori_loop(0, n_chunks, finish_body, 0)


def _ssd_scan(xbc, dt, sz, a_neg, dsk, nrm, s0, *, row0, n_seq, seq_len, want_final):
    assert row0 % seq_len == 0
    b0 = row0 // seq_len
    has_init = s0 is not None
    gw = GROUP_W
    b_blk0 = D_MODEL // D_STATE
    c_blk0 = b_blk0 + SSD_GROUPS
    in_specs = [
        pl.BlockSpec((seq_len, gw), lambda b, g: (b0 + b, g)),
        pl.BlockSpec((seq_len, D_STATE), lambda b, g: (b0 + b, b_blk0 + g)),
        pl.BlockSpec((seq_len, D_STATE), lambda b, g: (b0 + b, c_blk0 + g)),
        pl.BlockSpec((seq_len, LANES), lambda b, g: (b0 + b, g)),
        pl.BlockSpec((seq_len, gw), lambda b, g: (b0 + b, g)),
        pl.BlockSpec((None, 1, LANES), lambda b, g: (g, 0, 0)),
        pl.BlockSpec((None, 1, gw), lambda b, g: (g, 0, 0)),
        pl.BlockSpec((None, 1, gw), lambda b, g: (g, 0, 0)),
    ]
    args = [xbc, xbc, xbc, dt, sz, a_neg, dsk, nrm]
    if has_init:
        in_specs.append(pl.BlockSpec((None, None, 2, D_STATE, gw), lambda b, g: (b, g, 0, 0, 0)))
        args.append(s0)
    out_shape = [jax.ShapeDtypeStruct((n_seq * seq_len, D_MODEL), BF16)]
    out_specs = [pl.BlockSpec((seq_len, gw), lambda b, g: (b, g))]
    if want_final:
        out_shape.append(jax.ShapeDtypeStruct((n_seq, SSD_GROUPS, 2, D_STATE, gw), F32))
        out_specs.append(pl.BlockSpec((None, None, 2, D_STATE, gw), lambda b, g: (b, g, 0, 0, 0)))
    kern = functools.partial(_ssd_kernel, n_chunks=seq_len // CHUNK, has_init=has_init, has_final=want_final)
    return pl.pallas_call(
        kern,
        grid=(n_seq, SSD_GROUPS),
        in_specs=in_specs,
        out_specs=out_specs,
        out_shape=out_shape,
        scratch_shapes=[pltpu.VMEM((seq_len, gw), F32), pltpu.VMEM((seq_len, gw), F32),
                        pltpu.VMEM((D_STATE, gw), F32), pltpu.VMEM((D_STATE, gw), F32)],
        compiler_params=_cparams(("arbitrary", "arbitrary")),
        name="ssd_scan_ctx" if want_final else "ssd_scan_dec",
    )(*args)


def _ret_kernel(*refs, n_chunks, has_init, has_final):
    lg_ref, q_ref, k_ref, v_ref, sg_ref = refs[:5]
    k = 5
    s0_ref = None
    if has_init:
        s0_ref = refs[k]
        k += 1
    y_ref = refs[k]
    k += 1
    sfin_ref = None
    if has_final:
        sfin_ref = refs[k]
        k += 1
    y_scrs = (refs[k], refs[k + 1])
    s_scrs = (refs[k + 2], refs[k + 3])

    qn = RET_CHUNK
    h = pl.program_id(1)
    row = lax.broadcasted_iota(jnp.int32, (qn, qn), 0)
    col = lax.broadcasted_iota(jnp.int32, (qn, qn), 1)
    rowk = lax.broadcasted_iota(jnp.int32, (qn, DK_RET), 0).astype(F32)

    def tables(direction):
        lg = lg_ref[direction, h]
        if direction == 0:
            keep = col <= row
            dist = (row - col).astype(F32)
            e_q = jnp.exp(lg * (rowk + 1.0))
            w_k = jnp.exp(lg * (qn - 1.0 - rowk))
        else:
            keep = col >= row
            dist = (col - row).astype(F32)
            e_q = jnp.exp(lg * (qn - rowk))
            w_k = jnp.exp(lg * rowk)
        dmat = jnp.where(keep, jnp.exp(lg * dist), 0.0)
        return dmat, e_q, w_k, jnp.exp(jnp.full((1, 1), float(qn), F32) * lg)

    def chunk(ci, direction, tabs):
        dmat, e_q, w_k, dec = tabs
        s_scr = s_scrs[direction]
        r0 = pl.multiple_of(ci * qn, qn)
        qc = q_ref[pl.ds(r0, qn), :]
        kc = k_ref[pl.ds(r0, qn), :]
        vc = v_ref[pl.ds(r0, qn), :]
        scores = lax.dot_general(qc, kc, (((1,), (1,)), ((), ())), preferred_element_type=F32) * dmat
        s_old = s_scr[...]
        lhs = jnp.concatenate([scores.astype(BF16), (qc.astype(F32) * e_q).astype(BF16)], axis=1)
        rhs = jnp.concatenate([vc, s_old.astype(BF16)], axis=0)
        y = jnp.dot(lhs, rhs, preferred_element_type=F32)
        kw_t = (kc.astype(F32) * w_k).T.astype(BF16)
        s_scr[...] = s_old * dec + jnp.dot(kw_t, vc, preferred_element_type=F32)
        return y

    for direction in range(2):
        if has_init:
            s_scrs[direction][...] = s0_ref[direction]
        else:
            s_scrs[direction][...] = jnp.zeros_like(s_scrs[direction])
    tabs = (tables(0), tables(1))

    def scan_body(j, carry):
        for direction, ci in ((0, j), (1, n_chunks - 1 - j)):
            r0 = pl.multiple_of(ci * qn, qn)
            y_scrs[direction][pl.ds(r0, qn), :] = chunk(ci, direction, tabs[direction])
        return carry

    lax.fori_loop(0, n_chunks, scan_body, 0)
    if has_final:
        for direction in range(2):
            sfin_ref[direction] = s_scrs[direction][...]

    def finish_body(ci, carry):
        r0 = pl.multiple_of(ci * qn, qn)
        y = y_scrs[0][pl.ds(r0, qn), :] + y_scrs[1][pl.ds(r0, qn), :]
        y = y * lax.rsqrt(jnp.mean(y * y, axis=-1, keepdims=True) + EPS)
        y_ref[pl.ds(r0, qn), :] = (y * sg_ref[pl.ds(r0, qn), :].astype(F32)).astype(y_ref.dtype)
        return carry

    lax.fori_loop(0, n_chunks, finish_body, 0)


def _ret_scan(log_g, q, k, v, sg, s0, *, row0, n_seq, seq_len, want_final):
    assert row0 % seq_len == 0
    b0 = row0 // seq_len
    has_init = s0 is not None
    in_specs = [
        pl.BlockSpec(memory_space=pltpu.SMEM),
        pl.BlockSpec((seq_len, DK_RET), lambda b, h: (b0 + b, h)),
        pl.BlockSpec((seq_len, DK_RET), lambda b, h: (b0 + b, h)),
        pl.BlockSpec((seq_len, DV_RET), lambda b, h: (b0 + b, h)),
        pl.BlockSpec((seq_len, DV_RET), lambda b, h: (b0 + b, h)),
    ]
    args = [log_g, q, k, v, sg]
    if has_init:
        in_specs.append(pl.BlockSpec((None, 2, None, DK_RET, DV_RET), lambda b, h: (b, 0, h, 0, 0)))
        args.append(s0)
    out_shape = [jax.ShapeDtypeStruct((n_seq * seq_len, H_RET * DV_RET), BF16)]
    out_specs = [pl.BlockSpec((seq_len, DV_RET), lambda b, h: (b, h))]
    if want_final:
        out_shape.append(jax.ShapeDtypeStruct((n_seq, 2, H_RET, DK_RET, DV_RET), F32))
        out_specs.append(pl.BlockSpec((None, 2, None, DK_RET, DV_RET), lambda b, h: (b, 0, h, 0, 0)))
    assert seq_len % RET_CHUNK == 0
    kern = functools.partial(_ret_kernel, n_chunks=seq_len // RET_CHUNK, has_init=has_init, has_final=want_final)
    return pl.pallas_call(
        kern,
        grid=(n_seq, H_RET),
        in_specs=in_specs,
        out_specs=out_specs,
        out_shape=out_shape,
        scratch_shapes=[pltpu.VMEM((seq_len, DV_RET), F32), pltpu.VMEM((seq_len, DV_RET), F32),
                        pltpu.VMEM((DK_RET, DV_RET), F32), pltpu.VMEM((DK_RET, DV_RET), F32)],
        compiler_params=_cparams(("arbitrary", "arbitrary")),
        name="ret_scan_ctx" if want_final else "ret_scan_dec",
    )(*args)


def _mixer_rows(tok, bm):
    n_ctx_tiles = tok.t_ctx // bm

    def specs(width):
        return [pl.BlockSpec((bm, width), lambda i: (jnp.minimum(i, n_ctx_tiles - 1), 0)),
                pl.BlockSpec((bm, width), lambda i: (jnp.maximum(i - n_ctx_tiles, 0), 0))]

    def select(yc_ref, yd_ref):
        return jnp.where(pl.program_id(0) < n_ctx_tiles, yc_ref[...], yd_ref[...])

    return specs, select


def _out_a_kernel(x_ref, gate_ref, yc_ref, yd_ref, u_ref, v_ref, wsp_ref, bsp_ref, w1_ref, w2_ref, o_ref, sgu_scr,
                  *, select):
    bm = x_ref.shape[0]
    for ci in range(bm // CHUNK):
        rows = slice(ci * CHUNK, (ci + 1) * CHUNK)
        for g in range(SG_GROUPS):
            cols = slice(g * LANES, (g + 1) * LANES)
            mix = jnp.dot(wsp_ref[g], v_ref[rows, cols], preferred_element_type=F32) + bsp_ref[:, cols]
            sgu_scr[rows, cols] = (u_ref[rows, cols].astype(F32) * mix).astype(BF16)
    out = jnp.dot(select(yc_ref, yd_ref), w1_ref[...], preferred_element_type=F32)
    out = out + jnp.dot(sgu_scr[...], w2_ref[...], preferred_element_type=F32)
    o_ref[...] = x_ref[...] + gate_ref[...] * out


def _out_proj_a(x, gate, y_ctx, y_dec, u, v, w_sp, b_full, w1, w2, tok, *, bm):
    t, d = x.shape
    group = tok.group_of_tile(bm)
    y_specs, select = _mixer_rows(tok, bm)
    return pl.pallas_call(
        functools.partial(_out_a_kernel, select=select),
        grid=(t // bm,),
        in_specs=[
            pl.BlockSpec((bm, d), lambda i: (i, 0)),
            pl.BlockSpec((None, 1, d), lambda i: (group(i), 0, 0)),
            *y_specs(d),
            pl.BlockSpec((bm, d), lambda i: (i, 0)),
            pl.BlockSpec((bm, d), lambda i: (i, 0)),
            pl.BlockSpec((SG_GROUPS, CHUNK, CHUNK), lambda i: (0, 0, 0)),
            pl.BlockSpec((CHUNK, d), lambda i: (0, 0)),
            pl.BlockSpec((d, d), lambda i: (0, 0)),
            pl.BlockSpec((d, d), lambda i: (0, 0)),
        ],
        out_specs=pl.BlockSpec((bm, d), lambda i: (i, 0)),
        out_shape=jax.ShapeDtypeStruct((t, d), F32),
        scratch_shapes=[pltpu.VMEM((bm, d), BF16)],
        compiler_params=_cparams(("arbitrary",)),
        name="out_proj_a",
    )(x, gate, y_ctx, y_dec, u, v, w_sp, b_full, w1, w2)


def _out_c_kernel(x_ref, gate_ref, yc_ref, yd_ref, w_ref, o_ref, *, select):
    out = jnp.dot(select(yc_ref, yd_ref), w_ref[...], preferred_element_type=F32)
    o_ref[...] = x_ref[...] + gate_ref[...] * out


def _out_proj_c(x, gate, y_ctx, y_dec, w, tok, *, bm):
    t, d = x.shape
    kdim = y_ctx.shape[1]
    group = tok.group_of_tile(bm)
    y_specs, select = _mixer_rows(tok, bm)
    return pl.pallas_call(
        functools.partial(_out_c_kernel, select=select),
        grid=(t // bm,),
        in_specs=[
            pl.BlockSpec((bm, d), lambda i: (i, 0)),
            pl.BlockSpec((None, 1, d), lambda i: (group(i), 0, 0)),
            *y_specs(kdim),
            pl.BlockSpec((kdim, d), lambda i: (0, 0)),
        ],
        out_specs=pl.BlockSpec((bm, d), lambda i: (i, 0)),
        out_shape=jax.ShapeDtypeStruct((t, d), F32),
        compiler_params=_cparams(("arbitrary",)),
        name="out_proj_c",
    )(x, gate, y_ctx, y_dec, w)


def _split_kernel(w_ref, g_ref, l_ref):
    w = w_ref[...]
    k, n2 = w.shape
    half = LANES // 2
    lane = lax.broadcasted_iota(jnp.int32, (k, LANES), 1)
    first = lane < half
    idx = jnp.where(first, 2 * lane, 2 * (lane - half) + 1)
    gs, ls = [], []
    for j in range(n2 // (2 * LANES)):
        a = jnp.take_along_axis(w[:, (2 * j) * LANES:(2 * j + 1) * LANES], idx, axis=1)
        b = jnp.take_along_axis(w[:, (2 * j + 1) * LANES:(2 * j + 2) * LANES], idx, axis=1)
        gs.append(jnp.where(first, a, pltpu.roll(b, half, axis=1)))
        ls.append(jnp.where(first, pltpu.roll(a, half, axis=1), b))
    g_ref[...] = jnp.concatenate(gs, axis=1).astype(BF16)
    l_ref[...] = jnp.concatenate(ls, axis=1).astype(BF16)


def _split_gate_lin(w_gu):
    dl, e, k, n2 = w_gu.shape
    tn = 512
    spec_out = pl.BlockSpec((None, None, k, tn), lambda a, b, j: (a, b, 0, j))
    return pl.pallas_call(
        _split_kernel,
        grid=(dl, e, n2 // (2 * tn)),
        in_specs=[pl.BlockSpec((None, None, k, 2 * tn), lambda a, b, j: (a, b, 0, j))],
        out_specs=[spec_out, spec_out],
        out_shape=[jax.ShapeDtypeStruct((dl, e, k, n2 // 2), BF16)] * 2,
        compiler_params=_cparams(("arbitrary",) * 3),
        name="split_gate_lin",
    )(w_gu)


def _router_kernel(x_ref, gam_ref, sc_ref, sh_ref, wr_ref, br_ref, h_ref, e_ref, r_ref, g_ref, cnt_ref, cnt_scr):
    i = pl.program_id(0)
    bm = x_ref.shape[0]

    @pl.when(i == 0)
    def _():
        cnt_scr[...] = jnp.zeros_like(cnt_scr)

    h = _modulated_norm(x_ref[...], gam_ref[...], sc_ref[...], sh_ref[...])
    h_ref[...] = h
    w = wr_ref[...]
    h_hi, w_hi = h.astype(BF16), w.astype(BF16)
    h_lo = (h - h_hi.astype(F32)).astype(BF16)
    w_lo = (w - w_hi.astype(F32)).astype(BF16)
    logits = (jnp.dot(h_hi, w_hi, preferred_element_type=F32) + jnp.dot(h_hi, w_lo, preferred_element_type=F32)
              + jnp.dot(h_lo, w_hi, preferred_element_type=F32)) + br_ref[...]
    lane = lax.broadcasted_iota(jnp.int32, logits.shape, 1).astype(F32)
    vals, idxs = [], []
    work = logits
    for _ in range(TOP_K):
        m = jnp.max(work, axis=-1, keepdims=True)
        idx = jnp.min(jnp.where(work == m, lane, float(LANES)), axis=-1, keepdims=True)
        vals.append(m)
        idxs.append(idx)
        work = jnp.where(lane == idx, -jnp.inf, work)
    exps = [jnp.exp(v - vals[0]) for v in vals]
    inv = 1.0 / functools.reduce(lambda a, b: a + b, exps)
    hot = functools.reduce(jnp.logical_or, [lane == idx for idx in idxs])
    hot_f = hot.astype(F32)
    ri = lax.broadcasted_iota(jnp.int32, (bm, bm), 0)
    ci = lax.broadcasted_iota(jnp.int32, (bm, bm), 1)
    before = (ci < ri).astype(BF16)
    rank_all = cnt_scr[...] + jnp.dot(before, hot_f.astype(BF16), preferred_element_type=F32)
    e_out = jnp.zeros(logits.shape, F32)
    r_out = jnp.zeros(logits.shape, F32)
    g_out = jnp.zeros(logits.shape, F32)
    for k in range(TOP_K):
        rk = jnp.sum(jnp.where(lane == idxs[k], rank_all, 0.0), axis=-1, keepdims=True)
        e_out = jnp.where(lane == float(k), idxs[k], e_out)
        r_out = jnp.where(lane == float(k), rk, r_out)
        g_out = jnp.where(lane == float(k), exps[k] * inv, g_out)
    e_ref[...] = e_out.astype(jnp.int32)
    r_ref[...] = r_out.astype(jnp.int32)
    g_ref[...] = g_out
    cnt_scr[...] = cnt_scr[...] + jnp.sum(hot_f, axis=0, keepdims=True)
    cnt_ref[...] = cnt_scr[...]


def _router(x, gam, sc, sh, w_r, b_r, tok, *, bm):
    t, d = x.shape
    group = tok.group_of_tile(bm)
    row_spec = pl.BlockSpec((bm, LANES), lambda i: (i, 0))
    return pl.pallas_call(
        _router_kernel,
        grid=(t // bm,),
        in_specs=[
            pl.BlockSpec((bm, d), lambda i: (i, 0)),
            pl.BlockSpec((1, d), lambda i: (0, 0)),
            pl.BlockSpec((None, 1, d), lambda i: (group(i), 0, 0)),
            pl.BlockSpec((None, 1, d), lambda i: (group(i), 0, 0)),
            pl.BlockSpec((d, LANES), lambda i: (0, 0)),
            pl.BlockSpec((1, LANES), lambda i: (0, 0)),
        ],
        out_specs=[pl.BlockSpec((bm, d), lambda i: (i, 0)), row_spec, row_spec, row_spec,
                   pl.BlockSpec((1, LANES), lambda i: (0, 0))],
        out_shape=[jax.ShapeDtypeStruct((t, d), F32), jax.ShapeDtypeStruct((t, LANES), jnp.int32),
                   jax.ShapeDtypeStruct((t, LANES), jnp.int32), jax.ShapeDtypeStruct((t, LANES), F32),
                   jax.ShapeDtypeStruct((1, LANES), F32)],
        scratch_shapes=[pltpu.VMEM((1, LANES), F32)],
        compiler_params=_cparams(("arbitrary",)),
        name="moe_router",
    )(x, gam, sc, sh, w_r, b_r)


SLABS = D_MODEL // LANES


def _to_row_tiles(ref, x):
    for s in range(SLABS):
        ref[:, s, :] = x[:, s * LANES:(s + 1) * LANES]


def _from_row_tiles(ref):
    return jnp.concatenate([ref[:, s, :] for s in range(SLABS)], axis=1)


def _dispatch_kernel(pe_ref, dest_ref, h_ref, xs_hbm, stage, zbuf, sem, zsem):
    bm = h_ref.shape[0]

    @pl.when(pl.program_id(0) == 0)
    def _():
        zbuf[...] = jnp.zeros_like(zbuf)

        def tail_copy(e):
            start = pl.multiple_of(pe_ref[e + 1] - MOE_ROWS, MOE_ROWS)
            return pltpu.make_async_copy(zbuf, xs_hbm.at[pl.ds(start, MOE_ROWS)], zsem)

        for e in range(N_EXPERTS):
            @pl.when(pe_ref[e + 1] > pe_ref[e])
            def _():
                tail_copy(e).start()
        for e in range(N_EXPERTS):
            @pl.when(pe_ref[e + 1] > pe_ref[e])
            def _():
                tail_copy(e).wait()

        def spare_copy(b):
            return pltpu.make_async_copy(zbuf, xs_hbm.at[pl.ds(pl.multiple_of(b * MOE_ROWS, MOE_ROWS), MOE_ROWS)], zsem)

        first_spare = pe_ref[N_EXPERTS] // MOE_ROWS
        n_blk = xs_hbm.shape[0] // MOE_ROWS
        lax.fori_loop(first_spare, n_blk, lambda b, c: (spare_copy(b).start(), c)[1], 0)
        lax.fori_loop(first_spare, n_blk, lambda b, c: (spare_copy(b).wait(), c)[1], 0)

    i = pl.program_id(0)
    slot = i % 2
    _to_row_tiles(stage.at[slot], h_ref[...])

    def issue(r, carry):
        for k in range(TOP_K):
            pltpu.make_async_copy(stage.at[slot, r], xs_hbm.at[dest_ref[r * TOP_K + k]],
                                  sem.at[slot]).start(priority=k % 2)
        return carry

    lax.fori_loop(0, bm, issue, 0, unroll=2)

    def drain(s):
        for k in range(TOP_K):
            pltpu.make_async_copy(stage.at[s], xs_hbm.at[pl.ds(0, bm)], sem.at[s]).wait()

    @pl.when(i >= 1)
    def _():
        drain(1 - slot)

    @pl.when(i == pl.num_programs(0) - 1)
    def _():
        drain(slot)


def _dispatch(pad_bounds, dest, h, n_rows, *, bm):
    t, d = h.shape
    n_tiles = t // bm
    grid_spec = pltpu.PrefetchScalarGridSpec(
        num_scalar_prefetch=1,
        grid=(n_tiles,),
        in_specs=[
            pl.BlockSpec((None, None, bm * TOP_K), lambda i, pe: (i, 0, 0), memory_space=pltpu.SMEM),
            pl.BlockSpec((bm, d), lambda i, pe: (i, 0)),
        ],
        out_specs=pl.BlockSpec(memory_space=pl.ANY),
        scratch_shapes=[pltpu.VMEM((2, bm, SLABS, LANES), F32), pltpu.VMEM((MOE_ROWS, SLABS, LANES), F32),
                        pltpu.SemaphoreType.DMA((2,)), pltpu.SemaphoreType.DMA],
    )
    return pl.pallas_call(
        _dispatch_kernel,
        grid_spec=grid_spec,
        out_shape=jax.ShapeDtypeStruct((n_rows, SLABS, LANES), F32),
        compiler_params=_cparams(("arbitrary",)),
        name="moe_dispatch",
    )(pad_bounds, dest.reshape(n_tiles, 1, bm * TOP_K), h)


def _expert_kernel(be_ref, na_ref, xs_hbm, wg_ref, wl_ref, bg_ref, bl_ref, wd_ref, bd_ref, ys_hbm,
                   xbuf, ybuf, zbuf, wd_s, sem_in, sem_out, zsem):
    i = pl.program_id(0)
    n_act = na_ref[0]
    slot = i % 2

    def rows(b):
        return pl.ds(pl.multiple_of(b * MOE_ROWS, MOE_ROWS), MOE_ROWS)

    def in_copy(b, s, j):
        return pltpu.make_async_copy(xs_hbm.at[rows(b), j, :], xbuf.at[s, :, pl.ds(j * LANES, LANES)], sem_in.at[s])

    def out_copy(b, s, j):
        return pltpu.make_async_copy(ybuf.at[s, :, pl.ds(j * LANES, LANES)], ys_hbm.at[rows(b), j, :], sem_out.at[s])

    @pl.when(i < n_act)
    def _():
        @pl.when(i == 0)
        def _():
            for j in range(SLABS):
                in_copy(0, 0, j).start()

        @pl.when(i + 1 < n_act)
        def _():
            for j in range(SLABS):
                in_copy(i + 1, 1 - slot, j).start()

        for j in range(SLABS):
            in_copy(i, slot, j).wait()

        @pl.when(i >= 2)
        def _():
            for j in range(SLABS):
                out_copy(i - 2, slot, j).wait()

        @pl.when(jnp.logical_or(i == 0, be_ref[i] != be_ref[jnp.maximum(i - 1, 0)]))
        def _():
            wd_s[...] = wd_ref[...].astype(BF16)

        x = xbuf[slot].astype(BF16)
        hg = jnp.dot(x, wg_ref[...], preferred_element_type=F32) + bg_ref[...]
        hl = jnp.dot(x, wl_ref[...], preferred_element_type=F32) + bl_ref[...]
        glu = jnp.minimum(hg, SWIGLU_LIMIT)
        lin = jnp.clip(hl, -SWIGLU_LIMIT, SWIGLU_LIMIT)
        act = glu * (1.0 / (1.0 + jnp.exp(-SWIGLU_ALPHA * glu))) * (lin + 1.0)
        ybuf[slot] = jnp.dot(act.astype(BF16), wd_s[...], preferred_element_type=F32) + bd_ref[...]
        for j in range(SLABS):
            out_copy(i, slot, j).start()

        @pl.when(i == n_act - 1)
        def _():
            for j in range(SLABS):
                out_copy(i, slot, j).wait()

            @pl.when(i >= 1)
            def _():
                for j in range(SLABS):
                    out_copy(i - 1, 1 - slot, j).wait()

    @pl.when(i >= n_act)
    def _():
        @pl.when(i == n_act)
        def _():
            zbuf[...] = jnp.zeros_like(zbuf)

        fill = pltpu.make_async_copy(zbuf, ys_hbm.at[rows(i)], zsem)
        fill.start()
        fill.wait()


def _experts(blk_expert, n_active, xs, wg, wl, bg, bl, wd, bd, layer):
    n_rows = xs.shape[0]
    d, dff = wg.shape[2], wg.shape[3]
    n_blk = n_rows // MOE_ROWS
    wmap = lambda i, be, na: (layer, be[i], 0, 0)
    grid_spec = pltpu.PrefetchScalarGridSpec(
        num_scalar_prefetch=2,
        grid=(n_blk,),
        in_specs=[
            pl.BlockSpec(memory_space=pl.ANY),
            pl.BlockSpec((None, None, d, dff), wmap),
            pl.BlockSpec((None, None, d, dff), wmap),
            pl.BlockSpec((None, None, 1, dff), wmap),
            pl.BlockSpec((None, None, 1, dff), wmap),
            pl.BlockSpec((None, None, dff, d), wmap),
            pl.BlockSpec((None, None, 1, d), wmap),
        ],
        out_specs=pl.BlockSpec(memory_space=pl.ANY),
        scratch_shapes=[pltpu.VMEM((2, MOE_ROWS, d), F32), pltpu.VMEM((2, MOE_ROWS, d), F32),
                        pltpu.VMEM((MOE_ROWS, SLABS, LANES), F32), pltpu.VMEM((dff, d), BF16),
                        pltpu.SemaphoreType.DMA((2,)), pltpu.SemaphoreType.DMA((2,)), pltpu.SemaphoreType.DMA],
    )
    return pl.pallas_call(
        _expert_kernel,
        grid_spec=grid_spec,
        out_shape=jax.ShapeDtypeStruct((n_rows, SLABS, LANES), F32),
        compiler_params=_cparams(("arbitrary",)),
        name="moe_experts",
    )(blk_expert, n_active, xs, wg, wl, bg, bl, wd, bd)


def _combine_kernel(dest_ref, next_ref, g_ref, x_ref, gate_ref, gam_ref, ys_hbm, o_ref, *rest, final_norm, n_ctx_tiles):
    if final_norm:
        of_ref, buf, acc_scr, sem = rest
    else:
        buf, acc_scr, sem = rest
    bm = x_ref.shape[0]
    i = pl.program_id(0)
    slot = i % 2

    def gather(idx_ref, s):
        def issue(r, carry):
            for k in range(TOP_K):
                pltpu.make_async_copy(ys_hbm.at[idx_ref[r * TOP_K + k]], buf.at[s, k, r],
                                      sem.at[s]).start(priority=k % 2)
            return carry

        lax.fori_loop(0, bm, issue, 0, unroll=2)

    @pl.when(i == 0)
    def _():
        gather(dest_ref, 0)

    @pl.when(i + 1 < pl.num_programs(0))
    def _():
        gather(next_ref, 1 - slot)

    for k in range(TOP_K):
        pltpu.make_async_copy(ys_hbm.at[pl.ds(0, bm)], buf.at[slot, k], sem.at[slot]).wait()

    def mix(r, carry):
        acc = g_ref[r * TOP_K] * buf[slot, 0, r]
        for k in range(1, TOP_K):
            acc = acc + g_ref[r * TOP_K + k] * buf[slot, k, r]
        acc_scr[r] = acc
        return carry

    lax.fori_loop(0, bm, mix, 0, unroll=8)
    xn = x_ref[...] + gate_ref[...] * _from_row_tiles(acc_scr)
    if not final_norm:
        o_ref[...] = xn
    else:
        ms = jnp.mean(xn * xn, axis=-1, keepdims=True)
        yn = xn * lax.rsqrt(ms + EPS) * gam_ref[...]
        is_ctx = pl.program_id(0) < n_ctx_tiles

        @pl.when(is_ctx)
        def _():
            o_ref[...] = yn

        @pl.when(jnp.logical_not(is_ctx))
        def _():
            of_ref[...] = yn


def _combine(dest, x, gate, gates, gam_final, ys, tok, *, bm, final_norm):
    t, d = x.shape
    group = tok.group_of_tile(bm)
    n_tiles = t // bm
    n_ctx_tiles = tok.t_ctx // bm
    dest_tiles = dest.reshape(n_tiles, 1, bm * TOP_K)
    if final_norm:
        out_shape = [jax.ShapeDtypeStruct((tok.t_ctx, d), F32), jax.ShapeDtypeStruct((tok.t_dec, d), F32)]
        out_specs = [pl.BlockSpec((bm, d), lambda i: (jnp.minimum(i, n_ctx_tiles - 1), 0)),
                     pl.BlockSpec((bm, d), lambda i: (jnp.maximum(i - n_ctx_tiles, 0), 0))]
    else:
        out_shape = [jax.ShapeDtypeStruct((t, d), F32)]
        out_specs = [pl.BlockSpec((bm, d), lambda i: (i, 0))]
    return pl.pallas_call(
        functools.partial(_combine_kernel, final_norm=final_norm, n_ctx_tiles=n_ctx_tiles),
        grid=(n_tiles,),
        in_specs=[
            pl.BlockSpec((None, None, bm * TOP_K), lambda i: (i, 0, 0), memory_space=pltpu.SMEM),
            pl.BlockSpec((None, None, bm * TOP_K), lambda i: (jnp.minimum(i + 1, n_tiles - 1), 0, 0),
                         memory_space=pltpu.SMEM),
            pl.BlockSpec((None, None, bm * TOP_K), lambda i: (i, 0, 0), memory_space=pltpu.SMEM),
            pl.BlockSpec((bm, d), lambda i: (i, 0)),
            pl.BlockSpec((None, 1, d), lambda i: (group(i), 0, 0)),
            pl.BlockSpec((1, d), lambda i: (0, 0)),
            pl.BlockSpec(memory_space=pl.ANY),
        ],
        out_specs=out_specs,
        out_shape=out_shape,
        scratch_shapes=[pltpu.VMEM((2, TOP_K, bm, SLABS, LANES), F32), pltpu.VMEM((bm, SLABS, LANES), F32),
                        pltpu.SemaphoreType.DMA((2,))],
        compiler_params=_cparams(("arbitrary",)),
        name="moe_combine_final" if final_norm else "moe_combine",
    )(dest_tiles, dest_tiles, gates[:, :TOP_K].reshape(n_tiles, 1, bm * TOP_K), x, gate, gam_final, ys)


def _moe(x, gam, sc, sh, gate, w_r, b_r, wg, wl, bg, bl, wd, bd, layer, gam_final, tok, *, final_norm):
    t, d = x.shape
    h, e_out, r_out, gates, counts = _router(x, gam, sc, sh, w_r, b_r, tok, bm=512)
    counts = counts[0, :N_EXPERTS].astype(jnp.int32)
    padded = (counts + MOE_ROWS - 1) // MOE_ROWS * MOE_ROWS
    pad_end = jnp.cumsum(padded)
    pad_start = pad_end - padded
    e_sel = e_out[:, :TOP_K]
    dest = pad_start[e_sel] + r_out[:, :TOP_K]
    n_rows = t * TOP_K + N_EXPERTS * MOE_ROWS
    n_blk = n_rows // MOE_ROWS
    blk_start = jnp.arange(n_blk, dtype=jnp.int32) * MOE_ROWS
    blk_expert = jnp.minimum(jnp.sum((pad_end[None, :] <= blk_start[:, None]).astype(jnp.int32), axis=1),
                             N_EXPERTS - 1)
    n_active = (pad_end[-1:] // MOE_ROWS).astype(jnp.int32)
    pad_bounds = jnp.concatenate([jnp.zeros((1,), jnp.int32), pad_end.astype(jnp.int32)])
    xs = _dispatch(pad_bounds, dest, h, n_rows, bm=256)
    ys = _experts(blk_expert, n_active, xs, wg, wl, bg, bl, wd, bd, layer)
    return _combine(dest, x, gate, gates, gam_final, ys, tok, bm=256, final_norm=final_norm)


def _rope_tables(tok):
    rows = tok.dec_len // GRID_W
    r = jnp.repeat(jnp.arange(rows), GRID_W).astype(F32)
    cidx = jnp.tile(jnp.arange(GRID_W), rows).astype(F32)
    n_freq = DK_RET // 4
    inv = ROPE_BASE ** (-jnp.arange(n_freq, dtype=F32) / n_freq)
    ang = jnp.concatenate([r[:, None] * inv, cidx[:, None] * inv], axis=-1)
    cos = jnp.concatenate([jnp.ones((tok.t_ctx, DK_RET // 2), F32), jnp.tile(jnp.cos(ang), (tok.n_dec_seq, 1))])
    sin = jnp.concatenate([jnp.zeros((tok.t_ctx, DK_RET // 2), F32), jnp.tile(jnp.sin(ang), (tok.n_dec_seq, 1))])
    return cos, sin


def _group_cols(p):
    h = p.shape[1]
    a = p.reshape(2, SSD_GROUPS, h // SSD_GROUPS).transpose(1, 0, 2).reshape(SSD_GROUPS, -1)
    return jnp.pad(a, ((0, 0), (0, LANES - a.shape[1])))[:, None, :]


def kernel(x_prompt, x_sample, state_ssd, state_ret, c, c_ctx, w_mod, b_mod, norm_mix, norm_ffn, w_in_a, conv_w, conv_b, dt_bias, a_log, d_skip, ssd_norm, w_sp, b_sp, w_out_a, w_in_c, decay_logit, w_out_c, w_router, b_router, w_gu, b_gu, w_down, b_down, norm_final):
    n_ctx, ctx_len, d = x_prompt.shape
    n_dec, dec_len, _ = x_sample.shape
    tok = _Tokens(n_ctx, ctx_len, n_dec, dec_len)
    depth = w_mod.shape[0]
    x = jnp.concatenate([x_prompt.reshape(tok.t_ctx, d), x_sample.reshape(tok.t_dec, d)])

    cvecs = jnp.concatenate([c_ctx[None], c, jnp.zeros((MOD_ROWS - 1 - n_dec, d), F32)])
    mod = _modulation(cvecs, w_mod, b_mod)
    mod = mod.reshape(depth, MOD_ROWS, 6, 1, d).transpose(0, 2, 1, 3, 4)

    h_ssd = a_log.shape[2]
    xbc_w = d + 2 * SSD_GROUPS * D_STATE
    o1, o2, o3 = d, d + xbc_w, d + xbc_w + 2 * h_ssd
    cos, sin = _rope_tables(tok)
    wg_all, wl_all = _split_gate_lin(w_gu)
    bg_all = b_gu[:, :, None, 0::2]
    bl_all = b_gu[:, :, None, 1::2]
    wd_all = w_down
    bd_all = b_down[:, :, None, :]
    new_ssd, new_ret = [], []
    y_final = None
    bm_proj = 512

    for l in range(depth):
        sh1, sc1, g1, sh2, sc2, g2 = (mod[l, j] for j in range(6))
        gam_mix = norm_mix[l][None]
        i = l // 2
        if l % 2 == 0:
            w_in = w_in_a[i]
            w_z = w_in[:, :o1].astype(BF16)
            w_xbc = w_in[:, o1:o2].astype(BF16)
            w_dt = w_in[:, o2:o3].reshape(d, 2, SSD_GROUPS, h_ssd // SSD_GROUPS).transpose(0, 2, 1, 3)
            w_dt = jnp.pad(w_dt.reshape(d, SSD_GROUPS, -1), ((0, 0), (0, 0), (0, LANES - 2 * h_ssd // SSD_GROUPS)))
            w_dt = w_dt.reshape(d, SSD_GROUPS * LANES).astype(BF16)
            w_uv = w_in[:, o3:].astype(BF16)
            xbc = _norm_proj(x, gam_mix, sc1, sh1, w_xbc, tok=tok, bm=dec_len, tn=256, out_dtype=BF16,
                             epilogue=_ep_conv_silu, extra=(conv_w[i], conv_b[i][None]),
                             extra_specs=(pl.BlockSpec((4, 256), lambda r, j: (0, j)),
                                          pl.BlockSpec((1, 256), lambda r, j: (0, j))), name="proj_xbc")
            dtb = _group_cols(dt_bias[i]).reshape(1, SSD_GROUPS * LANES)
            sz, dt, u, v = _norm_proj_multi(
                x, gam_mix, sc1, sh1, jnp.concatenate([w_z, w_dt, w_uv], axis=1), tok,
                [(d, 512, BF16, _ep_silu, False),
                 (SSD_GROUPS * LANES, SSD_GROUPS * LANES, F32, _ep_softplus_bias, True),
                 (d, 512, BF16, _ep_gelu, False), (d, d, BF16, _ep_gelu_ln, False)],
                bm=bm_proj, extra=(dtb,), extra_specs=(pl.BlockSpec((1, SSD_GROUPS * LANES), lambda r: (0, 0)),),
                name="proj_zdtuv")
            a_neg = _group_cols(-jnp.exp(a_log[i]))
            dsk = jnp.repeat(d_skip[i], SSD_HEAD).reshape(SSD_GROUPS, 1, GROUP_W)
            nrm = ssd_norm[i].reshape(SSD_GROUPS, 1, GROUP_W)
            s0 = state_ssd[:, i].reshape(n_dec, 2, SSD_GROUPS, HEADS_PER_GROUP, D_STATE, SSD_HEAD)
            s0 = s0.transpose(0, 2, 1, 4, 3, 5).reshape(n_dec, SSD_GROUPS, 2, D_STATE, GROUP_W)
            y_ctx, s_fin = _ssd_scan(xbc, dt, sz, a_neg, dsk, nrm, None, row0=0, n_seq=n_ctx,
                                     seq_len=ctx_len, want_final=True)
            (y_dec,) = _ssd_scan(xbc, dt, sz, a_neg, dsk, nrm, s0, row0=tok.t_ctx, n_seq=n_dec,
                                 seq_len=dec_len, want_final=False)
            s_fin = s_fin.reshape(n_ctx, SSD_GROUPS, 2, D_STATE, HEADS_PER_GROUP, SSD_HEAD)
            new_ssd.append(s_fin.transpose(0, 2, 1, 4, 3, 5).reshape(n_ctx, 2, h_ssd, D_STATE, SSD_HEAD))
            b_full = jnp.repeat(b_sp[i].T, LANES, axis=1)
            w_o = w_out_a[i].astype(BF16)
            x = _out_proj_a(x, g1, y_ctx, y_dec, u, v, w_sp[i].astype(BF16), b_full, w_o[:d], w_o[d:], tok, bm=512)
        else:
            hk = H_RET * DK_RET
            hv = H_RET * DV_RET
            w_in = w_in_c[i].astype(BF16)
            rope_specs = (pl.BlockSpec((bm_proj, DK_RET // 2), lambda r: (r, 0)),) * 2
            q, kk, vv, sg = _norm_proj_multi(
                x, gam_mix, sc1, sh1, w_in, tok,
                [(hk, 512, BF16, functools.partial(_ep_rope, scale=1.0), True),
                 (hk, 512, BF16, functools.partial(_ep_rope, scale=DK_RET ** -0.5), True),
                 (hv, 512, BF16, _ep_plain, False), (hv, 512, BF16, _ep_silu, False)],
                bm=bm_proj, extra=(cos, sin), extra_specs=rope_specs, name="proj_qkvg")
            log_g = jax.nn.log_sigmoid(decay_logit[i].astype(F32))
            y_ctx, s_fin = _ret_scan(log_g, q, kk, vv, sg, None, row0=0, n_seq=n_ctx, seq_len=ctx_len, want_final=True)
            (y_dec,) = _ret_scan(log_g, q, kk, vv, sg, state_ret[:, i], row0=tok.t_ctx, n_seq=n_dec, seq_len=dec_len,
                                 want_final=False)
            new_ret.append(s_fin)
            x = _out_proj_c(x, g1, y_ctx, y_dec, w_out_c[i].astype(BF16), tok, bm=512)

        w_r = jnp.pad(w_router[l], ((0, 0), (0, LANES - N_EXPERTS)))
        b_r = jnp.pad(b_router[l], (0, LANES - N_EXPERTS), constant_values=-1e30)[None]
        last = l == depth - 1
        res = _moe(x, norm_ffn[l][None], sc2, sh2, g2, w_r, b_r, wg_all, wl_all, bg_all, bl_all, wd_all, bd_all, l,
                   norm_final[None], tok, final_norm=last)
        if last:
            y_final = res
        else:
            x = res[0]

    y_prompt = y_final[0].reshape(n_ctx, ctx_len, d)
    y_sample = y_final[1].reshape(n_dec, dec_len, d)
    return (y_prompt, y_sample, jnp.stack(new_ssd, axis=1), jnp.stack(new_ret, axis=1))
```

```python
import functools
import math

import jax
import jax.numpy as jnp
from jax import lax
from jax.experimental import pallas as pl
from jax.experimental.pallas import tpu as pltpu

D_MODEL = 1024
GRID_W = 64
CHUNK = 128
RET_CHUNK = 256
SSD_HEAD = 64
SSD_GROUPS = 2
D_STATE = 128
GROUP_W = D_MODEL // SSD_GROUPS
HEADS_PER_GROUP = GROUP_W // SSD_HEAD
SG_GROUPS = 8
H_RET = 4
DK_RET = D_MODEL // H_RET
DV_RET = 2 * DK_RET
ROPE_BASE = 10000.0
N_EXPERTS = 32
TOP_K = 4
SWIGLU_LIMIT = 7.0
SWIGLU_ALPHA = 1.702
EPS = 1e-6

LANES = 128
MOD_ROWS = 8
MOE_ROWS = 512
VMEM_LIMIT = 56 * 1024 * 1024

F32 = jnp.float32
BF16 = jnp.bfloat16
HI = lax.Precision.HIGHEST


def _cparams(sem):
    return pltpu.CompilerParams(dimension_semantics=sem, vmem_limit_bytes=VMEM_LIMIT)


def _silu(x):
    return x * (1.0 / (1.0 + jnp.exp(-x)))


def _gelu_tanh(x):
    return 0.5 * x * (1.0 + jnp.tanh(math.sqrt(2.0 / math.pi) * (x + 0.044715 * (x * x * x))))


def _softplus(x):
    return jnp.maximum(x, 0.0) + jnp.log(1.0 + jnp.exp(-jnp.abs(x)))


def _mod_kernel(c_ref, w_ref, b_ref, o_ref):
    a = _silu(c_ref[...])
    o_ref[...] = jnp.dot(a, w_ref[...], precision=HI, preferred_element_type=F32) + b_ref[...]


def _modulation(cvecs, w_mod, b_mod):
    depth, d, n = w_mod.shape
    tn = 1536
    return pl.pallas_call(
        _mod_kernel,
        grid=(depth, n // tn),
        in_specs=[
            pl.BlockSpec((MOD_ROWS, d), lambda l, j: (0, 0)),
            pl.BlockSpec((None, d, tn), lambda l, j: (l, 0, j)),
            pl.BlockSpec((None, 1, tn), lambda l, j: (l, 0, j)),
        ],
        out_specs=pl.BlockSpec((None, MOD_ROWS, tn), lambda l, j: (l, 0, j)),
        out_shape=jax.ShapeDtypeStruct((depth, MOD_ROWS, n), F32),
        compiler_params=_cparams(("arbitrary", "arbitrary")),
        name="modulation",
    )(cvecs, w_mod, b_mod.reshape(depth, 1, n))


class _Tokens:
    def __init__(self, n_ctx_seq, ctx_len, n_dec_seq, dec_len):
        self.n_ctx_seq, self.ctx_len = n_ctx_seq, ctx_len
        self.n_dec_seq, self.dec_len = n_dec_seq, dec_len
        self.t_ctx = n_ctx_seq * ctx_len
        self.t_dec = n_dec_seq * dec_len
        self.total = self.t_ctx + self.t_dec

    def group_of_tile(self, bm):
        assert self.t_ctx % bm == 0 and self.dec_len % bm == 0
        n_ctx_tiles = self.t_ctx // bm
        per_seq = self.dec_len // bm

        def group(i):
            return jnp.where(i < n_ctx_tiles, 0, 1 + (i - n_ctx_tiles) // per_seq)

        return group


def _modulated_norm(x, gam, sc, sh):
    ms = jnp.mean(x * x, axis=-1, keepdims=True)
    return (x * lax.rsqrt(ms + EPS) * gam) * (1.0 + sc) + sh


def _x_specs(x, tok, bm):
    if not isinstance(x, tuple):
        return [pl.BlockSpec((bm, x.shape[1]), lambda i, *_: (i, 0))]
    n_ctx_tiles = tok.t_ctx // bm
    d = x[0].shape[1]
    return [pl.BlockSpec((bm, d), lambda i, *_: (jnp.minimum(i, n_ctx_tiles - 1), 0)),
            pl.BlockSpec((bm, d), lambda i, *_: (jnp.maximum(i - n_ctx_tiles, 0), 0))]


def _x_rows(x_refs, tok, bm):
    if len(x_refs) == 1:
        return x_refs[0][...]
    return jnp.where(pl.program_id(0) < tok.t_ctx // bm, x_refs[0][...], x_refs[1][...])


def _proj_kernel(*refs, epilogue, n_extra, tok, bm, n_x):
    x_refs = refs[:n_x]
    gam_ref, sc_ref, sh_ref, w_ref = refs[n_x:n_x + 4]
    extra = refs[n_x + 4:n_x + 4 + n_extra]
    o_ref = refs[n_x + 4 + n_extra]
    h_scr = refs[n_x + 5 + n_extra]
    i = pl.program_id(0)

    @pl.when(pl.program_id(1) == 0)
    def _():
        h_scr[...] = _modulated_norm(_x_rows(x_refs, tok, bm), gam_ref[...], sc_ref[...], sh_ref[...]).astype(BF16)

    acc = jnp.dot(h_scr[...], w_ref[...], preferred_element_type=F32)
    o_ref[...] = epilogue(acc, i, tok, bm, *extra).astype(o_ref.dtype)


def _ep_plain(acc, i, tok, bm):
    return acc


def _ep_silu(acc, i, tok, bm):
    return _silu(acc)


def _ep_gelu(acc, i, tok, bm):
    return _gelu_tanh(acc)


def _ep_softplus_bias(acc, i, tok, bm, bias_ref):
    return _softplus(acc + bias_ref[...])


def _ep_gelu_ln(acc, i, tok, bm):
    g = _gelu_tanh(acc)
    mu = jnp.mean(g, axis=-1, keepdims=True)
    gc = g - mu
    return gc * lax.rsqrt(jnp.mean(gc * gc, axis=-1, keepdims=True) + 1e-5)


def _ep_conv_silu(acc, i, tok, bm, cw_ref, cb_ref):
    n = acc.shape[0]
    seq = jnp.where(i * bm < tok.t_ctx, tok.ctx_len, tok.dec_len)
    t = lax.broadcasted_iota(jnp.int32, (n, 1), 0) & (seq - 1)
    cw = cw_ref[...]
    y = acc * cw[2:3, :] + cb_ref[...]
    y = y + jnp.where(t >= 2, pltpu.roll(acc, 2, axis=0), 0.0) * cw[0:1, :]
    y = y + jnp.where(t >= 1, pltpu.roll(acc, 1, axis=0), 0.0) * cw[1:2, :]
    y = y + jnp.where(t < seq - 1, pltpu.roll(acc, n - 1, axis=0), 0.0) * cw[3:4, :]
    return _silu(y)


def _ep_rope(acc, i, tok, bm, cos_ref, sin_ref, *, scale):
    cs, sn = cos_ref[...], sin_ref[...]
    half = DK_RET // 2
    outs = []
    for h in range(acc.shape[1] // DK_RET):
        x1 = acc[:, h * DK_RET:h * DK_RET + half] * scale
        x2 = acc[:, h * DK_RET + half:(h + 1) * DK_RET] * scale
        outs.append(x1 * cs - x2 * sn)
        outs.append(x2 * cs + x1 * sn)
    return jnp.concatenate(outs, axis=1)


def _norm_proj(x, gam, sc, sh, w, tok, *, bm, tn, out_dtype, epilogue, extra=(), extra_specs=(), name):
    xs = x if isinstance(x, tuple) else (x,)
    t, d = tok.total, w.shape[0]
    n = w.shape[1]
    group = tok.group_of_tile(bm)
    kern = functools.partial(_proj_kernel, epilogue=epilogue, n_extra=len(extra), tok=tok, bm=bm, n_x=len(xs))
    return pl.pallas_call(
        kern,
        grid=(t // bm, n // tn),
        in_specs=[
            *_x_specs(x, tok, bm),
            pl.BlockSpec((1, d), lambda i, j: (0, 0)),
            pl.BlockSpec((None, 1, d), lambda i, j: (group(i), 0, 0)),
            pl.BlockSpec((None, 1, d), lambda i, j: (group(i), 0, 0)),
            pl.BlockSpec((d, tn), lambda i, j: (0, j)),
            *extra_specs,
        ],
        out_specs=pl.BlockSpec((bm, tn), lambda i, j: (i, j)),
        out_shape=jax.ShapeDtypeStruct((t, n), out_dtype),
        scratch_shapes=[pltpu.VMEM((bm, d), BF16)],
        compiler_params=_cparams(("arbitrary", "arbitrary")),
        name=name,
    )(*xs, gam, sc, sh, w, *extra)


def _proj_multi_kernel(*refs, pieces, n_extra, tok, bm, n_x):
    x_refs = refs[:n_x]
    gam_ref, sc_ref, sh_ref, w_ref = refs[n_x:n_x + 4]
    extra = refs[n_x + 4:n_x + 4 + n_extra]
    outs = refs[n_x + 4 + n_extra:n_x + 4 + n_extra + len(pieces)]
    h_scr = refs[n_x + 4 + n_extra + len(pieces)]
    i = pl.program_id(0)
    h_scr[...] = _modulated_norm(_x_rows(x_refs, tok, bm), gam_ref[...], sc_ref[...], sh_ref[...]).astype(BF16)
    c0 = 0
    for (width, chunk, _, epilogue, uses_extra), o_ref in zip(pieces, outs):
        for j in range(width // chunk):
            acc = jnp.dot(h_scr[...], w_ref[:, c0 + j * chunk:c0 + (j + 1) * chunk], preferred_element_type=F32)
            res = epilogue(acc, i, tok, bm, *(extra if uses_extra else ()))
            o_ref[:, j * chunk:(j + 1) * chunk] = res.astype(o_ref.dtype)
        c0 += width


def _norm_proj_multi(x, gam, sc, sh, w, tok, pieces, *, bm, extra=(), extra_specs=(), name):
    xs = x if isinstance(x, tuple) else (x,)
    t, d = tok.total, w.shape[0]
    n = w.shape[1]
    assert n == sum(p[0] for p in pieces)
    group = tok.group_of_tile(bm)
    kern = functools.partial(_proj_multi_kernel, pieces=tuple(pieces), n_extra=len(extra), tok=tok, bm=bm,
                             n_x=len(xs))
    return pl.pallas_call(
        kern,
        grid=(t // bm,),
        in_specs=[
            *_x_specs(x, tok, bm),
            pl.BlockSpec((1, d), lambda i: (0, 0)),
            pl.BlockSpec((None, 1, d), lambda i: (group(i), 0, 0)),
            pl.BlockSpec((None, 1, d), lambda i: (group(i), 0, 0)),
            pl.BlockSpec((d, n), lambda i: (0, 0), pipeline_mode=pl.Buffered(1)),
            *extra_specs,
        ],
        out_specs=[pl.BlockSpec((bm, p[0]), lambda i: (i, 0)) for p in pieces],
        out_shape=[jax.ShapeDtypeStruct((t, p[0]), p[2]) for p in pieces],
        scratch_shapes=[pltpu.VMEM((bm, d), BF16)],
        compiler_params=_cparams(("arbitrary",)),
        name=name,
    )(*xs, gam, sc, sh, w, *extra)


def _ssd_kernel(*refs, n_chunks, has_init, has_final):
    xs_ref, b_ref, c_ref, dt_ref, sz_ref, a_ref, dsk_ref, nrm_ref = refs[:8]
    k = 8
    s0_ref = None
    if has_init:
        s0_ref = refs[k]
        k += 1
    y_ref = refs[k]
    k += 1
    sfin_ref = None
    if has_final:
        sfin_ref = refs[k]
        k += 1
    y_scrs = (refs[k], refs[k + 1])
    s_scrs = (refs[k + 2], refs[k + 3])

    q = CHUNK
    row = lax.broadcasted_iota(jnp.int32, (q, q), 0)
    col = lax.broadcasted_iota(jnp.int32, (q, q), 1)
    lane = lax.broadcasted_iota(jnp.int32, (1, q), 1)
    left = lane < SSD_HEAD
    a_neg = a_ref[...]
    n_pairs = HEADS_PER_GROUP // 2
    keeps = (col <= row, col >= row)
    tris = tuple(kp.astype(F32).astype(BF16) for kp in keeps)

    def masked_sums(tri, x):
        hi = x.astype(BF16)
        r1 = x - hi.astype(F32)
        mid = r1.astype(BF16)
        lo = (r1 - mid.astype(F32)).astype(BF16)
        return (jnp.dot(tri, hi, preferred_element_type=F32) + jnp.dot(tri, mid, preferred_element_type=F32)
                + jnp.dot(tri, lo, preferred_element_type=F32))

    def chunk(ci, direction):
        s_scr = s_scrs[direction]
        r0 = pl.multiple_of(ci * q, q)
        xs = xs_ref[pl.ds(r0, q), :]
        bm_ = b_ref[pl.ds(r0, q), :]
        cm = c_ref[pl.ds(r0, q), :]
        dt = dt_ref[pl.ds(r0, q), :]
        la = dt * a_neg
        keep, tri = keeps[direction], tris[direction]
        last = q - 1 if direction == 0 else 0
        cum = masked_sums(tri, la)
        cum_t = cum.T
        dt_t = dt.T
        tot_t = jnp.broadcast_to(cum_t[:, last:last + 1], (q, q))
        w_t = dt_t * jnp.exp(tot_t - cum_t)
        g = lax.dot_general(cm, bm_, (((1,), (1,)), ((), ())), preferred_element_type=F32)
        b_t = bm_.astype(F32).T
        cm_f = cm.astype(F32)
        outs = []
        for p in range(n_pairs):
            xs_p = xs[:, p * LANES:(p + 1) * LANES]
            s_p = s_scr[:, p * LANES:(p + 1) * LANES]
            s_b = s_p.astype(BF16)
            zero = jnp.zeros_like(xs_p)
            zero_s = jnp.zeros_like(s_b)
            lhs, rhs, lhs_s, rhs_s, decs = [], [], [], [], []
            for hh in range(2):
                cidx = direction * HEADS_PER_GROUP + 2 * p + hh
                cum_b = jnp.broadcast_to(cum[:, cidx:cidx + 1], (q, q))
                dec = jnp.exp(jnp.where(keep, cum_b - cum_t[cidx:cidx + 1, :], -jnp.inf))
                scores = g * dec * dt_t[cidx:cidx + 1, :]
                lhs += [scores.astype(BF16), (cm_f * jnp.exp(cum_b)).astype(BF16)]
                sel = left if hh == 0 else jnp.logical_not(left)
                rhs += [jnp.where(sel, xs_p, zero), jnp.where(sel, s_b, zero_s)]
                lhs_s.append((b_t * w_t[cidx:cidx + 1, :]).astype(BF16))
                rhs_s.append(jnp.where(sel, xs_p, zero))
                decs.append(jnp.exp(cum_t[cidx:cidx + 1, last:last + 1]))
            y_p = jnp.dot(jnp.concatenate(lhs, axis=1), jnp.concatenate(rhs, axis=0),
                          preferred_element_type=F32)
            upd = jnp.dot(jnp.concatenate(lhs_s, axis=1), jnp.concatenate(rhs_s, axis=0),
                          preferred_element_type=F32)
            s_scr[:, p * LANES:(p + 1) * LANES] = s_p * jnp.where(left, decs[0], decs[1]) + upd
            outs.append(y_p)
        return jnp.concatenate(outs, axis=1)

    for direction in range(2):
        if has_init:
            s_scrs[direction][...] = s0_ref[direction]
        else:
            s_scrs[direction][...] = jnp.zeros_like(s_scrs[direction])

    def scan_body(j, carry):
        for direction, ci in ((0, j), (1, n_chunks - 1 - j)):
            r0 = pl.multiple_of(ci * q, q)
            y_scrs[direction][pl.ds(r0, q), :] = chunk(ci, direction)
        return carry

    lax.fori_loop(0, n_chunks, scan_body, 0)
    if has_final:
        for direction in range(2):
            sfin_ref[direction] = s_scrs[direction][...]

    def finish_body(ci, carry):
        r0 = pl.multiple_of(ci * q, q)
        y = y_scrs[0][pl.ds(r0, q), :] + y_scrs[1][pl.ds(r0, q), :]
        y = y + dsk_ref[...] * xs_ref[pl.ds(r0, q), :].astype(F32)
        y = y * sz_ref[pl.ds(r0, q), :].astype(F32)
        y = y * lax.rsqrt(jnp.mean(y * y, axis=-1, keepdims=True) + EPS)
        y_ref[pl.ds(r0, q), :] = (y * nrm_ref[...]).astype(y_ref.dtype)
        return carry

    lax.fori_loop(0, n_chunks, finish_body, 0)


def _ssd_scan(xbc, dt, sz, a_neg, dsk, nrm, s0, *, row0, n_seq, seq_len, want_final):
    assert row0 % seq_len == 0
    b0 = row0 // seq_len
    has_init = s0 is not None
    gw = GROUP_W
    b_blk0 = D_MODEL // D_STATE
    c_blk0 = b_blk0 + SSD_GROUPS
    in_specs = [
        pl.BlockSpec((seq_len, gw), lambda b, g: (b0 + b, g)),
        pl.BlockSpec((seq_len, D_STATE), lambda b, g: (b0 + b, b_blk0 + g)),
        pl.BlockSpec((seq_len, D_STATE), lambda b, g: (b0 + b, c_blk0 + g)),
        pl.BlockSpec((seq_len, LANES), lambda b, g: (b0 + b, g)),
        pl.BlockSpec((seq_len, gw), lambda b, g: (b0 + b, g)),
        pl.BlockSpec((None, 1, LANES), lambda b, g: (g, 0, 0)),
        pl.BlockSpec((None, 1, gw), lambda b, g: (g, 0, 0)),
        pl.BlockSpec((None, 1, gw), lambda b, g: (g, 0, 0)),
    ]
    args = [xbc, xbc, xbc, dt, sz, a_neg, dsk, nrm]
    if has_init:
        in_specs.append(pl.BlockSpec((None, None, 2, D_STATE, gw), lambda b, g: (b, g, 0, 0, 0)))
        args.append(s0)
    out_shape = [jax.ShapeDtypeStruct((n_seq * seq_len, D_MODEL), BF16)]
    out_specs = [pl.BlockSpec((seq_len, gw), lambda b, g: (b, g))]
    if want_final:
        out_shape.append(jax.ShapeDtypeStruct((n_seq, SSD_GROUPS, 2, D_STATE, gw), F32))
        out_specs.append(pl.BlockSpec((None, None, 2, D_STATE, gw), lambda b, g: (b, g, 0, 0, 0)))
    kern = functools.partial(_ssd_kernel, n_chunks=seq_len // CHUNK, has_init=has_init, has_final=want_final)
    return pl.pallas_call(
        kern,
        grid=(n_seq, SSD_GROUPS),
        in_specs=in_specs,
        out_specs=out_specs,
        out_shape=out_shape,
        scratch_shapes=[pltpu.VMEM((seq_len, gw), F32), pltpu.VMEM((seq_len, gw), F32),
                        pltpu.VMEM((D_STATE, gw), F32), pltpu.VMEM((D_STATE, gw), F32)],
        compiler_params=_cparams(("arbitrary", "arbitrary")),
        name="ssd_scan_ctx" if want_final else "ssd_scan_dec",
    )(*args)


def _ret_kernel(*refs, n_chunks, has_init, has_final):
    lg_ref, q_ref, k_ref, v_ref, sg_ref = refs[:5]
    k = 5
    s0_ref = None
    if has_init:
        s0_ref = refs[k]
        k += 1
    y_ref = refs[k]
    k += 1
    sfin_ref = None
    if has_final:
        sfin_ref = refs[k]
        k += 1
    y_scrs = (refs[k], refs[k + 1])
    s_scrs = (refs[k + 2], refs[k + 3])

    qn = RET_CHUNK
    h = pl.program_id(1)
    row = lax.broadcasted_iota(jnp.int32, (qn, qn), 0)
    col = lax.broadcasted_iota(jnp.int32, (qn, qn), 1)
    rowk = lax.broadcasted_iota(jnp.int32, (qn, DK_RET), 0).astype(F32)

    def tables(direction):
        lg = lg_ref[direction, h]
        if direction == 0:
            keep = col <= row
            dist = (row - col).astype(F32)
            e_q = jnp.exp(lg * (rowk + 1.0))
            w_k = jnp.exp(lg * (qn - 1.0 - rowk))
        else:
            keep = col >= row
            dist = (col - row).astype(F32)
            e_q = jnp.exp(lg * (qn - rowk))
            w_k = jnp.exp(lg * rowk)
        dmat = jnp.where(keep, jnp.exp(lg * dist), 0.0)
        return dmat, e_q, w_k, jnp.exp(jnp.full((1, 1), float(qn), F32) * lg)

    def chunk(ci, direction, tabs):
        dmat, e_q, w_k, dec = tabs
        s_scr = s_scrs[direction]
        r0 = pl.multiple_of(ci * qn, qn)
        qc = q_ref[pl.ds(r0, qn), :]
        kc = k_ref[pl.ds(r0, qn), :]
        vc = v_ref[pl.ds(r0, qn), :]
        scores = lax.dot_general(qc, kc, (((1,), (1,)), ((), ())), preferred_element_type=F32) * dmat
        s_old = s_scr[...]
        lhs = jnp.concatenate([scores.astype(BF16), (qc.astype(F32) * e_q).astype(BF16)], axis=1)
        rhs = jnp.concatenate([vc, s_old.astype(BF16)], axis=0)
        y = jnp.dot(lhs, rhs, preferred_element_type=F32)
        kw_t = (kc.astype(F32) * w_k).T.astype(BF16)
        s_scr[...] = s_old * dec + jnp.dot(kw_t, vc, preferred_element_type=F32)
        return y

    for direction in range(2):
        if has_init:
            s_scrs[direction][...] = s0_ref[direction]
        else:
            s_scrs[direction][...] = jnp.zeros_like(s_scrs[direction])
    tabs = (tables(0), tables(1))

    def scan_body(j, carry):
        for direction, ci in ((0, j), (1, n_chunks - 1 - j)):
            r0 = pl.multiple_of(ci * qn, qn)
            y_scrs[direction][pl.ds(r0, qn), :] = chunk(ci, direction, tabs[direction])
        return carry

    lax.fori_loop(0, n_chunks, scan_body, 0)
    if has_final:
        for direction in range(2):
            sfin_ref[direction] = s_scrs[direction][...]

    def finish_body(ci, carry):
        r0 = pl.multiple_of(ci * qn, qn)
        y = y_scrs[0][pl.ds(r0, qn), :] + y_scrs[1][pl.ds(r0, qn), :]
        y = y * lax.rsqrt(jnp.mean(y * y, axis=-1, keepdims=True) + EPS)
        y_ref[pl.ds(r0, qn), :] = (y * sg_ref[pl.ds(r0, qn), :].astype(F32)).astype(y_ref.dtype)
        return carry

    lax.fori_loop(0, n_chunks, finish_body, 0)


def _ret_scan(log_g, q, k, v, sg, s0, *, row0, n_seq, seq_len, want_final):
    assert row0 % seq_len == 0
    b0 = row0 // seq_len
    has_init = s0 is not None
    in_specs = [
        pl.BlockSpec(memory_space=pltpu.SMEM),
        pl.BlockSpec((seq_len, DK_RET), lambda b, h: (b0 + b, h)),
        pl.BlockSpec((seq_len, DK_RET), lambda b, h: (b0 + b, h)),
        pl.BlockSpec((seq_len, DV_RET), lambda b, h: (b0 + b, h)),
        pl.BlockSpec((seq_len, DV_RET), lambda b, h: (b0 + b, h)),
    ]
    args = [log_g, q, k, v, sg]
    if has_init:
        in_specs.append(pl.BlockSpec((None, 2, None, DK_RET, DV_RET), lambda b, h: (b, 0, h, 0, 0)))
        args.append(s0)
    out_shape = [jax.ShapeDtypeStruct((n_seq * seq_len, H_RET * DV_RET), BF16)]
    out_specs = [pl.BlockSpec((seq_len, DV_RET), lambda b, h: (b, h))]
    if want_final:
        out_shape.append(jax.ShapeDtypeStruct((n_seq, 2, H_RET, DK_RET, DV_RET), F32))
        out_specs.append(pl.BlockSpec((None, 2, None, DK_RET, DV_RET), lambda b, h: (b, 0, h, 0, 0)))
    assert seq_len % RET_CHUNK == 0
    kern = functools.partial(_ret_kernel, n_chunks=seq_len // RET_CHUNK, has_init=has_init, has_final=want_final)
    return pl.pallas_call(
        kern,
        grid=(n_seq, H_RET),
        in_specs=in_specs,
        out_specs=out_specs,
        out_shape=out_shape,
        scratch_shapes=[pltpu.VMEM((seq_len, DV_RET), F32), pltpu.VMEM((seq_len, DV_RET), F32),
                        pltpu.VMEM((DK_RET, DV_RET), F32), pltpu.VMEM((DK_RET, DV_RET), F32)],
        compiler_params=_cparams(("arbitrary", "arbitrary")),
        name="ret_scan_ctx" if want_final else "ret_scan_dec",
    )(*args)


def _mixer_rows(tok, bm):
    n_ctx_tiles = tok.t_ctx // bm

    def specs(width):
        return [pl.BlockSpec((bm, width), lambda i: (jnp.minimum(i, n_ctx_tiles - 1), 0)),
                pl.BlockSpec((bm, width), lambda i: (jnp.maximum(i - n_ctx_tiles, 0), 0))]

    def select(yc_ref, yd_ref):
        return jnp.where(pl.program_id(0) < n_ctx_tiles, yc_ref[...], yd_ref[...])

    return specs, select


def _out_a_kernel(xc_ref, xd_ref, gate_ref, yc_ref, yd_ref, u_ref, v_ref, wsp_ref, bsp_ref, w1_ref, w2_ref, o_ref,
                  sgu_scr, *, select):
    bm = u_ref.shape[0]
    for ci in range(bm // CHUNK):
        rows = slice(ci * CHUNK, (ci + 1) * CHUNK)
        for g in range(SG_GROUPS):
            cols = slice(g * LANES, (g + 1) * LANES)
            mix = jnp.dot(wsp_ref[g], v_ref[rows, cols], preferred_element_type=F32) + bsp_ref[:, cols]
            sgu_scr[rows, cols] = (u_ref[rows, cols].astype(F32) * mix).astype(BF16)
    out = jnp.dot(select(yc_ref, yd_ref), w1_ref[...], preferred_element_type=F32)
    out = out + jnp.dot(sgu_scr[...], w2_ref[...], preferred_element_type=F32)
    o_ref[...] = select(xc_ref, xd_ref) + gate_ref[...] * out


def _out_proj_a(x_ctx, x_dec, gate, y_ctx, y_dec, u, v, w_sp, b_full, w1, w2, tok, *, bm):
    t, d = tok.total, x_ctx.shape[1]
    group = tok.group_of_tile(bm)
    y_specs, select = _mixer_rows(tok, bm)
    return pl.pallas_call(
        functools.partial(_out_a_kernel, select=select),
        grid=(t // bm,),
        in_specs=[
            *y_specs(d),
            pl.BlockSpec((None, 1, d), lambda i: (group(i), 0, 0)),
            *y_specs(d),
            pl.BlockSpec((bm, d), lambda i: (i, 0)),
            pl.BlockSpec((bm, d), lambda i: (i, 0)),
            pl.BlockSpec((SG_GROUPS, CHUNK, CHUNK), lambda i: (0, 0, 0)),
            pl.BlockSpec((CHUNK, d), lambda i: (0, 0)),
            pl.BlockSpec((d, d), lambda i: (0, 0)),
            pl.BlockSpec((d, d), lambda i: (0, 0)),
        ],
        out_specs=pl.BlockSpec((bm, d), lambda i: (i, 0)),
        out_shape=jax.ShapeDtypeStruct((t, d), F32),
        scratch_shapes=[pltpu.VMEM((bm, d), BF16)],
        compiler_params=_cparams(("arbitrary",)),
        name="out_proj_a",
    )(x_ctx, x_dec, gate, y_ctx, y_dec, u, v, w_sp, b_full, w1, w2)


def _out_c_kernel(x_ref, gate_ref, yc_ref, yd_ref, w_ref, o_ref, *, select):
    out = jnp.dot(select(yc_ref, yd_ref), w_ref[...], preferred_element_type=F32)
    o_ref[...] = x_ref[...] + gate_ref[...] * out


def _out_proj_c(x, gate, y_ctx, y_dec, w, tok, *, bm):
    t, d = x.shape
    kdim = y_ctx.shape[1]
    group = tok.group_of_tile(bm)
    y_specs, select = _mixer_rows(tok, bm)
    return pl.pallas_call(
        functools.partial(_out_c_kernel, select=select),
        grid=(t // bm,),
        in_specs=[
            pl.BlockSpec((bm, d), lambda i: (i, 0)),
            pl.BlockSpec((None, 1, d), lambda i: (group(i), 0, 0)),
            *y_specs(kdim),
            pl.BlockSpec((kdim, d), lambda i: (0, 0)),
        ],
        out_specs=pl.BlockSpec((bm, d), lambda i: (i, 0)),
        out_shape=jax.ShapeDtypeStruct((t, d), F32),
        compiler_params=_cparams(("arbitrary",)),
        name="out_proj_c",
    )(x, gate, y_ctx, y_dec, w)


def _split_kernel(w_ref, g_ref, l_ref):
    w = w_ref[...]
    k, n2 = w.shape
    half = LANES // 2
    lane = lax.broadcasted_iota(jnp.int32, (k, LANES), 1)
    first = lane < half
    idx = jnp.where(first, 2 * lane, 2 * (lane - half) + 1)
    gs, ls = [], []
    for j in range(n2 // (2 * LANES)):
        a = jnp.take_along_axis(w[:, (2 * j) * LANES:(2 * j + 1) * LANES], idx, axis=1)
        b = jnp.take_along_axis(w[:, (2 * j + 1) * LANES:(2 * j + 2) * LANES], idx, axis=1)
        gs.append(jnp.where(first, a, pltpu.roll(b, half, axis=1)))
        ls.append(jnp.where(first, pltpu.roll(a, half, axis=1), b))
    g_ref[...] = jnp.concatenate(gs, axis=1).astype(BF16)
    l_ref[...] = jnp.concatenate(ls, axis=1).astype(BF16)


def _split_gate_lin(w_gu):
    dl, e, k, n2 = w_gu.shape
    tn = 512
    spec_out = pl.BlockSpec((None, None, k, tn), lambda a, b, j: (a, b, 0, j))
    return pl.pallas_call(
        _split_kernel,
        grid=(dl, e, n2 // (2 * tn)),
        in_specs=[pl.BlockSpec((None, None, k, 2 * tn), lambda a, b, j: (a, b, 0, j))],
        out_specs=[spec_out, spec_out],
        out_shape=[jax.ShapeDtypeStruct((dl, e, k, n2 // 2), BF16)] * 2,
        compiler_params=_cparams(("arbitrary",) * 3),
        name="split_gate_lin",
    )(w_gu)


def _router_kernel(x_ref, gam_ref, sc_ref, sh_ref, wr_ref, br_ref, h_ref, e_ref, r_ref, g_ref, cnt_ref, cnt_scr):
    i = pl.program_id(0)
    bm = x_ref.shape[0]

    @pl.when(i == 0)
    def _():
        cnt_scr[...] = jnp.zeros_like(cnt_scr)

    h = _modulated_norm(x_ref[...], gam_ref[...], sc_ref[...], sh_ref[...])
    h_ref[...] = h
    w = wr_ref[...]
    h_hi, w_hi = h.astype(BF16), w.astype(BF16)
    h_lo = (h - h_hi.astype(F32)).astype(BF16)
    w_lo = (w - w_hi.astype(F32)).astype(BF16)
    logits = (jnp.dot(h_hi, w_hi, preferred_element_type=F32) + jnp.dot(h_hi, w_lo, preferred_element_type=F32)
              + jnp.dot(h_lo, w_hi, preferred_element_type=F32)) + br_ref[...]
    lane = lax.broadcasted_iota(jnp.int32, logits.shape, 1).astype(F32)
    vals, idxs = [], []
    work = logits
    for _ in range(TOP_K):
        m = jnp.max(work, axis=-1, keepdims=True)
        idx = jnp.min(jnp.where(work == m, lane, float(LANES)), axis=-1, keepdims=True)
        vals.append(m)
        idxs.append(idx)
        work = jnp.where(lane == idx, -jnp.inf, work)
    exps = [jnp.exp(v - vals[0]) for v in vals]
    inv = 1.0 / functools.reduce(lambda a, b: a + b, exps)
    hot = functools.reduce(jnp.logical_or, [lane == idx for idx in idxs])
    hot_f = hot.astype(F32)
    ri = lax.broadcasted_iota(jnp.int32, (bm, bm), 0)
    ci = lax.broadcasted_iota(jnp.int32, (bm, bm), 1)
    before = (ci < ri).astype(BF16)
    rank_all = cnt_scr[...] + jnp.dot(before, hot_f.astype(BF16), preferred_element_type=F32)
    e_out = jnp.zeros(logits.shape, F32)
    r_out = jnp.zeros(logits.shape, F32)
    g_out = jnp.zeros(logits.shape, F32)
    for k in range(TOP_K):
        rk = jnp.sum(jnp.where(lane == idxs[k], rank_all, 0.0), axis=-1, keepdims=True)
        e_out = jnp.where(lane == float(k), idxs[k], e_out)
        r_out = jnp.where(lane == float(k), rk, r_out)
        g_out = jnp.where(lane == float(k), exps[k] * inv, g_out)
    e_ref[...] = e_out.astype(jnp.int32)
    r_ref[...] = r_out.astype(jnp.int32)
    g_ref[...] = g_out
    cnt_scr[...] = cnt_scr[...] + jnp.sum(hot_f, axis=0, keepdims=True)
    cnt_ref[...] = cnt_scr[...]


def _router(x, gam, sc, sh, w_r, b_r, tok, *, bm):
    t, d = x.shape
    group = tok.group_of_tile(bm)
    row_spec = pl.BlockSpec((bm, LANES), lambda i: (i, 0))
    return pl.pallas_call(
        _router_kernel,
        grid=(t // bm,),
        in_specs=[
            pl.BlockSpec((bm, d), lambda i: (i, 0)),
            pl.BlockSpec((1, d), lambda i: (0, 0)),
            pl.BlockSpec((None, 1, d), lambda i: (group(i), 0, 0)),
            pl.BlockSpec((None, 1, d), lambda i: (group(i), 0, 0)),
            pl.BlockSpec((d, LANES), lambda i: (0, 0)),
            pl.BlockSpec((1, LANES), lambda i: (0, 0)),
        ],
        out_specs=[pl.BlockSpec((bm, d), lambda i: (i, 0)), row_spec, row_spec, row_spec,
                   pl.BlockSpec((1, LANES), lambda i: (0, 0))],
        out_shape=[jax.ShapeDtypeStruct((t, d), F32), jax.ShapeDtypeStruct((t, LANES), jnp.int32),
                   jax.ShapeDtypeStruct((t, LANES), jnp.int32), jax.ShapeDtypeStruct((t, LANES), F32),
                   jax.ShapeDtypeStruct((1, LANES), F32)],
        scratch_shapes=[pltpu.VMEM((1, LANES), F32)],
        compiler_params=_cparams(("arbitrary",)),
        name="moe_router",
    )(x, gam, sc, sh, w_r, b_r)


SLABS = D_MODEL // LANES


def _to_row_tiles(ref, x):
    for s in range(SLABS):
        ref[:, s, :] = x[:, s * LANES:(s + 1) * LANES]


def _from_row_tiles(ref):
    return jnp.concatenate([ref[:, s, :] for s in range(SLABS)], axis=1)


def _dispatch_kernel(pe_ref, dest_ref, h_ref, xs_hbm, stage, zbuf, sem, zsem):
    bm = h_ref.shape[0]

    @pl.when(pl.program_id(0) == 0)
    def _():
        zbuf[...] = jnp.zeros_like(zbuf)

        def tail_copy(e):
            start = pl.multiple_of(pe_ref[e + 1] - MOE_ROWS, MOE_ROWS)
            return pltpu.make_async_copy(zbuf, xs_hbm.at[pl.ds(start, MOE_ROWS)], zsem)

        for e in range(N_EXPERTS):
            @pl.when(pe_ref[e + 1] > pe_ref[e])
            def _():
                tail_copy(e).start()
        for e in range(N_EXPERTS):
            @pl.when(pe_ref[e + 1] > pe_ref[e])
            def _():
                tail_copy(e).wait()

        def spare_copy(b):
            return pltpu.make_async_copy(zbuf, xs_hbm.at[pl.ds(pl.multiple_of(b * MOE_ROWS, MOE_ROWS), MOE_ROWS)], zsem)

        first_spare = pe_ref[N_EXPERTS] // MOE_ROWS
        n_blk = xs_hbm.shape[0] // MOE_ROWS
        lax.fori_loop(first_spare, n_blk, lambda b, c: (spare_copy(b).start(), c)[1], 0)
        lax.fori_loop(first_spare, n_blk, lambda b, c: (spare_copy(b).wait(), c)[1], 0)

    i = pl.program_id(0)
    slot = i % 2
    _to_row_tiles(stage.at[slot], h_ref[...])

    def issue(r, carry):
        for k in range(TOP_K):
            pltpu.make_async_copy(stage.at[slot, r], xs_hbm.at[dest_ref[r * TOP_K + k]],
                                  sem.at[slot]).start(priority=k % 2)
        return carry

    lax.fori_loop(0, bm, issue, 0, unroll=2)

    def drain(s):
        for k in range(TOP_K):
            pltpu.make_async_copy(stage.at[s], xs_hbm.at[pl.ds(0, bm)], sem.at[s]).wait()

    @pl.when(i >= 1)
    def _():
        drain(1 - slot)

    @pl.when(i == pl.num_programs(0) - 1)
    def _():
        drain(slot)


def _dispatch(pad_bounds, dest_tiles, h, n_rows, *, bm):
    t, d = h.shape
    n_tiles = t // bm
    grid_spec = pltpu.PrefetchScalarGridSpec(
        num_scalar_prefetch=1,
        grid=(n_tiles,),
        in_specs=[
            pl.BlockSpec((None, None, bm * TOP_K), lambda i, pe: (i, 0, 0), memory_space=pltpu.SMEM),
            pl.BlockSpec((bm, d), lambda i, pe: (i, 0)),
        ],
        out_specs=pl.BlockSpec(memory_space=pl.ANY),
        scratch_shapes=[pltpu.VMEM((2, bm, SLABS, LANES), F32), pltpu.VMEM((MOE_ROWS, SLABS, LANES), F32),
                        pltpu.SemaphoreType.DMA((2,)), pltpu.SemaphoreType.DMA],
    )
    return pl.pallas_call(
        _dispatch_kernel,
        grid_spec=grid_spec,
        out_shape=jax.ShapeDtypeStruct((n_rows, SLABS, LANES), F32),
        compiler_params=_cparams(("arbitrary",)),
        name="moe_dispatch",
    )(pad_bounds, dest_tiles, h)


def _expert_kernel(be_ref, na_ref, xs_hbm, wg_ref, wl_ref, bg_ref, bl_ref, wd_ref, bd_ref, ys_hbm,
                   xbuf, ybuf, zbuf, wd_s, sem_in, sem_out, zsem):
    i = pl.program_id(0)
    n_act = na_ref[0]
    slot = i % 2

    def rows(b):
        return pl.ds(pl.multiple_of(b * MOE_ROWS, MOE_ROWS), MOE_ROWS)

    def in_copy(b, s, j):
        return pltpu.make_async_copy(xs_hbm.at[rows(b), j, :], xbuf.at[s, :, pl.ds(j * LANES, LANES)], sem_in.at[s])

    def out_copy(b, s, j):
        return pltpu.make_async_copy(ybuf.at[s, :, pl.ds(j * LANES, LANES)], ys_hbm.at[rows(b), j, :], sem_out.at[s])

    @pl.when(i < n_act)
    def _():
        @pl.when(i == 0)
        def _():
            for j in range(SLABS):
                in_copy(0, 0, j).start()

        @pl.when(i + 1 < n_act)
        def _():
            for j in range(SLABS):
                in_copy(i + 1, 1 - slot, j).start()

        for j in range(SLABS):
            in_copy(i, slot, j).wait()

        @pl.when(i >= 2)
        def _():
            for j in range(SLABS):
                out_copy(i - 2, slot, j).wait()

        @pl.when(jnp.logical_or(i == 0, be_ref[i] != be_ref[jnp.maximum(i - 1, 0)]))
        def _():
            wd_s[...] = wd_ref[...].astype(BF16)

        x = xbuf[slot].astype(BF16)
        hg = jnp.dot(x, wg_ref[...], preferred_element_type=F32) + bg_ref[...]
        hl = jnp.dot(x, wl_ref[...], preferred_element_type=F32) + bl_ref[...]
        glu = jnp.minimum(hg, SWIGLU_LIMIT)
        lin = jnp.clip(hl, -SWIGLU_LIMIT, SWIGLU_LIMIT)
        act = glu * (1.0 / (1.0 + jnp.exp(-SWIGLU_ALPHA * glu))) * (lin + 1.0)
        ybuf[slot] = jnp.dot(act.astype(BF16), wd_s[...], preferred_element_type=F32) + bd_ref[...]
        for j in range(SLABS):
            out_copy(i, slot, j).start()

        @pl.when(i == n_act - 1)
        def _():
            for j in range(SLABS):
                out_copy(i, slot, j).wait()

            @pl.when(i >= 1)
            def _():
                for j in range(SLABS):
                    out_copy(i - 1, 1 - slot, j).wait()

    @pl.when(i >= n_act)
    def _():
        @pl.when(i == n_act)
        def _():
            zbuf[...] = jnp.zeros_like(zbuf)

        fill = pltpu.make_async_copy(zbuf, ys_hbm.at[rows(i)], zsem)
        fill.start()
        fill.wait()


def _experts(blk_expert, n_active, xs, wg, wl, bg, bl, wd, bd, layer):
    n_rows = xs.shape[0]
    d, dff = wg.shape[2], wg.shape[3]
    n_blk = n_rows // MOE_ROWS
    wmap = lambda i, be, na: (layer, be[i], 0, 0)
    grid_spec = pltpu.PrefetchScalarGridSpec(
        num_scalar_prefetch=2,
        grid=(n_blk,),
        in_specs=[
            pl.BlockSpec(memory_space=pl.ANY),
            pl.BlockSpec((None, None, d, dff), wmap),
            pl.BlockSpec((None, None, d, dff), wmap),
            pl.BlockSpec((None, None, 1, dff), wmap),
            pl.BlockSpec((None, None, 1, dff), wmap),
            pl.BlockSpec((None, None, dff, d), wmap),
            pl.BlockSpec((None, None, 1, d), wmap),
        ],
        out_specs=pl.BlockSpec(memory_space=pl.ANY),
        scratch_shapes=[pltpu.VMEM((2, MOE_ROWS, d), F32), pltpu.VMEM((2, MOE_ROWS, d), F32),
                        pltpu.VMEM((MOE_ROWS, SLABS, LANES), F32), pltpu.VMEM((dff, d), BF16),
                        pltpu.SemaphoreType.DMA((2,)), pltpu.SemaphoreType.DMA((2,)), pltpu.SemaphoreType.DMA],
    )
    return pl.pallas_call(
        _expert_kernel,
        grid_spec=grid_spec,
        out_shape=jax.ShapeDtypeStruct((n_rows, SLABS, LANES), F32),
        compiler_params=_cparams(("arbitrary",)),
        name="moe_experts",
    )(blk_expert, n_active, xs, wg, wl, bg, bl, wd, bd)


def _combine_kernel(dest_ref, next_ref, g_ref, x_ref, gate_ref, gam_ref, ys_hbm, o_ref, *rest, final_norm, n_ctx_tiles):
    if final_norm:
        of_ref, buf, acc_scr, sem = rest
    else:
        buf, acc_scr, sem = rest
    bm = x_ref.shape[0]
    i = pl.program_id(0)
    slot = i % 2

    def gather(idx_ref, s):
        def issue(r, carry):
            for k in range(TOP_K):
                pltpu.make_async_copy(ys_hbm.at[idx_ref[r * TOP_K + k]], buf.at[s, k, r],
                                      sem.at[s]).start(priority=k % 2)
            return carry

        lax.fori_loop(0, bm, issue, 0, unroll=2)

    @pl.when(i == 0)
    def _():
        gather(dest_ref, 0)

    @pl.when(i + 1 < pl.num_programs(0))
    def _():
        gather(next_ref, 1 - slot)

    for k in range(TOP_K):
        pltpu.make_async_copy(ys_hbm.at[pl.ds(0, bm)], buf.at[slot, k], sem.at[slot]).wait()

    def mix(r, carry):
        acc = g_ref[r * TOP_K] * buf[slot, 0, r]
        for k in range(1, TOP_K):
            acc = acc + g_ref[r * TOP_K + k] * buf[slot, k, r]
        acc_scr[r] = acc
        return carry

    lax.fori_loop(0, bm, mix, 0, unroll=8)
    xn = x_ref[...] + gate_ref[...] * _from_row_tiles(acc_scr)
    if not final_norm:
        o_ref[...] = xn
    else:
        ms = jnp.mean(xn * xn, axis=-1, keepdims=True)
        yn = xn * lax.rsqrt(ms + EPS) * gam_ref[...]
        is_ctx = pl.program_id(0) < n_ctx_tiles

        @pl.when(is_ctx)
        def _():
            o_ref[...] = yn

        @pl.when(jnp.logical_not(is_ctx))
        def _():
            of_ref[...] = yn


def _combine(dest_tiles, x, gate, gates, gam_final, ys, tok, *, bm, final_norm):
    t, d = x.shape
    group = tok.group_of_tile(bm)
    n_tiles = t // bm
    n_ctx_tiles = tok.t_ctx // bm
    cur_spec = pl.BlockSpec((None, None, bm * TOP_K), lambda i: (i, 0, 0), memory_space=pltpu.SMEM)
    nxt_spec = pl.BlockSpec((None, None, bm * TOP_K), lambda i: (jnp.minimum(i + 1, n_tiles - 1), 0, 0),
                            memory_space=pltpu.SMEM)
    if final_norm:
        out_shape = [jax.ShapeDtypeStruct((tok.t_ctx, d), F32), jax.ShapeDtypeStruct((tok.t_dec, d), F32)]
        out_specs = [pl.BlockSpec((bm, d), lambda i: (jnp.minimum(i, n_ctx_tiles - 1), 0)),
                     pl.BlockSpec((bm, d), lambda i: (jnp.maximum(i - n_ctx_tiles, 0), 0))]
    else:
        out_shape = [jax.ShapeDtypeStruct((t, d), F32)]
        out_specs = [pl.BlockSpec((bm, d), lambda i: (i, 0))]
    return pl.pallas_call(
        functools.partial(_combine_kernel, final_norm=final_norm, n_ctx_tiles=n_ctx_tiles),
        grid=(n_tiles,),
        in_specs=[
            cur_spec, nxt_spec, cur_spec,
            pl.BlockSpec((bm, d), lambda i: (i, 0)),
            pl.BlockSpec((None, 1, d), lambda i: (group(i), 0, 0)),
            pl.BlockSpec((1, d), lambda i: (0, 0)),
            pl.BlockSpec(memory_space=pl.ANY),
        ],
        out_specs=out_specs,
        out_shape=out_shape,
        scratch_shapes=[pltpu.VMEM((2, TOP_K, bm, SLABS, LANES), F32), pltpu.VMEM((bm, SLABS, LANES), F32),
                        pltpu.SemaphoreType.DMA((2,))],
        compiler_params=_cparams(("arbitrary",)),
        name="moe_combine_final" if final_norm else "moe_combine",
    )(dest_tiles, dest_tiles, gates[:, :TOP_K].reshape(n_tiles, 1, bm * TOP_K), x, gate, gam_final, ys)


def _moe(x, gam, sc, sh, gate, w_r, b_r, wg, wl, bg, bl, wd, bd, layer, gam_final, tok, *, final_norm):
    t, d = x.shape
    h, e_out, r_out, gates, counts = _router(x, gam, sc, sh, w_r, b_r, tok, bm=512)
    counts = counts[0, :N_EXPERTS].astype(jnp.int32)
    padded = (counts + MOE_ROWS - 1) // MOE_ROWS * MOE_ROWS
    pad_end = jnp.cumsum(padded)
    n_rows = t * TOP_K + N_EXPERTS * MOE_ROWS
    n_blk = n_rows // MOE_ROWS
    blk_start = jnp.arange(n_blk, dtype=jnp.int32) * MOE_ROWS
    blk_expert = jnp.minimum(jnp.sum((pad_end[None, :] <= blk_start[:, None]).astype(jnp.int32), axis=1),
                             N_EXPERTS - 1)
    n_active = (pad_end[-1:] // MOE_ROWS).astype(jnp.int32)
    pad_bounds = jnp.concatenate([jnp.zeros((1,), jnp.int32), pad_end.astype(jnp.int32)])
    bm = 256
    dest = pad_bounds[e_out[:, :TOP_K]] + r_out[:, :TOP_K]
    dest_tiles = dest.reshape(t // bm, 1, bm * TOP_K)
    xs = _dispatch(pad_bounds, dest_tiles, h, n_rows, bm=bm)
    ys = _experts(blk_expert, n_active, xs, wg, wl, bg, bl, wd, bd, layer)
    return _combine(dest_tiles, x, gate, gates, gam_final, ys, tok, bm=bm, final_norm=final_norm)


def _rope_tables(tok):
    rows = tok.dec_len // GRID_W
    r = jnp.repeat(jnp.arange(rows), GRID_W).astype(F32)
    cidx = jnp.tile(jnp.arange(GRID_W), rows).astype(F32)
    n_freq = DK_RET // 4
    inv = ROPE_BASE ** (-jnp.arange(n_freq, dtype=F32) / n_freq)
    ang = jnp.concatenate([r[:, None] * inv, cidx[:, None] * inv], axis=-1)
    cos = jnp.concatenate([jnp.ones((tok.t_ctx, DK_RET // 2), F32), jnp.tile(jnp.cos(ang), (tok.n_dec_seq, 1))])
    sin = jnp.concatenate([jnp.zeros((tok.t_ctx, DK_RET // 2), F32), jnp.tile(jnp.sin(ang), (tok.n_dec_seq, 1))])
    return cos, sin


def _group_cols(p):
    h = p.shape[1]
    a = p.reshape(2, SSD_GROUPS, h // SSD_GROUPS).transpose(1, 0, 2).reshape(SSD_GROUPS, -1)
    return jnp.pad(a, ((0, 0), (0, LANES - a.shape[1])))[:, None, :]


def kernel(x_prompt, x_sample, state_ssd, state_ret, c, c_ctx, w_mod, b_mod, norm_mix, norm_ffn, w_in_a, conv_w, conv_b, dt_bias, a_log, d_skip, ssd_norm, w_sp, b_sp, w_out_a, w_in_c, decay_logit, w_out_c, w_router, b_router, w_gu, b_gu, w_down, b_down, norm_final):
    n_ctx, ctx_len, d = x_prompt.shape
    n_dec, dec_len, _ = x_sample.shape
    tok = _Tokens(n_ctx, ctx_len, n_dec, dec_len)
    depth = w_mod.shape[0]
    x = (x_prompt.reshape(tok.t_ctx, d), x_sample.reshape(tok.t_dec, d))

    cvecs = jnp.concatenate([c_ctx[None], c, jnp.zeros((MOD_ROWS - 1 - n_dec, d), F32)])
    mod = _modulation(cvecs, w_mod, b_mod)
    mod = mod.reshape(depth, MOD_ROWS, 6, 1, d).transpose(0, 2, 1, 3, 4)

    h_ssd = a_log.shape[2]
    xbc_w = d + 2 * SSD_GROUPS * D_STATE
    o1, o2, o3 = d, d + xbc_w, d + xbc_w + 2 * h_ssd
    cos, sin = _rope_tables(tok)
    wg_all, wl_all = _split_gate_lin(w_gu)
    bg_all = b_gu[:, :, None, 0::2]
    bl_all = b_gu[:, :, None, 1::2]
    wd_all = w_down
    bd_all = b_down[:, :, None, :]
    new_ssd, new_ret = [], []
    y_final = None
    bm_proj = 512

    for l in range(depth):
        sh1, sc1, g1, sh2, sc2, g2 = (mod[l, j] for j in range(6))
        gam_mix = norm_mix[l][None]
        i = l // 2
        if l % 2 == 0:
            w_in = w_in_a[i]
            w_z = w_in[:, :o1].astype(BF16)
            w_xbc = w_in[:, o1:o2].astype(BF16)
            w_dt = w_in[:, o2:o3].reshape(d, 2, SSD_GROUPS, h_ssd // SSD_GROUPS).transpose(0, 2, 1, 3)
            w_dt = jnp.pad(w_dt.reshape(d, SSD_GROUPS, -1), ((0, 0), (0, 0), (0, LANES - 2 * h_ssd // SSD_GROUPS)))
            w_dt = w_dt.reshape(d, SSD_GROUPS * LANES).astype(BF16)
            w_uv = w_in[:, o3:].astype(BF16)
            xbc = _norm_proj(x, gam_mix, sc1, sh1, w_xbc, tok=tok, bm=dec_len, tn=256, out_dtype=BF16,
                             epilogue=_ep_conv_silu, extra=(conv_w[i], conv_b[i][None]),
                             extra_specs=(pl.BlockSpec((4, 256), lambda r, j: (0, j)),
                                          pl.BlockSpec((1, 256), lambda r, j: (0, j))), name="proj_xbc")
            dtb = _group_cols(dt_bias[i]).reshape(1, SSD_GROUPS * LANES)
            sz, dt, u, v = _norm_proj_multi(
                x, gam_mix, sc1, sh1, jnp.concatenate([w_z, w_dt, w_uv], axis=1), tok,
                [(d, 512, BF16, _ep_silu, False),
                 (SSD_GROUPS * LANES, SSD_GROUPS * LANES, F32, _ep_softplus_bias, True),
                 (d, 512, BF16, _ep_gelu, False), (d, d, BF16, _ep_gelu_ln, False)],
                bm=bm_proj, extra=(dtb,), extra_specs=(pl.BlockSpec((1, SSD_GROUPS * LANES), lambda r: (0, 0)),),
                name="proj_zdtuv")
            a_neg = _group_cols(-jnp.exp(a_log[i]))
            dsk = jnp.repeat(d_skip[i], SSD_HEAD).reshape(SSD_GROUPS, 1, GROUP_W)
            nrm = ssd_norm[i].reshape(SSD_GROUPS, 1, GROUP_W)
            s0 = state_ssd[:, i].reshape(n_dec, 2, SSD_GROUPS, HEADS_PER_GROUP, D_STATE, SSD_HEAD)
            s0 = s0.transpose(0, 2, 1, 4, 3, 5).reshape(n_dec, SSD_GROUPS, 2, D_STATE, GROUP_W)
            y_ctx, s_fin = _ssd_scan(xbc, dt, sz, a_neg, dsk, nrm, None, row0=0, n_seq=n_ctx,
                                     seq_len=ctx_len, want_final=True)
            (y_dec,) = _ssd_scan(xbc, dt, sz, a_neg, dsk, nrm, s0, row0=tok.t_ctx, n_seq=n_dec,
                                 seq_len=dec_len, want_final=False)
            s_fin = s_fin.reshape(n_ctx, SSD_GROUPS, 2, D_STATE, HEADS_PER_GROUP, SSD_HEAD)
            new_ssd.append(s_fin.transpose(0, 2, 1, 4, 3, 5).reshape(n_ctx, 2, h_ssd, D_STATE, SSD_HEAD))
            b_full = jnp.repeat(b_sp[i].T, LANES, axis=1)
            w_o = w_out_a[i].astype(BF16)
            x_pair = x if isinstance(x, tuple) else (x[:tok.t_ctx], x[tok.t_ctx:])
            x = _out_proj_a(*x_pair, g1, y_ctx, y_dec, u, v, w_sp[i].astype(BF16), b_full, w_o[:d], w_o[d:], tok,
                            bm=512)
        else:
            hk = H_RET * DK_RET
            hv = H_RET * DV_RET
            w_in = w_in_c[i].astype(BF16)
            rope_specs = (pl.BlockSpec((bm_proj, DK_RET // 2), lambda r: (r, 0)),) * 2
            q, kk, vv, sg = _norm_proj_multi(
                x, gam_mix, sc1, sh1, w_in, tok,
                [(hk, 512, BF16, functools.partial(_ep_rope, scale=1.0), True),
                 (hk, 512, BF16, functools.partial(_ep_rope, scale=DK_RET ** -0.5), True),
                 (hv, 512, BF16, _ep_plain, False), (hv, 512, BF16, _ep_silu, False)],
                bm=bm_proj, extra=(cos, sin), extra_specs=rope_specs, name="proj_qkvg")
            log_g = jax.nn.log_sigmoid(decay_logit[i].astype(F32))
            y_ctx, s_fin = _ret_scan(log_g, q, kk, vv, sg, None, row0=0, n_seq=n_ctx, seq_len=ctx_len, want_final=True)
            (y_dec,) = _ret_scan(log_g, q, kk, vv, sg, state_ret[:, i], row0=tok.t_ctx, n_seq=n_dec, seq_len=dec_len,
                                 want_final=False)
            new_ret.append(s_fin)
            x = _out_proj_c(x, g1, y_ctx, y_dec, w_out_c[i].astype(BF16), tok, bm=512)

        w_r = jnp.pad(w_router[l], ((0, 0), (0, LANES - N_EXPERTS)))
        b_r = jnp.pad(b_router[l], (0, LANES - N_EXPERTS), constant_values=-1e30)[None]
        last = l == depth - 1
        res = _moe(x, norm_ffn[l][None], sc2, sh2, g2, w_r, b_r, wg_all, wl_all, bg_all, bl_all, wd_all, bd_all, l,
                   norm_final[None], tok, final_norm=last)
        if last:
            y_final = res
        else:
            x = res[0]

    y_prompt = y_final[0].reshape(n_ctx, ctx_len, d)
    y_sample = y_final[1].reshape(n_dec, dec_len, d)
    return (y_prompt, y_sample, jnp.stack(new_ssd, axis=1), jnp.stack(new_ret, axis=1))
```
